```python
import jax, jax.numpy as jnp
from jax import lax
import numpy as np

D_MODEL = 2048
BATCH = 8
SEQ = 2048
DEPTH = 1

HEAD_DIM = 128
N_HEADS = D_MODEL // HEAD_DIM
N_HEADS_B = N_HEADS // 4
N_HEADS_A = N_HEADS - N_HEADS_B
DILATED_PAIRS = ((128, 1), (512, 4), (2048, 16))
N_GROUPS_A = len(DILATED_PAIRS)
HEADS_PER_GROUP_A = N_HEADS_A // N_GROUPS_A
GRID_W = 64
WIN_R = 8
WIN_C = 16
QKV_W = N_HEADS * HEAD_DIM
D_IN = 3 * QKV_W + 2 * D_MODEL
D_A_OUT = HEADS_PER_GROUP_A * HEAD_DIM
D_B_OUT = N_HEADS_B * HEAD_DIM
D_FF = 4 * D_MODEL
ROPE_THETA = 10000.0
EPS = 1e-6
NEG_INF = -1e30

kernel_name = "hybrid_dilated_neighbourhood_gated_encoder"


def rms_norm(x, g):
    x32 = x.astype(jnp.float32)
    y = x32 * lax.rsqrt(jnp.mean(x32 * x32, axis=-1, keepdims=True) + EPS)
    return (y * g.astype(jnp.float32)).astype(x.dtype)


def rope(x, seq_len):
    pos = jnp.arange(seq_len, dtype=jnp.float32)
    inv = ROPE_THETA ** (-jnp.arange(0, HEAD_DIM, 2, dtype=jnp.float32) / HEAD_DIM)
    ang = pos[:, None] * inv[None, :]
    cos = jnp.cos(ang)[None, :, None, :]
    sin = jnp.sin(ang)[None, :, None, :]
    x32 = x.astype(jnp.float32)
    x1, x2 = jnp.split(x32, 2, axis=-1)
    return jnp.concatenate([x1 * cos - x2 * sin, x2 * cos + x1 * sin], axis=-1).astype(x.dtype)


def dilated_window_attention(q, k, v, window, dil):
    B, S, H, E = q.shape
    half = window // (2 * dil)
    blk = half
    M = S // dil
    nb = -(-M // blk)
    Mp = nb * blk

    def to_sub(t):
        return t.reshape(B, M, dil, H, E).transpose(0, 2, 3, 1, 4)

    qs = jnp.pad(to_sub(q), ((0, 0), (0, 0), (0, 0), (0, Mp - M), (0, 0)))
    qs = qs.reshape(B, dil, H, nb, blk, E)
    pad_kv = ((0, 0), (0, 0), (0, 0), (half, Mp - M + half), (0, 0))
    ks = jnp.pad(to_sub(k), pad_kv)
    vs = jnp.pad(to_sub(v), pad_kv)
    kb_len = blk + 2 * half
    idx = (jnp.arange(nb) * blk)[:, None] + jnp.arange(kb_len)[None, :]
    kb = ks[:, :, :, idx]
    vb = vs[:, :, :, idx]
    kpos = idx - half
    qpos = (jnp.arange(nb) * blk)[:, None] + jnp.arange(blk)[None, :]
    valid = (kpos >= 0) & (kpos < M)
    mask = (jnp.abs(kpos[:, None, :] - qpos[:, :, None]) <= half) & valid[:, None, :]
    s = jnp.einsum('bdhnqe,bdhnke->bdhnqk', qs, kb,
                   preferred_element_type=jnp.float32) * (E ** -0.5)
    s = jnp.where(mask, s, NEG_INF)
    lse = jax.nn.logsumexp(s, axis=-1)
    p = jnp.exp(s - lse[..., None]).astype(v.dtype)
    o = jnp.einsum('bdhnqk,bdhnke->bdhnqe', p, vb)
    o = o.reshape(B, dil, H, Mp, E)[:, :, :, :M].transpose(0, 3, 1, 2, 4).reshape(B, S, H, E)
    lse = lse.reshape(B, dil, H, Mp)[..., :M].transpose(0, 3, 1, 2).reshape(B, S, H)
    return o, lse


def neighbourhood_attention(q, k, v, rpb):
    B, S, H, E = q.shape
    rows = S // GRID_W
    kr = min(WIN_R, rows)
    kc = WIN_C

    def to_grid(t):
        return t.reshape(B, rows, GRID_W, H, E).transpose(0, 3, 1, 2, 4)

    qg, kg, vg = to_grid(q), to_grid(k), to_grid(v)
    r = jnp.arange(rows)
    row_idx = jnp.clip(r - kr // 2, 0, rows - kr)[:, None] + jnp.arange(kr)[None, :]
    kb = kg[:, :, row_idx]
    vb = vg[:, :, row_idx]
    c = jnp.arange(GRID_W)
    col_start = jnp.clip(c - kc // 2, 0, GRID_W - kc)
    col_mask = (c[None, :] >= col_start[:, None]) & (c[None, :] < col_start[:, None] + kc)
    dr = row_idx - r[:, None] + (WIN_R - 1)
    dc = jnp.clip(c[None, :] - c[:, None], -(kc - 1), kc - 1) + (WIN_C - 1)
    bias = rpb[:, dr[:, None, :, None], dc[None, :, None, :]]
    s = jnp.einsum('bhrqe,bhrjke->bhrqjk', qg, kb,
                   preferred_element_type=jnp.float32) * (E ** -0.5)
    s = s + bias.astype(jnp.float32)[None]
    s = jnp.where(col_mask[:, None, :], s, NEG_INF)
    p = jax.nn.softmax(s.reshape(B, H, rows, GRID_W, kr * GRID_W), axis=-1)
    p = p.reshape(s.shape).astype(v.dtype)
    o = jnp.einsum('bhrqjk,bhrjke->bhrqe', p, vb)
    return o.transpose(0, 2, 3, 1, 4).reshape(B, S, H, E)


def setup_inputs(seed: int = 0) -> dict:
    key = jax.random.key(seed)
    ks = jax.random.split(key, 16)
    f32 = jnp.float32

    def nrm(k, shape, scale):
        return jax.random.normal(k, shape, f32) * scale

    return {
        "x": nrm(ks[0], (BATCH, SEQ, D_MODEL), 1.0),
        "norm_mix": 1.0 + nrm(ks[1], (DEPTH, D_MODEL), 0.05),
        "w_in": nrm(ks[2], (DEPTH, D_MODEL, D_IN), D_MODEL ** -0.5),
        "b_gate": nrm(ks[3], (DEPTH, 2 * D_MODEL), 0.1),
        "q_norm_a": 1.0 + nrm(ks[4], (DEPTH, HEAD_DIM), 0.05),
        "k_norm_a": 1.0 + nrm(ks[5], (DEPTH, HEAD_DIM), 0.05),
        "q_norm_b": 1.0 + nrm(ks[6], (DEPTH, HEAD_DIM), 0.05),
        "k_norm_b": 1.0 + nrm(ks[7], (DEPTH, HEAD_DIM), 0.05),
        "rpb_b": nrm(ks[8], (DEPTH, N_HEADS_B, 2 * WIN_R - 1, 2 * WIN_C - 1), 0.1),
        "w_proj_a": nrm(ks[9], (DEPTH, D_A_OUT, D_MODEL), D_A_OUT ** -0.5),
        "w_proj_b": nrm(ks[10], (DEPTH, D_B_OUT, D_MODEL), D_B_OUT ** -0.5),
        "w_out": nrm(ks[11], (DEPTH, D_MODEL, D_MODEL), D_MODEL ** -0.5),
        "norm_ffn": 1.0 + nrm(ks[12], (DEPTH, D_MODEL), 0.05),
        "w_up": nrm(ks[13], (DEPTH, D_MODEL, D_FF), D_MODEL ** -0.5),
        "w_down": nrm(ks[14], (DEPTH, D_FF, D_MODEL), D_FF ** -0.5),
    }


def reference(x, norm_mix, w_in, b_gate, q_norm_a, k_norm_a, q_norm_b, k_norm_b, rpb_b,
              w_proj_a, w_proj_b, w_out, norm_ffn, w_up, w_down):
    B, S, _ = x.shape
    h = x
    for l in range(DEPTH):
        xn = rms_norm(h, norm_mix[l])
        proj = xn @ w_in[l]
        q, k, v, gate = jnp.split(proj, [QKV_W, 2 * QKV_W, 3 * QKV_W], axis=-1)
        q = q.reshape(B, S, N_HEADS, HEAD_DIM)
        k = k.reshape(B, S, N_HEADS, HEAD_DIM)
        v = v.reshape(B, S, N_HEADS, HEAD_DIM)

        qa = rope(rms_norm(q[:, :, :N_HEADS_A], q_norm_a[l]), S)
        ka = rope(rms_norm(k[:, :, :N_HEADS_A], k_norm_a[l]), S)
        va = v[:, :, :N_HEADS_A]
        outs, lses = [], []
        for g, (win, dil) in enumerate(DILATED_PAIRS):
            sl = slice(g * HEADS_PER_GROUP_A, (g + 1) * HEADS_PER_GROUP_A)
            o_g, lse_g = dilated_window_attention(qa[:, :, sl], ka[:, :, sl], va[:, :, sl], win, dil)
            outs.append(o_g)
            lses.append(lse_g)
        wts = jax.nn.softmax(jnp.stack(lses, axis=0), axis=0)
        oa = jnp.einsum('gbsh,gbshe->bshe', wts,
                        jnp.stack(outs, axis=0).astype(jnp.float32)).astype(x.dtype)

        qb = rms_norm(q[:, :, N_HEADS_A:], q_norm_b[l])
        kb = rms_norm(k[:, :, N_HEADS_A:], k_norm_b[l])
        ob = neighbourhood_attention(qb, kb, v[:, :, N_HEADS_A:], rpb_b[l])

        ya = oa.reshape(B, S, D_A_OUT) @ w_proj_a[l]
        yb = ob.reshape(B, S, D_B_OUT) @ w_proj_b[l]
        ga, gb = jnp.split(jax.nn.sigmoid((gate + b_gate[l]).astype(jnp.float32)), 2, axis=-1)
        mixed = (ga * ya + gb * yb).astype(x.dtype)
        h = h + mixed @ w_out[l]

        hn = rms_norm(h, norm_ffn[l])
        u = jax.nn.relu(hn @ w_up[l])
        h = h + (u * u) @ w_down[l]
    return h
```

```python
import functools

import numpy as np
import jax
import jax.numpy as jnp
from jax import lax
from jax.experimental import pallas as pl
from jax.experimental.pallas import tpu as pltpu

D_MODEL = 2048
HEAD_DIM = 128
N_HEADS = 16
N_HEADS_A = 12
N_HEADS_B = 4
DILATED_PAIRS = ((128, 1), (512, 4), (2048, 16))
HEADS_PER_GROUP = 4
GROUP_W = HEADS_PER_GROUP * HEAD_DIM
GRID_W = 64
WIN_R = 8
WIN_C = 16
QKV_W = N_HEADS * HEAD_DIM
D_FF = 4 * D_MODEL
ROPE_THETA = 10000.0
EPS = 1e-6
NEG_INF = -1e30
SCALE = HEAD_DIM ** -0.5

N_PROJ_BLOCKS = 20
D_PROJ = N_PROJ_BLOCKS * GROUP_W
B_BLOCK0 = 9
GATE_BLOCK0 = 12

VMEM_LIMIT = 56 * 1024 * 1024

F32 = jnp.float32
BF16 = jnp.bfloat16


def _proj_column_order():
    cols = []
    for g in range(3):
        for t in range(3):
            base = t * QKV_W + g * GROUP_W
            cols.append(np.arange(base, base + GROUP_W))
    for t in range(3):
        base = t * QKV_W + N_HEADS_A * HEAD_DIM
        cols.append(np.arange(base, base + GROUP_W))
    cols.append(np.arange(3 * QKV_W, 3 * QKV_W + 2 * D_MODEL))
    return np.concatenate(cols)


def _rope_tables(seq):
    pos = np.arange(seq, dtype=np.float64)
    inv = ROPE_THETA ** (-np.arange(0, HEAD_DIM, 2, dtype=np.float64) / HEAD_DIM)
    ang = pos[:, None] * inv[None, :]
    cos = np.concatenate([np.cos(ang), np.cos(ang)], axis=-1)
    sin = np.concatenate([-np.sin(ang), np.sin(ang)], axis=-1)
    return jnp.asarray(cos, F32), jnp.asarray(sin, F32)


def _in_proj_kernel(x_ref, g_ref, w_ref, o_ref, xn_ref):
    @pl.when(pl.program_id(1) == 0)
    def _():
        x = x_ref[...]
        ms = jnp.mean(x * x, axis=-1, keepdims=True)
        xn_ref[...] = ((x * lax.rsqrt(ms + EPS)) * g_ref[...]).astype(BF16)

    o_ref[...] = jnp.dot(xn_ref[...], w_ref[...], preferred_element_type=F32).astype(o_ref.dtype)


def _in_proj(x2, gain, w, tm=1024, tn=1024):
    n_tok = x2.shape[0]
    return pl.pallas_call(
        _in_proj_kernel,
        name="in_proj",
        grid=(n_tok // tm, D_PROJ // tn),
        in_specs=[
            pl.BlockSpec((tm, D_MODEL), lambda i, j: (i, 0)),
            pl.BlockSpec((1, D_MODEL), lambda i, j: (0, 0)),
            pl.BlockSpec((D_MODEL, tn), lambda i, j: (0, j)),
        ],
        out_specs=pl.BlockSpec((tm, tn), lambda i, j: (i, j)),
        out_shape=jax.ShapeDtypeStruct((n_tok, D_PROJ), BF16),
        scratch_shapes=[pltpu.VMEM((tm, D_MODEL), BF16)],
        compiler_params=pltpu.CompilerParams(
            dimension_semantics=("parallel", "arbitrary"), vmem_limit_bytes=VMEM_LIMIT),
    )(x2, gain, w)


Q_BLK = 128


def _rms(x, g):
    ms = jnp.mean(x * x, axis=-1, keepdims=True)
    return (x * lax.rsqrt(ms + EPS)) * g


def _attn_a_kernel(q_ref, k_ref, v_ref, cos_ref, sin_ref, gq_ref, gk_ref, o_ref, lse_ref,
                   qn_ref, kn_ref, *, sub_len, win):
    n_blk = sub_len // Q_BLK
    gq = gq_ref[...]
    gk = gk_ref[...]

    def prep(n, c):
        r0 = pl.multiple_of(n * Q_BLK, Q_BLK)
        cos = cos_ref[pl.ds(r0, Q_BLK), :]
        sin = sin_ref[pl.ds(r0, Q_BLK), :]
        for h in range(HEADS_PER_GROUP):
            sl = slice(h * HEAD_DIM, (h + 1) * HEAD_DIM)
            qh = _rms(q_ref[pl.ds(r0, Q_BLK), sl].astype(F32), gq)
            qh = qh * cos + pltpu.roll(qh, HEAD_DIM // 2, 1) * sin
            qn_ref[pl.ds(r0, Q_BLK), sl] = (qh * SCALE).astype(BF16)
            kh = _rms(k_ref[pl.ds(r0, Q_BLK), sl].astype(F32), gk)
            kh = kh * cos + pltpu.roll(kh, HEAD_DIM // 2, 1) * sin
            kn_ref[pl.ds(r0, Q_BLK), sl] = kh.astype(BF16)
        return c

    lax.fori_loop(0, n_blk, prep, 0)

    half = 64
    rel0 = (lax.broadcasted_iota(jnp.int32, (Q_BLK, win), 1)
            - lax.broadcasted_iota(jnp.int32, (Q_BLK, win), 0))

    def block(n, c):
        q0 = pl.multiple_of(n * Q_BLK, Q_BLK)
        start = pl.multiple_of(jnp.clip(n * Q_BLK - half, 0, sub_len - win), half)
        valid = jnp.abs(rel0 + (start - q0)) <= half
        for h in range(HEADS_PER_GROUP):
            sl = slice(h * HEAD_DIM, (h + 1) * HEAD_DIM)
            q = qn_ref[pl.ds(q0, Q_BLK), sl]
            k = kn_ref[pl.ds(start, win), sl]
            v = v_ref[pl.ds(start, win), sl]
            s = lax.dot_general(q, k, (((1,), (1,)), ((), ())), preferred_element_type=F32)
            s = jnp.where(valid, s, NEG_INF)
            m = jnp.max(s, axis=-1, keepdims=True)
            p = jnp.exp(s - m)
            l = jnp.sum(p, axis=-1, keepdims=True)
            acc = jnp.dot(p.astype(BF16), v, preferred_element_type=F32)
            o_ref[pl.ds(q0, Q_BLK), sl] = (acc * (1.0 / l)).astype(o_ref.dtype)
            lse_ref[pl.ds(q0, Q_BLK), sl] = jnp.broadcast_to(m + jnp.log(l), (Q_BLK, HEAD_DIM))
        return c

    lax.fori_loop(0, n_blk, block, 0)


def _attn_a_group(proj, cos, sin, gq, gk, group, dil, batch, seq):
    sub_len = seq // dil
    win = min(2 * Q_BLK, sub_len)
    proj_v = proj.reshape(batch, sub_len, dil * D_PROJ)
    cos_v = cos.reshape(sub_len, dil * HEAD_DIM)
    sin_v = sin.reshape(sub_len, dil * HEAD_DIM)

    def qkv_spec(t):
        return pl.BlockSpec((None, sub_len, GROUP_W),
                            lambda b, r: (b, 0, r * N_PROJ_BLOCKS + 3 * group + t))

    rope_spec = pl.BlockSpec((sub_len, HEAD_DIM), lambda b, r: (0, r))
    gain_spec = pl.BlockSpec((1, HEAD_DIM), lambda b, r: (0, 0))
    out_spec = pl.BlockSpec((None, sub_len, GROUP_W), lambda b, r: (b, 0, r))
    o, lse = pl.pallas_call(
        functools.partial(_attn_a_kernel, sub_len=sub_len, win=win),
        name=f"attn_a{group}",
        grid=(batch, dil),
        in_specs=[qkv_spec(0), qkv_spec(1), qkv_spec(2), rope_spec, rope_spec, gain_spec, gain_spec],
        out_specs=[out_spec, out_spec],
        out_shape=[jax.ShapeDtypeStruct((batch, sub_len, dil * GROUP_W), BF16),
                   jax.ShapeDtypeStruct((batch, sub_len, dil * GROUP_W), F32)],
        scratch_shapes=[pltpu.VMEM((sub_len, GROUP_W), BF16), pltpu.VMEM((sub_len, GROUP_W), BF16)],
        compiler_params=pltpu.CompilerParams(
            dimension_semantics=("parallel", "arbitrary"), vmem_limit_bytes=VMEM_LIMIT),
    )(proj_v, proj_v, proj_v, cos_v, sin_v, gq, gk)
    return o.reshape(batch * seq, GROUP_W), lse.reshape(batch * seq, GROUP_W)


def _attn_b_kernel(q_ref, k_ref, v_ref, gq_ref, gk_ref, bias_ref, o_ref, qn_ref, kn_ref, *, rows):
    gq = gq_ref[...]
    gk = gk_ref[...]
    key_rows = min(WIN_R, rows)
    n_keys = key_rows * GRID_W

    def prep(n, c):
        r0 = pl.multiple_of(n * Q_BLK, Q_BLK)
        for h in range(N_HEADS_B):
            sl = slice(h * HEAD_DIM, (h + 1) * HEAD_DIM)
            qh = _rms(q_ref[pl.ds(r0, Q_BLK), sl].astype(F32), gq)
            qn_ref[pl.ds(r0, Q_BLK), sl] = (qh * SCALE).astype(BF16)
            kh = _rms(k_ref[pl.ds(r0, Q_BLK), sl].astype(F32), gk)
            kn_ref[pl.ds(r0, Q_BLK), sl] = kh.astype(BF16)
        return c

    lax.fori_loop(0, rows * GRID_W // Q_BLK, prep, 0)

    def row(r, c):
        rs = jnp.clip(r - key_rows // 2, 0, rows - key_rows)
        di = r - rs
        q0 = pl.multiple_of(r * GRID_W, GRID_W)
        k0 = pl.multiple_of(rs * GRID_W, GRID_W)
        for h in range(N_HEADS_B):
            sl = slice(h * HEAD_DIM, (h + 1) * HEAD_DIM)
            q = qn_ref[pl.ds(q0, GRID_W), sl]
            k = kn_ref[pl.ds(k0, n_keys), sl]
            v = v_ref[pl.ds(k0, n_keys), sl]
            s = lax.dot_general(q, k, (((1,), (1,)), ((), ())), preferred_element_type=F32)
            s = s + bias_ref[h, di]
            m = jnp.max(s, axis=-1, keepdims=True)
            p = jnp.exp(s - m)
            l = jnp.sum(p, axis=-1, keepdims=True)
            acc = jnp.dot(p.astype(BF16), v, preferred_element_type=F32)
            o_ref[pl.ds(q0, GRID_W), sl] = (acc * (1.0 / l)).astype(o_ref.dtype)
        return c

    lax.fori_loop(0, rows, row, 0)


def _neighbourhood_bias(rpb, rows):
    key_rows = min(WIN_R, rows)
    j = np.arange(key_rows)
    dr = j[None, :] - j[:, None] + (WIN_R - 1)
    c = np.arange(GRID_W)
    dc = np.clip(c[None, :] - c[:, None], -(WIN_C - 1), WIN_C - 1) + (WIN_C - 1)
    col_start = np.clip(c - WIN_C // 2, 0, GRID_W - WIN_C)
    col_mask = (c[None, :] >= col_start[:, None]) & (c[None, :] < col_start[:, None] + WIN_C)
    tab = rpb.astype(F32)[:, dr[:, None, :, None], dc[None, :, None, :]]
    tab = jnp.where(col_mask[None, None, :, None, :], tab, NEG_INF)
    return tab.reshape(rpb.shape[0], key_rows, GRID_W, key_rows * GRID_W)


def _attn_b(proj, gq, gk, bias, batch, seq):
    rows = seq // GRID_W
    proj_v = proj.reshape(batch, seq, D_PROJ)

    def qkv_spec(t):
        return pl.BlockSpec((None, seq, GROUP_W), lambda b: (b, 0, B_BLOCK0 + t))

    gain_spec = pl.BlockSpec((1, HEAD_DIM), lambda b: (0, 0))
    bias_spec = pl.BlockSpec(bias.shape, lambda b: (0, 0, 0, 0))
    o = pl.pallas_call(
        functools.partial(_attn_b_kernel, rows=rows),
        name="attn_b",
        grid=(batch,),
        in_specs=[qkv_spec(0), qkv_spec(1), qkv_spec(2), gain_spec, gain_spec, bias_spec],
        out_specs=pl.BlockSpec((None, seq, GROUP_W), lambda b: (b, 0, 0)),
        out_shape=jax.ShapeDtypeStruct((batch, seq, GROUP_W), BF16),
        scratch_shapes=[pltpu.VMEM((seq, GROUP_W), BF16), pltpu.VMEM((seq, GROUP_W), BF16)],
        compiler_params=pltpu.CompilerParams(
            dimension_semantics=("parallel",), vmem_limit_bytes=VMEM_LIMIT),
    )(proj_v, proj_v, proj_v, gq, gk, bias)
    return o.reshape(batch * seq, GROUP_W)


def _mix_out_kernel(o0_ref, o1_ref, o2_ref, l0_ref, l1_ref, l2_ref, ob_ref, ga_ref, gb_ref, bg_ref,
                    x_ref, pa_ref, pb_ref, wo_ref, h_ref):
    l0 = l0_ref[...]
    l1 = l1_ref[...]
    l2 = l2_ref[...]
    mx = jnp.maximum(jnp.maximum(l0, l1), l2)
    e0 = jnp.exp(l0 - mx)
    e1 = jnp.exp(l1 - mx)
    e2 = jnp.exp(l2 - mx)
    inv = 1.0 / (e0 + e1 + e2)
    oa = (e0 * inv) * o0_ref[...].astype(F32)
    oa = oa + (e1 * inv) * o1_ref[...].astype(F32)
    oa = oa + (e2 * inv) * o2_ref[...].astype(F32)
    ya = jnp.dot(oa.astype(BF16), pa_ref[...], preferred_element_type=F32)
    yb = jnp.dot(ob_ref[...], pb_ref[...], preferred_element_type=F32)
    ga = jax.nn.sigmoid(ga_ref[...].astype(F32) + bg_ref[:, :D_MODEL])
    gb = jax.nn.sigmoid(gb_ref[...].astype(F32) + bg_ref[:, D_MODEL:])
    mixed = (ga * ya + gb * yb).astype(BF16)
    h_ref[...] = x_ref[...] + jnp.dot(mixed, wo_ref[...], preferred_element_type=F32)


def _mix_out(o_groups, lse_groups, ob, proj, b_gate, x2, pa, pb, wo, tm=256):
    n_tok = x2.shape[0]
    row512 = pl.BlockSpec((tm, GROUP_W), lambda i: (i, 0))
    const = pl.Buffered(1)
    gate_blk = GATE_BLOCK0 * GROUP_W // D_MODEL
    return pl.pallas_call(
        _mix_out_kernel,
        name="mix_out",
        grid=(n_tok // tm,),
        in_specs=[row512, row512, row512, row512, row512, row512, row512,
                  pl.BlockSpec((tm, D_MODEL), lambda i: (i, gate_blk)),
                  pl.BlockSpec((tm, D_MODEL), lambda i: (i, gate_blk + 1)),
                  pl.BlockSpec((1, 2 * D_MODEL), lambda i: (0, 0)),
                  pl.BlockSpec((tm, D_MODEL), lambda i: (i, 0)),
                  pl.BlockSpec((GROUP_W, D_MODEL), lambda i: (0, 0), pipeline_mode=const),
                  pl.BlockSpec((GROUP_W, D_MODEL), lambda i: (0, 0), pipeline_mode=const),
                  pl.BlockSpec((D_MODEL, D_MODEL), lambda i: (0, 0), pipeline_mode=const)],
        out_specs=pl.BlockSpec((tm, D_MODEL), lambda i: (i, 0)),
        out_shape=jax.ShapeDtypeStruct((n_tok, D_MODEL), F32),
        compiler_params=pltpu.CompilerParams(
            dimension_semantics=("parallel",), vmem_limit_bytes=VMEM_LIMIT),
    )(*o_groups, *lse_groups, ob, proj, proj, b_gate, x2, pa, pb, wo)


def _ffn_kernel(h_ref, g_ref, wu_ref, wd_ref, o_ref, hn_ref, acc_ref):
    f = pl.program_id(1)

    @pl.when(f == 0)
    def _():
        hn_ref[...] = _rms(h_ref[...], g_ref[...]).astype(BF16)

    u = jnp.maximum(jnp.dot(hn_ref[...], wu_ref[...], preferred_element_type=F32), 0.0)
    part = jnp.dot((u * u).astype(BF16), wd_ref[...], preferred_element_type=F32)

    @pl.when(f == 0)
    def _():
        acc_ref[...] = part

    @pl.when(f > 0)
    def _():
        acc_ref[...] += part

    @pl.when(f == pl.num_programs(1) - 1)
    def _():
        o_ref[...] = h_ref[...] + acc_ref[...]


def _ffn(h, gain, wu, wd, tm=512, tf=1024):
    n_tok = h.shape[0]
    return pl.pallas_call(
        _ffn_kernel,
        name="ffn",
        grid=(n_tok // tm, D_FF // tf),
        in_specs=[pl.BlockSpec((tm, D_MODEL), lambda i, f: (i, 0)),
                  pl.BlockSpec((1, D_MODEL), lambda i, f: (0, 0)),
                  pl.BlockSpec((D_MODEL, tf), lambda i, f: (0, f)),
                  pl.BlockSpec((tf, D_MODEL), lambda i, f: (f, 0))],
        out_specs=pl.BlockSpec((tm, D_MODEL), lambda i, f: (i, 0)),
        out_shape=jax.ShapeDtypeStruct((n_tok, D_MODEL), F32),
        scratch_shapes=[pltpu.VMEM((tm, D_MODEL), BF16), pltpu.VMEM((tm, D_MODEL), F32)],
        compiler_params=pltpu.CompilerParams(
            dimension_semantics=("parallel", "arbitrary"), vmem_limit_bytes=VMEM_LIMIT),
    )(h, gain, wu, wd)


def kernel(x, norm_mix, w_in, b_gate, q_norm_a, k_norm_a, q_norm_b, k_norm_b, rpb_b,
           w_proj_a, w_proj_b, w_out, norm_ffn, w_up, w_down):
    batch, seq, d_model = x.shape
    assert d_model == D_MODEL and seq % (DILATED_PAIRS[-1][1] * Q_BLK) == 0
    depth = norm_mix.shape[0]
    col_order = _proj_column_order()
    cos, sin = _rope_tables(seq)
    h = x.reshape(batch * seq, D_MODEL)
    for l in range(depth):
        w_in_l = w_in[l][:, col_order].astype(BF16)
        bg_l = b_gate[l].reshape(1, 2 * D_MODEL)
        proj = _in_proj(h, norm_mix[l].reshape(1, D_MODEL), w_in_l)

        gq_a = q_norm_a[l].reshape(1, HEAD_DIM)
        gk_a = k_norm_a[l].reshape(1, HEAD_DIM)
        o_groups, lse_groups = [], []
        for g, (win, dil) in enumerate(DILATED_PAIRS):
            assert win // (2 * dil) == Q_BLK // 2
            o_g, lse_g = _attn_a_group(proj, cos, sin, gq_a, gk_a, g, dil, batch, seq)
            o_groups.append(o_g)
            lse_groups.append(lse_g)

        bias = _neighbourhood_bias(rpb_b[l], seq // GRID_W)
        ob = _attn_b(proj, q_norm_b[l].reshape(1, HEAD_DIM), k_norm_b[l].reshape(1, HEAD_DIM),
                     bias, batch, seq)

        h = _mix_out(o_groups, lse_groups, ob, proj, bg_l, h,
                     w_proj_a[l].astype(BF16), w_proj_b[l].astype(BF16), w_out[l].astype(BF16))
        h = _ffn(h, norm_ffn[l].reshape(1, D_MODEL), w_up[l].astype(BF16), w_down[l].astype(BF16))
    return h.reshape(batch, seq, D_MODEL)
```

```python
import functools

import numpy as np
import jax
import jax.numpy as jnp
from jax import lax
from jax.experimental import pallas as pl
from jax.experimental.pallas import tpu as pltpu

D_MODEL = 2048
HEAD_DIM = 128
N_HEADS = 16
N_HEADS_A = 12
N_HEADS_B = 4
DILATED_PAIRS = ((128, 1), (512, 4), (2048, 16))
N_GROUPS_A = len(DILATED_PAIRS)
HEADS_PER_GROUP = 4
GROUP_W = HEADS_PER_GROUP * HEAD_DIM
SLOT_W = 3 * N_GROUPS_A * HEAD_DIM
GRID_W = 64
WIN_R = 8
WIN_C = 16
QKV_W = N_HEADS * HEAD_DIM
D_FF = 4 * D_MODEL
ROPE_THETA = 10000.0
EPS = 1e-6
NEG_INF = -1e30
SCALE = HEAD_DIM ** -0.5

A_W = HEADS_PER_GROUP * SLOT_W
D_PROJ = A_W + 3 * GROUP_W + 2 * D_MODEL
B_BLOCK0 = A_W // GROUP_W
GATE_BLOCK0 = (A_W + 3 * GROUP_W) // D_MODEL

VMEM_LIMIT = 56 * 1024 * 1024

F32 = jnp.float32
BF16 = jnp.bfloat16


def _regroup_w_in(w):
    parts = []
    for slot in range(HEADS_PER_GROUP):
        for g in range(N_GROUPS_A):
            for t in range(3):
                base = t * QKV_W + (g * HEADS_PER_GROUP + slot) * HEAD_DIM
                parts.append(w[:, base:base + HEAD_DIM])
    for t in range(3):
        base = t * QKV_W + N_HEADS_A * HEAD_DIM
        parts.append(w[:, base:base + GROUP_W])
    parts.append(w[:, 3 * QKV_W:])
    return jnp.concatenate([p.astype(BF16) for p in parts], axis=1)


def _rope_tables(seq):
    pos = np.arange(seq, dtype=np.float64)
    inv = ROPE_THETA ** (-np.arange(0, HEAD_DIM, 2, dtype=np.float64) / HEAD_DIM)
    ang = pos[:, None] * inv[None, :]
    cos = np.concatenate([np.cos(ang), np.cos(ang)], axis=-1)
    sin = np.concatenate([-np.sin(ang), np.sin(ang)], axis=-1)
    return jnp.asarray(cos, F32), jnp.asarray(sin, F32)


def _rms(x, g):
    ms = jnp.mean(x * x, axis=-1, keepdims=True)
    return (x * lax.rsqrt(ms + EPS)) * g


def _in_proj_kernel(x_ref, g_ref, w_ref, o_ref, xn_ref):
    @pl.when(pl.program_id(1) == 0)
    def _():
        xn_ref[...] = _rms(x_ref[...], g_ref[...]).astype(BF16)

    o_ref[...] = jnp.dot(xn_ref[...], w_ref[...], preferred_element_type=F32).astype(o_ref.dtype)


def _in_proj(x2, gain, w, tm=1024, tn=1024):
    n_tok = x2.shape[0]
    return pl.pallas_call(
        _in_proj_kernel,
        name="in_proj",
        grid=(n_tok // tm, D_PROJ // tn),
        in_specs=[
            pl.BlockSpec((tm, D_MODEL), lambda i, j: (i, 0)),
            pl.BlockSpec((1, D_MODEL), lambda i, j: (0, 0)),
            pl.BlockSpec((D_MODEL, tn), lambda i, j: (0, j)),
        ],
        out_specs=pl.BlockSpec((tm, tn), lambda i, j: (i, j)),
        out_shape=jax.ShapeDtypeStruct((n_tok, D_PROJ), BF16),
        scratch_shapes=[pltpu.VMEM((tm, D_MODEL), BF16)],
        compiler_params=pltpu.CompilerParams(
            dimension_semantics=("parallel", "arbitrary"), vmem_limit_bytes=VMEM_LIMIT),
    )(x2, gain, w)


Q_BLK = 128
HALF_WIN = 64


def _attn_a_kernel(qkv_ref, cos_ref, sin_ref, gq_ref, gk_ref, o_ref,
                   qn_ref, kn_ref, vn_ref, sq_ref, sk_ref, sv_ref, og_ref, lg_ref, *, seq, dils):
    n_blk = seq // Q_BLK
    gq = gq_ref[...]
    gk = gk_ref[...]

    for g, dil in enumerate(dils):
        sub_len = seq // dil
        win = min(2 * Q_BLK, sub_len)
        blk_per_sub = sub_len // Q_BLK
        c0 = 3 * g * HEAD_DIM
        q_sl = slice(c0, c0 + HEAD_DIM)
        k_sl = slice(c0 + HEAD_DIM, c0 + 2 * HEAD_DIM)
        v_sl = slice(c0 + 2 * HEAD_DIM, c0 + 3 * HEAD_DIM)

        def prep(n, c, dil=dil, q_sl=q_sl, k_sl=k_sl, v_sl=v_sl):
            r0 = pl.multiple_of(n * Q_BLK, Q_BLK)
            rows = pl.ds(r0, Q_BLK)
            cos = cos_ref[rows, :]
            sin = sin_ref[rows, :]
            qh = _rms(qkv_ref[rows, q_sl].astype(F32), gq)
            qh = (qh * cos + pltpu.roll(qh, HEAD_DIM // 2, 1) * sin) * SCALE
            kh = _rms(qkv_ref[rows, k_sl].astype(F32), gk)
            kh = kh * cos + pltpu.roll(kh, HEAD_DIM // 2, 1) * sin
            if dil == 1:
                qn_ref[rows, :] = qh.astype(BF16)
                kn_ref[rows, :] = kh.astype(BF16)
                vn_ref[rows, :] = qkv_ref[rows, v_sl]
            else:
                sq_ref[rows, :] = qh
                sk_ref[rows, :] = kh
                sv_ref[rows, :] = qkv_ref[rows, v_sl].astype(F32)
            return c

        lax.fori_loop(0, n_blk, prep, 0)

        def token_rows(n, dil=dil, sub_len=sub_len):
            base = n * Q_BLK
            r = base // sub_len
            m0 = base - r * sub_len
            if dil == 1:
                return pl.ds(pl.multiple_of(base, Q_BLK), Q_BLK)
            return pl.ds(r + m0 * dil, Q_BLK, stride=dil)

        if dil > 1:
            def gather(n, c, token_rows=token_rows):
                dst = pl.ds(pl.multiple_of(n * Q_BLK, Q_BLK), Q_BLK)
                src = token_rows(n)
                qn_ref[dst, :] = sq_ref[src, :].astype(BF16)
                kn_ref[dst, :] = sk_ref[src, :].astype(BF16)
                vn_ref[dst, :] = sv_ref[src, :].astype(BF16)
                return c

            lax.fori_loop(0, n_blk, gather, 0)

        rel0 = (lax.broadcasted_iota(jnp.int32, (Q_BLK, win), 1)
                - lax.broadcasted_iota(jnp.int32, (Q_BLK, win), 0))

        def block(n, c, g=g, sub_len=sub_len, win=win, blk_per_sub=blk_per_sub, rel0=rel0,
                  token_rows=token_rows):
            q0 = pl.multiple_of(n * Q_BLK, Q_BLK)
            sub0 = (n // blk_per_sub) * sub_len
            q_loc = q0 - sub0
            k_loc = jnp.clip(q_loc - HALF_WIN, 0, sub_len - win)
            k0 = pl.multiple_of(sub0 + k_loc, HALF_WIN)
            valid = jnp.abs(rel0 + (k_loc - q_loc)) <= HALF_WIN
            q = qn_ref[pl.ds(q0, Q_BLK), :]
            k = kn_ref[pl.ds(k0, win), :]
            v = vn_ref[pl.ds(k0, win), :]
            s = lax.dot_general(q, k, (((1,), (1,)), ((), ())), preferred_element_type=F32)
            s = jnp.where(valid, s, NEG_INF)
            m = jnp.max(s, axis=-1, keepdims=True)
            p = jnp.exp(s - m)
            l = jnp.sum(p, axis=-1, keepdims=True)
            acc = jnp.dot(p.astype(BF16), v, preferred_element_type=F32)
            dst = token_rows(n)
            og_ref.at[g][dst, :] = acc * (1.0 / l)
            lg_ref.at[g][dst, :] = jnp.broadcast_to(m + jnp.log(l), (Q_BLK, HEAD_DIM))
            return c

        lax.fori_loop(0, n_blk, block, 0)

    def combine(n, c):
        rows = pl.ds(pl.multiple_of(n * Q_BLK, Q_BLK), Q_BLK)
        lses = [lg_ref[g, rows, :] for g in range(len(dils))]
        mx = functools.reduce(jnp.maximum, lses)
        es = [jnp.exp(x - mx) for x in lses]
        inv = 1.0 / functools.reduce(lambda a, b: a + b, es)
        oa = (es[0] * inv) * og_ref[0, rows, :]
        for g in range(1, len(dils)):
            oa = oa + (es[g] * inv) * og_ref[g, rows, :]
        o_ref[rows, :] = oa.astype(o_ref.dtype)
        return c

    lax.fori_loop(0, n_blk, combine, 0)


def _attn_a(proj3, cos, sin, gq, gk):
    batch, seq, _ = proj3.shape
    dils = tuple(d for _, d in DILATED_PAIRS)
    n_g = len(dils)
    rope_spec = pl.BlockSpec((seq, HEAD_DIM), lambda b, s: (0, 0), pipeline_mode=pl.Buffered(1))
    gain_spec = pl.BlockSpec((1, HEAD_DIM), lambda b, s: (0, 0))
    return pl.pallas_call(
        functools.partial(_attn_a_kernel, seq=seq, dils=dils),
        name="attn_a",
        grid=(batch, HEADS_PER_GROUP),
        in_specs=[pl.BlockSpec((None, seq, SLOT_W), lambda b, s: (b, 0, s)),
                  rope_spec, rope_spec, gain_spec, gain_spec],
        out_specs=pl.BlockSpec((None, seq, HEAD_DIM), lambda b, s: (b, 0, s)),
        out_shape=jax.ShapeDtypeStruct((batch, seq, GROUP_W), BF16),
        scratch_shapes=[pltpu.VMEM((seq, HEAD_DIM), BF16)] * 3
        + [pltpu.VMEM((seq, HEAD_DIM), F32)] * 3
        + [pltpu.VMEM((n_g, seq, HEAD_DIM), F32)] * 2,
        compiler_params=pltpu.CompilerParams(
            dimension_semantics=("parallel", "arbitrary"), vmem_limit_bytes=VMEM_LIMIT),
    )(proj3, cos, sin, gq, gk)


B_QROWS = 4
B_KROWS = B_QROWS + WIN_R
B_NQ = B_QROWS * GRID_W
B_NK = B_KROWS * GRID_W


def _attn_b_kernel(q_ref, k_ref, v_ref, gq_ref, gk_ref, bias_ref, o_ref, qn_ref, kn_ref, *, rows):
    gq = gq_ref[...]
    gk = gk_ref[...]
    n_grp = rows // B_QROWS

    def prep(n, c):
        r0 = pl.multiple_of(n * Q_BLK, Q_BLK)
        for h in range(N_HEADS_B):
            sl = slice(h * HEAD_DIM, (h + 1) * HEAD_DIM)
            qh = _rms(q_ref[pl.ds(r0, Q_BLK), sl].astype(F32), gq)
            qn_ref[pl.ds(r0, Q_BLK), sl] = (qh * SCALE).astype(BF16)
            kh = _rms(k_ref[pl.ds(r0, Q_BLK), sl].astype(F32), gk)
            kn_ref[pl.ds(r0, Q_BLK), sl] = kh.astype(BF16)
        return c

    lax.fori_loop(0, rows * GRID_W // Q_BLK, prep, 0)

    def group(i, c):
        r0 = i * B_QROWS
        ws = jnp.clip(r0 - WIN_R // 2, 0, rows - B_KROWS)
        case = jnp.where(i == 0, 0, jnp.where(i == n_grp - 1, 2, 1))
        q0 = pl.multiple_of(r0 * GRID_W, B_NQ)
        k0 = pl.multiple_of(ws * GRID_W, GRID_W)
        for h in range(N_HEADS_B):
            sl = slice(h * HEAD_DIM, (h + 1) * HEAD_DIM)
            q = qn_ref[pl.ds(q0, B_NQ), sl]
            k = kn_ref[pl.ds(k0, B_NK), sl]
            v = v_ref[pl.ds(k0, B_NK), sl]
            s = lax.dot_general(q, k, (((1,), (1,)), ((), ())), preferred_element_type=F32)
            s = s + bias_ref[h, case]
            m = jnp.max(s, axis=-1, keepdims=True)
            p = jnp.exp(s - m)
            l = jnp.sum(p, axis=-1, keepdims=True)
            acc = jnp.dot(p.astype(BF16), v, preferred_element_type=F32)
            o_ref[pl.ds(q0, B_NQ), sl] = (acc * (1.0 / l)).astype(o_ref.dtype)
        return c

    lax.fori_loop(0, n_grp, group, 0)


def _neighbourhood_bias(rpb, rows):
    assert rows % B_QROWS == 0 and rows >= B_KROWS
    n_h = rpb.shape[0]
    rpb = rpb.astype(F32)
    c = np.arange(GRID_W)
    ext = jnp.pad(rpb, ((0, 0), (0, 0), (GRID_W - WIN_C, GRID_W - WIN_C)), mode="edge")
    t1 = jnp.stack([ext[:, :, GRID_W - 1 - cq:2 * GRID_W - 1 - cq] for cq in range(GRID_W)], axis=2)
    col_start = np.clip(c - WIN_C // 2, 0, GRID_W - WIN_C)
    col_mask = (c[None, :] >= col_start[:, None]) & (c[None, :] < col_start[:, None] + WIN_C)
    t1 = jnp.where(col_mask[None, None], t1, NEG_INF)
    neg = jnp.full((n_h, GRID_W, GRID_W), NEG_INF, F32)
    n_grp = rows // B_QROWS
    cases = []
    for i in (0, 1, n_grp - 1):
        r0 = i * B_QROWS
        ws = int(np.clip(r0 - WIN_R // 2, 0, rows - B_KROWS))
        per_q = []
        for rq in range(B_QROWS):
            r = r0 + rq
            rs = int(np.clip(r - WIN_R // 2, 0, rows - WIN_R))
            per_k = []
            for jr in range(B_KROWS):
                key_row = ws + jr
                inside = rs <= key_row < rs + WIN_R
                per_k.append(t1[:, key_row - r + WIN_R - 1] if inside else neg)
            per_q.append(jnp.stack(per_k, axis=2))
        cases.append(jnp.stack(per_q, axis=1))
    return jnp.stack(cases, axis=1).reshape(n_h, 3, B_NQ, B_NK)


def _attn_b(proj3, gq, gk, bias):
    batch, seq, _ = proj3.shape
    rows = seq // GRID_W

    def qkv_spec(t):
        return pl.BlockSpec((None, seq, GROUP_W), lambda b: (b, 0, B_BLOCK0 + t))

    gain_spec = pl.BlockSpec((1, HEAD_DIM), lambda b: (0, 0))
    bias_spec = pl.BlockSpec(bias.shape, lambda b: (0, 0, 0, 0), pipeline_mode=pl.Buffered(1))
    return pl.pallas_call(
        functools.partial(_attn_b_kernel, rows=rows),
        name="attn_b",
        grid=(batch,),
        in_specs=[qkv_spec(0), qkv_spec(1), qkv_spec(2), gain_spec, gain_spec, bias_spec],
        out_specs=pl.BlockSpec((None, seq, GROUP_W), lambda b: (b, 0, 0)),
        out_shape=jax.ShapeDtypeStruct((batch, seq, GROUP_W), BF16),
        scratch_shapes=[pltpu.VMEM((seq, GROUP_W), BF16), pltpu.VMEM((seq, GROUP_W), BF16)],
        compiler_params=pltpu.CompilerParams(
            dimension_semantics=("parallel",), vmem_limit_bytes=VMEM_LIMIT),
    )(proj3, proj3, proj3, gq, gk, bias)


def _mix_out_kernel(oa_ref, ob_ref, ga_ref, gb_ref, bg_ref, x_ref, pa_ref, pb_ref, wo_ref, h_ref):
    ya = jnp.dot(oa_ref[...], pa_ref[...], preferred_element_type=F32)
    yb = jnp.dot(ob_ref[...], pb_ref[...], preferred_element_type=F32)
    ga = jax.nn.sigmoid(ga_ref[...].astype(F32) + bg_ref[:, :D_MODEL])
    gb = jax.nn.sigmoid(gb_ref[...].astype(F32) + bg_ref[:, D_MODEL:])
    mixed = (ga * ya + gb * yb).astype(BF16)
    h_ref[...] = x_ref[...] + jnp.dot(mixed, wo_ref[...], preferred_element_type=F32)


def _mix_out(oa, ob, proj, b_gate, x2, pa, pb, wo, tm=512):
    n_tok = x2.shape[0]
    row512 = pl.BlockSpec((tm, GROUP_W), lambda i: (i, 0))
    const = pl.Buffered(1)
    return pl.pallas_call(
        _mix_out_kernel,
        name="mix_out",
        grid=(n_tok // tm,),
        in_specs=[row512, row512,
                  pl.BlockSpec((tm, D_MODEL), lambda i: (i, GATE_BLOCK0)),
                  pl.BlockSpec((tm, D_MODEL), lambda i: (i, GATE_BLOCK0 + 1)),
                  pl.BlockSpec((1, 2 * D_MODEL), lambda i: (0, 0)),
                  pl.BlockSpec((tm, D_MODEL), lambda i: (i, 0)),
                  pl.BlockSpec((GROUP_W, D_MODEL), lambda i: (0, 0), pipeline_mode=const),
                  pl.BlockSpec((GROUP_W, D_MODEL), lambda i: (0, 0), pipeline_mode=const),
                  pl.BlockSpec((D_MODEL, D_MODEL), lambda i: (0, 0), pipeline_mode=const)],
        out_specs=pl.BlockSpec((tm, D_MODEL), lambda i: (i, 0)),
        out_shape=jax.ShapeDtypeStruct((n_tok, D_MODEL), F32),
        compiler_params=pltpu.CompilerParams(
            dimension_semantics=("parallel",), vmem_limit_bytes=VMEM_LIMIT),
    )(oa, ob, proj, proj, b_gate, x2, pa, pb, wo)


def _ffn_kernel(h_ref, g_ref, wu_ref, wd_ref, o_ref, hn_ref, acc_ref):
    f = pl.program_id(1)

    @pl.when(f == 0)
    def _():
        hn_ref[...] = _rms(h_ref[...], g_ref[...]).astype(BF16)

    u = jnp.maximum(jnp.dot(hn_ref[...], wu_ref[...], preferred_element_type=F32), 0.0)
    part = jnp.dot((u * u).astype(BF16), wd_ref[...], preferred_element_type=F32)

    @pl.when(f == 0)
    def _():
        acc_ref[...] = part

    @pl.when(f > 0)
    def _():
        acc_ref[...] += part

    @pl.when(f == pl.num_programs(1) - 1)
    def _():
        o_ref[...] = h_ref[...] + acc_ref[...]


def _ffn(h, gain, wu, wd, tm=512, tf=1024):
    n_tok = h.shape[0]
    return pl.pallas_call(
        _ffn_kernel,
        name="ffn",
        grid=(n_tok // tm, D_FF // tf),
        in_specs=[pl.BlockSpec((tm, D_MODEL), lambda i, f: (i, 0)),
                  pl.BlockSpec((1, D_MODEL), lambda i, f: (0, 0)),
                  pl.BlockSpec((D_MODEL, tf), lambda i, f: (0, f)),
                  pl.BlockSpec((tf, D_MODEL), lambda i, f: (f, 0))],
        out_specs=pl.BlockSpec((tm, D_MODEL), lambda i, f: (i, 0)),
        out_shape=jax.ShapeDtypeStruct((n_tok, D_MODEL), F32),
        scratch_shapes=[pltpu.VMEM((tm, D_MODEL), BF16), pltpu.VMEM((tm, D_MODEL), F32)],
        compiler_params=pltpu.CompilerParams(
            dimension_semantics=("parallel", "arbitrary"), vmem_limit_bytes=VMEM_LIMIT),
    )(h, gain, wu, wd)


def kernel(x, norm_mix, w_in, b_gate, q_norm_a, k_norm_a, q_norm_b, k_norm_b, rpb_b,
           w_proj_a, w_proj_b, w_out, norm_ffn, w_up, w_down):
    batch, seq, d_model = x.shape
    assert d_model == D_MODEL and seq % (DILATED_PAIRS[-1][1] * Q_BLK) == 0
    for win, dil in DILATED_PAIRS:
        assert win // (2 * dil) == HALF_WIN
    depth = norm_mix.shape[0]
    cos, sin = _rope_tables(seq)
    h = x.reshape(batch * seq, D_MODEL)
    for l in range(depth):
        proj = _in_proj(h, norm_mix[l].reshape(1, D_MODEL), _regroup_w_in(w_in[l]))
        proj3 = proj.reshape(batch, seq, D_PROJ)
        oa = _attn_a(proj3, cos, sin, q_norm_a[l].reshape(1, HEAD_DIM), k_norm_a[l].reshape(1, HEAD_DIM))
        bias = _neighbourhood_bias(rpb_b[l], seq // GRID_W)
        ob = _attn_b(proj3, q_norm_b[l].reshape(1, HEAD_DIM), k_norm_b[l].reshape(1, HEAD_DIM), bias)
        h = _mix_out(oa.reshape(batch * seq, GROUP_W), ob.reshape(batch * seq, GROUP_W), proj,
                     b_gate[l].reshape(1, 2 * D_MODEL), h,
                     w_proj_a[l].astype(BF16), w_proj_b[l].astype(BF16), w_out[l].astype(BF16))
        h = _ffn(h, norm_ffn[l].reshape(1, D_MODEL), w_up[l].astype(BF16), w_down[l].astype(BF16))
    return h.reshape(batch, seq, D_MODEL)
```

```python
import functools

import numpy as np
import jax
import jax.numpy as jnp
from jax import lax
from jax.experimental import pallas as pl
from jax.experimental.pallas import tpu as pltpu

D_MODEL = 2048
HEAD_DIM = 128
N_HEADS = 16
N_HEADS_A = 12
N_HEADS_B = 4
DILATED_PAIRS = ((128, 1), (512, 4), (2048, 16))
N_GROUPS_A = len(DILATED_PAIRS)
HEADS_PER_GROUP = 4
GROUP_W = HEADS_PER_GROUP * HEAD_DIM
SLOT_W = 3 * N_GROUPS_A * HEAD_DIM
GRID_W = 64
WIN_R = 8
WIN_C = 16
QKV_W = N_HEADS * HEAD_DIM
D_FF = 4 * D_MODEL
ROPE_THETA = 10000.0
EPS = 1e-6
NEG_INF = -1e30
SCALE = HEAD_DIM ** -0.5

A_W = HEADS_PER_GROUP * SLOT_W
D_PROJ = A_W + 3 * GROUP_W + 2 * D_MODEL
B_BLOCK0 = A_W // GROUP_W
GATE_BLOCK0 = (A_W + 3 * GROUP_W) // D_MODEL

VMEM_LIMIT = 56 * 1024 * 1024

F32 = jnp.float32
BF16 = jnp.bfloat16


def _regroup_w_in(w):
    parts = []
    for slot in range(HEADS_PER_GROUP):
        for g in range(N_GROUPS_A):
            for t in range(3):
                base = t * QKV_W + (g * HEADS_PER_GROUP + slot) * HEAD_DIM
                parts.append(w[:, base:base + HEAD_DIM])
    for t in range(3):
        base = t * QKV_W + N_HEADS_A * HEAD_DIM
        parts.append(w[:, base:base + GROUP_W])
    parts.append(w[:, 3 * QKV_W:])
    return jnp.concatenate([p.astype(BF16) for p in parts], axis=1)


def _rope_tables(seq):
    pos = np.arange(seq, dtype=np.float64)
    inv = ROPE_THETA ** (-np.arange(0, HEAD_DIM, 2, dtype=np.float64) / HEAD_DIM)
    ang = pos[:, None] * inv[None, :]
    cos = np.concatenate([np.cos(ang), np.cos(ang)], axis=-1)
    sin = np.concatenate([-np.sin(ang), np.sin(ang)], axis=-1)
    return jnp.asarray(cos, F32), jnp.asarray(sin, F32)


def _rms(x, g):
    ms = jnp.mean(x * x, axis=-1, keepdims=True)
    return (x * lax.rsqrt(ms + EPS)) * g


def _mean_weights():
    return jnp.full((2 * HEAD_DIM, HEAD_DIM), 1.0 / HEAD_DIM, BF16)


def _lane_meansq(raw, mean_w):
    xf = raw.astype(F32)
    sq = xf * xf
    hi = sq.astype(BF16)
    lo = (sq - hi.astype(F32)).astype(BF16)
    return jnp.dot(jnp.concatenate([hi, lo], axis=1), mean_w, preferred_element_type=F32)


def _in_proj_kernel(x_ref, g_ref, w_ref, o_ref, xn_ref):
    @pl.when(pl.program_id(1) == 0)
    def _():
        xn_ref[...] = _rms(x_ref[...], g_ref[...]).astype(BF16)

    o_ref[...] = jnp.dot(xn_ref[...], w_ref[...], preferred_element_type=F32).astype(o_ref.dtype)


def _in_proj(x2, gain, w, tm=1024, tn=1024):
    n_tok = x2.shape[0]
    return pl.pallas_call(
        _in_proj_kernel,
        name="in_proj",
        grid=(n_tok // tm, D_PROJ // tn),
        in_specs=[
            pl.BlockSpec((tm, D_MODEL), lambda i, j: (i, 0)),
            pl.BlockSpec((1, D_MODEL), lambda i, j: (0, 0)),
            pl.BlockSpec((D_MODEL, tn), lambda i, j: (0, j)),
        ],
        out_specs=pl.BlockSpec((tm, tn), lambda i, j: (i, j)),
        out_shape=jax.ShapeDtypeStruct((n_tok, D_PROJ), BF16),
        scratch_shapes=[pltpu.VMEM((tm, D_MODEL), BF16)],
        compiler_params=pltpu.CompilerParams(
            dimension_semantics=("parallel", "arbitrary"), vmem_limit_bytes=VMEM_LIMIT),
    )(x2, gain, w)


Q_BLK = 128
HALF_WIN = 64
A_WIN = 2 * Q_BLK


def _band_masks():
    rel = np.arange(A_WIN)[None, :] - np.arange(Q_BLK)[:, None]
    tiles = [np.where(np.abs(rel - off) <= HALF_WIN, 0.0, NEG_INF) for off in (0, HALF_WIN, 2 * HALF_WIN)]
    return jnp.asarray(np.stack(tiles), F32)


def _attn_a_kernel(qkv_ref, qc_ref, qs_ref, kc_ref, ks_ref, mask_ref, o_ref,
                   qn_ref, kn_ref, vn_ref, fq_ref, fk_ref, fv_ref, og_ref, lg_ref, *, seq, dils):
    n_blk = seq // Q_BLK
    lane = lax.broadcasted_iota(jnp.int32, (HEAD_DIM, HEAD_DIM), 1)
    sub = lax.broadcasted_iota(jnp.int32, (HEAD_DIM, HEAD_DIM), 0)
    half_turn = jnp.where(lane == (sub + HEAD_DIM // 2) % HEAD_DIM, 1.0, 0.0).astype(BF16)
    mean_w = _mean_weights()
    vn_ref[:, HEAD_DIM:] = jnp.ones((seq, HEAD_DIM), BF16)

    def norm_rope(raw, cos_g, sin_g):
        inv_rms = lax.rsqrt(_lane_meansq(raw, mean_w) + EPS)
        turned =jnp.dot(raw, half_turn, preferred_element_type=F32)
        return (raw.astype(F32) * cos_g + turned * sin_g) * inv_rms

    for g, dil in enumerate(dils):
        sub_len = seq // dil
        win = min(A_WIN, sub_len)
        blk_per_sub = sub_len // Q_BLK
        c0 = 3 * g * HEAD_DIM
        q_sl = slice(c0, c0 + HEAD_DIM)
        k_sl = slice(c0 + HEAD_DIM, c0 + 2 * HEAD_DIM)
        v_sl = slice(c0 + 2 * HEAD_DIM, c0 + 3 * HEAD_DIM)

        def prep(n, c, dil=dil, q_sl=q_sl, k_sl=k_sl, v_sl=v_sl):
            rows = pl.ds(pl.multiple_of(n * Q_BLK, Q_BLK), Q_BLK)
            qh = norm_rope(qkv_ref[rows, q_sl], qc_ref[rows, :], qs_ref[rows, :])
            kh = norm_rope(qkv_ref[rows, k_sl], kc_ref[rows, :], ks_ref[rows, :])
            if dil == 1:
                qn_ref[rows, :] = qh.astype(BF16)
                kn_ref[rows, :] = kh.astype(BF16)
                vn_ref[rows, :HEAD_DIM] = qkv_ref[rows, v_sl]
            else:
                fq_ref[rows, :] = qh
                fk_ref[rows, :] = kh
                fv_ref[rows, :] = qkv_ref[rows, v_sl].astype(F32)
            return c

        lax.fori_loop(0, n_blk, prep, 0, unroll=2)

        def token_rows(n, dil=dil, sub_len=sub_len):
            base = n * Q_BLK
            r = base // sub_len
            m0 = base - r * sub_len
            if dil == 1:
                return pl.ds(pl.multiple_of(base, Q_BLK), Q_BLK)
            return pl.ds(r + m0 * dil, Q_BLK, stride=dil)

        if dil > 1:
            def gather(n, c, token_rows=token_rows):
                dst = pl.ds(pl.multiple_of(n * Q_BLK, Q_BLK), Q_BLK)
                src = token_rows(n)
                qn_ref[dst, :] = fq_ref[src, :].astype(BF16)
                kn_ref[dst, :] = fk_ref[src, :].astype(BF16)
                vn_ref[dst, :HEAD_DIM] = fv_ref[src, :].astype(BF16)
                return c

            lax.fori_loop(0, n_blk, gather, 0, unroll=2)

        scatter_later = dil % 8 == 0

        def block(n, c, g=g, sub_len=sub_len, win=win, blk_per_sub=blk_per_sub, token_rows=token_rows,
                  scatter_later=scatter_later):
            q0 = pl.multiple_of(n * Q_BLK, Q_BLK)
            sub0 = (n // blk_per_sub) * sub_len
            q_loc = q0 - sub0
            k_loc = jnp.clip(q_loc - HALF_WIN, 0, sub_len - win)
            k0 = pl.multiple_of(sub0 + k_loc, HALF_WIN)
            band = mask_ref[(q_loc - k_loc) // HALF_WIN, :, :win]
            q = qn_ref[pl.ds(q0, Q_BLK), :]
            k = kn_ref[pl.ds(k0, win), :]
            s = lax.dot_general(q, k, (((1,), (1,)), ((), ())), preferred_element_type=F32) + band
            m = jnp.max(s, axis=-1, keepdims=True)
            p = jnp.exp(s - m).astype(BF16)
            acc = jnp.dot(p, vn_ref[pl.ds(k0, win), :], preferred_element_type=F32)
            denom = acc[:, HEAD_DIM:]
            out = acc[:, :HEAD_DIM] * (1.0 / denom)
            lse = m + jnp.log(denom)
            if scatter_later:
                fq_ref[pl.ds(q0, Q_BLK), :] = out
                fk_ref[pl.ds(q0, Q_BLK), :] = lse
            else:
                dst = token_rows(n)
                og_ref.at[g][dst, :] = out
                lg_ref.at[g][dst, :] = lse
            return c

        lax.fori_loop(0, n_blk, block, 0, unroll=4)

        if scatter_later:
            def scatter(n, c, g=g, token_rows=token_rows):
                src = pl.ds(pl.multiple_of(n * Q_BLK, Q_BLK), Q_BLK)
                dst = token_rows(n)
                og_ref.at[g][dst, :] = fq_ref[src, :]
                lg_ref.at[g][dst, :] = fk_ref[src, :]
                return c

            lax.fori_loop(0, n_blk, scatter, 0, unroll=2)

    def combine(n, c):
        rows = pl.ds(pl.multiple_of(n * Q_BLK, Q_BLK), Q_BLK)
        lses = [lg_ref[g, rows, :] for g in range(len(dils))]
        mx = functools.reduce(jnp.maximum, lses)
        es = [jnp.exp(x - mx) for x in lses]
        inv = 1.0 / functools.reduce(lambda a, b: a + b, es)
        oa = (es[0] * inv) * og_ref[0, rows, :]
        for g in range(1, len(dils)):
            oa = oa + (es[g] * inv) * og_ref[g, rows, :]
        o_ref[rows, :] = oa.astype(o_ref.dtype)
        return c

    lax.fori_loop(0, n_blk, combine, 0, unroll=2)


def _attn_a(proj3, cos, sin, gq, gk):
    batch, seq, _ = proj3.shape
    dils = tuple(d for _, d in DILATED_PAIRS)
    n_g = len(dils)
    half = HEAD_DIM // 2
    tables = (cos * (gq * SCALE), sin * (jnp.roll(gq, half) * SCALE), cos * gk, sin * jnp.roll(gk, half))
    const = pl.Buffered(1)
    rope_spec = pl.BlockSpec((seq, HEAD_DIM), lambda b, s: (0, 0), pipeline_mode=const)
    masks = _band_masks()
    return pl.pallas_call(
        functools.partial(_attn_a_kernel, seq=seq, dils=dils),
        name="attn_a",
        grid=(batch, HEADS_PER_GROUP),
        in_specs=[pl.BlockSpec((None, seq, SLOT_W), lambda b, s: (b, 0, s)),
                  rope_spec, rope_spec, rope_spec, rope_spec,
                  pl.BlockSpec(masks.shape, lambda b, s: (0, 0, 0), pipeline_mode=const)],
        out_specs=pl.BlockSpec((None, seq, HEAD_DIM), lambda b, s: (b, 0, s)),
        out_shape=jax.ShapeDtypeStruct((batch, seq, GROUP_W), BF16),
        scratch_shapes=[pltpu.VMEM((seq, HEAD_DIM), BF16)] * 2
        + [pltpu.VMEM((seq, 2 * HEAD_DIM), BF16)]
        + [pltpu.VMEM((seq, HEAD_DIM), F32)] * 3
        + [pltpu.VMEM((n_g, seq, HEAD_DIM), F32)] * 2,
        compiler_params=pltpu.CompilerParams(
            dimension_semantics=("parallel", "arbitrary"), vmem_limit_bytes=VMEM_LIMIT),
    )(proj3, *tables, masks)


B_QROWS = 4
B_KROWS = B_QROWS + WIN_R
B_NQ = B_QROWS * GRID_W
B_NK = B_KROWS * GRID_W


def _attn_b_kernel(q_ref, k_ref, v_ref, gq_ref, gk_ref, bias_ref, o_ref, qn_ref, kn_ref, *, rows):
    gq = gq_ref[...] * SCALE
    gk = gk_ref[...]
    n_grp = rows // B_QROWS
    mean_w = _mean_weights()

    def norm(raw, gain):
        inv_rms = lax.rsqrt(_lane_meansq(raw, mean_w) + EPS)
        return ((raw.astype(F32) * inv_rms) * gain).astype(BF16)

    def prep(n, c):
        rows_n = pl.ds(pl.multiple_of(n * Q_BLK, Q_BLK), Q_BLK)
        for h in range(N_HEADS_B):
            sl = slice(h * HEAD_DIM, (h + 1) * HEAD_DIM)
            qn_ref[rows_n, sl] = norm(q_ref[rows_n, sl], gq)
            kn_ref[rows_n, sl] = norm(k_ref[rows_n, sl], gk)
        return c

    lax.fori_loop(0, rows * GRID_W // Q_BLK, prep, 0)

    def group(i, c):
        r0 = i * B_QROWS
        ws = jnp.clip(r0 - WIN_R // 2, 0, rows - B_KROWS)
        case = jnp.where(i == 0, 0, jnp.where(i == n_grp - 1, 2, 1))
        q0 = pl.multiple_of(r0 * GRID_W, B_NQ)
        k0 = pl.multiple_of(ws * GRID_W, GRID_W)
        for h in range(N_HEADS_B):
            sl = slice(h * HEAD_DIM, (h + 1) * HEAD_DIM)
            q = qn_ref[pl.ds(q0, B_NQ), sl]
            k = kn_ref[pl.ds(k0, B_NK), sl]
            v = v_ref[pl.ds(k0, B_NK), sl]
            s = lax.dot_general(q, k, (((1,), (1,)), ((), ())), preferred_element_type=F32)
            s = s + bias_ref[h, case]
            m = jnp.max(s, axis=-1, keepdims=True)
            p = jnp.exp(s - m)
            l = jnp.sum(p, axis=-1, keepdims=True)
            acc = jnp.dot(p.astype(BF16), v, preferred_element_type=F32)
            o_ref[pl.ds(q0, B_NQ), sl] = (acc * (1.0 / l)).astype(o_ref.dtype)
        return c

    lax.fori_loop(0, n_grp, group, 0)


def _neighbourhood_bias(rpb, rows):
    n_grp = rows // B_QROWS
    assert rows % B_QROWS == 0 and n_grp >= 3
    n_h = rpb.shape[0]
    c = np.arange(GRID_W)
    dc = np.clip(c[None, :] - c[:, None], -(WIN_C - 1), WIN_C - 1) + (WIN_C - 1)
    col_start = np.clip(c - WIN_C // 2, 0, GRID_W - WIN_C)
    col_ok = (c[None, :] >= col_start[:, None]) & (c[None, :] < col_start[:, None] + WIN_C)
    col_pick = np.zeros((2 * WIN_C - 1, GRID_W * GRID_W), np.float32)
    col_pick[dc.ravel(), np.arange(GRID_W * GRID_W)] = 1.0
    row_pick = np.zeros((3, B_QROWS, B_KROWS, 2 * WIN_R - 1), np.float32)
    row_ok = np.zeros((3, B_QROWS, B_KROWS), bool)
    for case, i in enumerate((0, 1, n_grp - 1)):
        r0 = i * B_QROWS
        ws = int(np.clip(r0 - WIN_R // 2, 0, rows - B_KROWS))
        for rq in range(B_QROWS):
            r = r0 + rq
            rs = int(np.clip(r - WIN_R // 2, 0, rows - WIN_R))
            for jr in range(B_KROWS):
                key_row = ws + jr
                if rs <= key_row < rs + WIN_R:
                    row_pick[case, rq, jr, key_row - r + WIN_R - 1] = 1.0
                    row_ok[case, rq, jr] = True
    tab = jnp.einsum("xa,hab,bc->hxc", row_pick.reshape(-1, 2 * WIN_R - 1), rpb.astype(F32), col_pick,
                     precision=lax.Precision.HIGHEST)
    ok = row_ok.reshape(-1)[:, None] & col_ok.reshape(-1)[None, :]
    tab = jnp.where(ok[None], tab, NEG_INF)
    tab = tab.reshape(n_h, 3, B_QROWS, B_KROWS, GRID_W, GRID_W).transpose(0, 1, 2, 4, 3, 5)
    return tab.reshape(n_h, 3, B_NQ, B_NK)


def _attn_b(proj3, gq, gk, bias):
    batch, seq, _ = proj3.shape
    rows = seq // GRID_W

    def qkv_spec(t):
        return pl.BlockSpec((None, seq, GROUP_W), lambda b: (b, 0, B_BLOCK0 + t))

    gain_spec = pl.BlockSpec((1, HEAD_DIM), lambda b: (0, 0))
    bias_spec = pl.BlockSpec(bias.shape, lambda b: (0, 0, 0, 0), pipeline_mode=pl.Buffered(1))
    return pl.pallas_call(
        functools.partial(_attn_b_kernel, rows=rows),
        name="attn_b",
        grid=(batch,),
        in_specs=[qkv_spec(0), qkv_spec(1), qkv_spec(2), gain_spec, gain_spec, bias_spec],
        out_specs=pl.BlockSpec((None, seq, GROUP_W), lambda b: (b, 0, 0)),
        out_shape=jax.ShapeDtypeStruct((batch, seq, GROUP_W), BF16),
        scratch_shapes=[pltpu.VMEM((seq, GROUP_W), BF16), pltpu.VMEM((seq, GROUP_W), BF16)],
        compiler_params=pltpu.CompilerParams(
            dimension_semantics=("parallel",), vmem_limit_bytes=VMEM_LIMIT),
    )(proj3, proj3, proj3, gq, gk, bias)


def _mix_out_kernel(oa_ref, ob_ref, ga_ref, gb_ref, bg_ref, x_ref, pa_ref, pb_ref, wo_ref, h_ref):
    ya = jnp.dot(oa_ref[...], pa_ref[...], preferred_element_type=F32)
    yb = jnp.dot(ob_ref[...], pb_ref[...], preferred_element_type=F32)
    ga = jax.nn.sigmoid(ga_ref[...].astype(F32) + bg_ref[:, :D_MODEL])
    gb = jax.nn.sigmoid(gb_ref[...].astype(F32) + bg_ref[:, D_MODEL:])
    mixed = (ga * ya + gb * yb).astype(BF16)
    h_ref[...] = x_ref[...] + jnp.dot(mixed, wo_ref[...], preferred_element_type=F32)


def _mix_out(oa, ob, proj, b_gate, x2, pa, pb, wo, tm=512):
    n_tok = x2.shape[0]
    row512 = pl.BlockSpec((tm, GROUP_W), lambda i: (i, 0))
    const = pl.Buffered(1)
    return pl.pallas_call(
        _mix_out_kernel,
        name="mix_out",
        grid=(n_tok // tm,),
        in_specs=[row512, row512,
                  pl.BlockSpec((tm, D_MODEL), lambda i: (i, GATE_BLOCK0)),
                  pl.BlockSpec((tm, D_MODEL), lambda i: (i, GATE_BLOCK0 + 1)),
                  pl.BlockSpec((1, 2 * D_MODEL), lambda i: (0, 0)),
                  pl.BlockSpec((tm, D_MODEL), lambda i: (i, 0)),
                  pl.BlockSpec((GROUP_W, D_MODEL), lambda i: (0, 0), pipeline_mode=const),
                  pl.BlockSpec((GROUP_W, D_MODEL), lambda i: (0, 0), pipeline_mode=const),
                  pl.BlockSpec((D_MODEL, D_MODEL), lambda i: (0, 0), pipeline_mode=const)],
        out_specs=pl.BlockSpec((tm, D_MODEL), lambda i: (i, 0)),
        out_shape=jax.ShapeDtypeStruct((n_tok, D_MODEL), F32),
        compiler_params=pltpu.CompilerParams(
            dimension_semantics=("parallel",), vmem_limit_bytes=VMEM_LIMIT),
    )(oa, ob, proj, proj, b_gate, x2, pa, pb, wo)


def _ffn_kernel(h_ref, g_ref, wu_ref, wd_ref, o_ref, hn_ref, acc_ref):
    f = pl.program_id(1)

    @pl.when(f == 0)
    def _():
        hn_ref[...] = _rms(h_ref[...], g_ref[...]).astype(BF16)

    u = jnp.maximum(jnp.dot(hn_ref[...], wu_ref[...], preferred_element_type=F32), 0.0)
    part = jnp.dot((u * u).astype(BF16), wd_ref[...], preferred_element_type=F32)

    @pl.when(f == 0)
    def _():
        acc_ref[...] = part

    @pl.when(f > 0)
    def _():
        acc_ref[...] += part

    @pl.when(f == pl.num_programs(1) - 1)
    def _():
        o_ref[...] = h_ref[...] + acc_ref[...]


def _ffn(h, gain, wu, wd, tm=512, tf=1024):
    n_tok = h.shape[0]
    return pl.pallas_call(
        _ffn_kernel,
        name="ffn",
        grid=(n_tok // tm, D_FF // tf),
        in_specs=[pl.BlockSpec((tm, D_MODEL), lambda i, f: (i, 0)),
                  pl.BlockSpec((1, D_MODEL), lambda i, f: (0, 0)),
                  pl.BlockSpec((D_MODEL, tf), lambda i, f: (0, f)),
                  pl.BlockSpec((tf, D_MODEL), lambda i, f: (f, 0))],
        out_specs=pl.BlockSpec((tm, D_MODEL), lambda i, f: (i, 0)),
        out_shape=jax.ShapeDtypeStruct((n_tok, D_MODEL), F32),
        scratch_shapes=[pltpu.VMEM((tm, D_MODEL), BF16), pltpu.VMEM((tm, D_MODEL), F32)],
        compiler_params=pltpu.CompilerParams(
            dimension_semantics=("parallel", "arbitrary"), vmem_limit_bytes=VMEM_LIMIT),
    )(h, gain, wu, wd)


def kernel(x, norm_mix, w_in, b_gate, q_norm_a, k_norm_a, q_norm_b, k_norm_b, rpb_b,
           w_proj_a, w_proj_b, w_out, norm_ffn, w_up, w_down):
    batch, seq, d_model = x.shape
    assert d_model == D_MODEL and seq % (DILATED_PAIRS[-1][1] * Q_BLK) == 0
    for win, dil in DILATED_PAIRS:
        assert win // (2 * dil) == HALF_WIN
    depth = norm_mix.shape[0]
    cos, sin = _rope_tables(seq)
    h = x.reshape(batch * seq, D_MODEL)
    for l in range(depth):
        proj = _in_proj(h, norm_mix[l].reshape(1, D_MODEL), _regroup_w_in(w_in[l]))
        proj3 = proj.reshape(batch, seq, D_PROJ)
        oa = _attn_a(proj3, cos, sin, q_norm_a[l].reshape(1, HEAD_DIM), k_norm_a[l].reshape(1, HEAD_DIM))
        bias = _neighbourhood_bias(rpb_b[l], seq // GRID_W)
        ob = _attn_b(proj3, q_norm_b[l].reshape(1, HEAD_DIM), k_norm_b[l].reshape(1, HEAD_DIM), bias)
        h = _mix_out(oa.reshape(batch * seq, GROUP_W), ob.reshape(batch * seq, GROUP_W), proj,
                     b_gate[l].reshape(1, 2 * D_MODEL), h,
                     w_proj_a[l].astype(BF16), w_proj_b[l].astype(BF16), w_out[l].astype(BF16))
        h = _ffn(h, norm_ffn[l].reshape(1, D_MODEL), w_up[l].astype(BF16), w_down[l].astype(BF16))
    return h.reshape(batch, seq, D_MODEL)
```

```python
import functools

import numpy as np
import jax
import jax.numpy as jnp
from jax import lax
from jax.experimental import pallas as pl
from jax.experimental.pallas import tpu as pltpu

D_MODEL = 2048
HEAD_DIM = 128
N_HEADS = 16
N_HEADS_A = 12
N_HEADS_B = 4
DILATED_PAIRS = ((128, 1), (512, 4), (2048, 16))
N_GROUPS_A = len(DILATED_PAIRS)
HEADS_PER_GROUP = 4
GROUP_W = HEADS_PER_GROUP * HEAD_DIM
SLOT_W = 3 * N_GROUPS_A * HEAD_DIM
GRID_W = 64
WIN_R = 8
WIN_C = 16
QKV_W = N_HEADS * HEAD_DIM
D_FF = 4 * D_MODEL
ROPE_THETA = 10000.0
EPS = 1e-6
NEG_INF = -1e30
SCALE = HEAD_DIM ** -0.5

A_W = HEADS_PER_GROUP * SLOT_W
D_PROJ = A_W + 3 * GROUP_W + 2 * D_MODEL
B_BLOCK0 = A_W // GROUP_W
GATE_BLOCK0 = (A_W + 3 * GROUP_W) // D_MODEL

VMEM_LIMIT = 56 * 1024 * 1024

F32 = jnp.float32
BF16 = jnp.bfloat16


def _regroup_w_in(w):
    d = w.shape[0]
    qkv = w[:, :3 * QKV_W].astype(BF16).reshape(d, 3, N_HEADS, HEAD_DIM)
    a = qkv[:, :, :N_HEADS_A].reshape(d, 3, N_GROUPS_A, HEADS_PER_GROUP, HEAD_DIM)
    a = a.transpose(0, 3, 2, 1, 4).reshape(d, A_W)
    b = qkv[:, :, N_HEADS_A:].reshape(d, 3 * GROUP_W)
    return jnp.concatenate([a, b, w[:, 3 * QKV_W:].astype(BF16)], axis=1)


def _rope_tables(seq):
    pos = np.arange(seq, dtype=np.float64)
    inv = ROPE_THETA ** (-np.arange(0, HEAD_DIM, 2, dtype=np.float64) / HEAD_DIM)
    ang = pos[:, None] * inv[None, :]
    cos = np.concatenate([np.cos(ang), np.cos(ang)], axis=-1)
    sin = np.concatenate([-np.sin(ang), np.sin(ang)], axis=-1)
    return jnp.asarray(cos, F32), jnp.asarray(sin, F32)


def _rms(x, g):
    ms = jnp.mean(x * x, axis=-1, keepdims=True)
    return (x * lax.rsqrt(ms + EPS)) * g


def _mean_weights():
    return jnp.full((2 * HEAD_DIM, HEAD_DIM), 1.0 / HEAD_DIM, BF16)


def _lane_meansq(raw, mean_w):
    xf = raw.astype(F32)
    sq = xf * xf
    hi = sq.astype(BF16)
    lo = (sq - hi.astype(F32)).astype(BF16)
    return jnp.dot(jnp.concatenate([hi, lo], axis=1), mean_w, preferred_element_type=F32)


def _in_proj_kernel(x_ref, g_ref, w_ref, o_ref, xn_ref):
    @pl.when(pl.program_id(1) == 0)
    def _():
        xn_ref[...] = _rms(x_ref[...], g_ref[...]).astype(BF16)

    o_ref[...] = jnp.dot(xn_ref[...], w_ref[...], preferred_element_type=F32).astype(o_ref.dtype)


def _in_proj(x2, gain, w, tm=1024, tn=1024):
    n_tok = x2.shape[0]
    return pl.pallas_call(
        _in_proj_kernel,
        name="in_proj",
        grid=(n_tok // tm, D_PROJ // tn),
        in_specs=[
            pl.BlockSpec((tm, D_MODEL), lambda i, j: (i, 0)),
            pl.BlockSpec((1, D_MODEL), lambda i, j: (0, 0)),
            pl.BlockSpec((D_MODEL, tn), lambda i, j: (0, j)),
        ],
        out_specs=pl.BlockSpec((tm, tn), lambda i, j: (i, j)),
        out_shape=jax.ShapeDtypeStruct((n_tok, D_PROJ), BF16),
        scratch_shapes=[pltpu.VMEM((tm, D_MODEL), BF16)],
        compiler_params=pltpu.CompilerParams(
            dimension_semantics=("parallel", "arbitrary"), vmem_limit_bytes=VMEM_LIMIT),
    )(x2, gain, w)


Q_BLK = 128
HALF_WIN = 64
A_WIN = 2 * Q_BLK


def _band_masks():
    rel = np.arange(A_WIN)[None, :] - np.arange(Q_BLK)[:, None]
    tiles = [np.where(np.abs(rel - off) <= HALF_WIN, 0.0, NEG_INF) for off in (0, HALF_WIN, 2 * HALF_WIN)]
    return jnp.asarray(np.stack(tiles), F32)


def _attn_a_kernel(qkv_ref, qc_ref, qs_ref, kc_ref, ks_ref, mask_ref, o_ref,
                   qn_ref, kn_ref, vn_ref, f_ref, og_ref, lg_ref, *, seq, dils):
    n_blk = seq // Q_BLK
    n_g = len(dils)
    lane = lax.broadcasted_iota(jnp.int32, (HEAD_DIM, HEAD_DIM), 1)
    sub = lax.broadcasted_iota(jnp.int32, (HEAD_DIM, HEAD_DIM), 0)
    half_turn = jnp.where(lane == (sub + HEAD_DIM // 2) % HEAD_DIM, 1.0, 0.0).astype(BF16)
    mean_w = _mean_weights()
    vn_ref[:, :, HEAD_DIM:] = jnp.ones((n_g, seq, HEAD_DIM), BF16)

    stage = {}
    for g, dil in enumerate(dils):
        if dil > 1:
            stage[g] = tuple(f_ref.at[3 * len(stage) + t] for t in range(3))

    def norm_rope(raw, cos_g, sin_g):
        inv_rms = lax.rsqrt(_lane_meansq(raw, mean_w) + EPS)
        turned = jnp.dot(raw, half_turn, preferred_element_type=F32)
        return (raw.astype(F32) * cos_g + turned * sin_g) * inv_rms

    def block_rows(n):
        return pl.ds(pl.multiple_of(n * Q_BLK, Q_BLK), Q_BLK)

    def token_rows(n, dil):
        if dil == 1:
            return block_rows(n)
        sub_len = seq // dil
        base = n * Q_BLK
        r = base // sub_len
        return pl.ds(r + (base - r * sub_len) * dil, Q_BLK, stride=dil)

    def prep(n, c):
        rows = block_rows(n)
        for g, dil in enumerate(dils):
            c0 = 3 * g * HEAD_DIM
            qh = norm_rope(qkv_ref[rows, c0:c0 + HEAD_DIM], qc_ref[rows, :], qs_ref[rows, :])
            kh = norm_rope(qkv_ref[rows, c0 + HEAD_DIM:c0 + 2 * HEAD_DIM], kc_ref[rows, :], ks_ref[rows, :])
            v = qkv_ref[rows, c0 + 2 * HEAD_DIM:c0 + 3 * HEAD_DIM]
            if dil == 1:
                qn_ref[g, rows, :] = qh.astype(BF16)
                kn_ref[g, rows, :] = kh.astype(BF16)
                vn_ref[g, rows, :HEAD_DIM] = v
            else:
                fq, fk, fv = stage[g]
                fq[rows, :] = qh
                fk[rows, :] = kh
                fv[rows, :] = v.astype(F32)
        return c

    lax.fori_loop(0, n_blk, prep, 0)

    def gather(n, c):
        dst = block_rows(n)
        for g, dil in enumerate(dils):
            if dil > 1:
                fq, fk, fv = stage[g]
                src = token_rows(n, dil)
                qn_ref[g, dst, :] = fq[src, :].astype(BF16)
                kn_ref[g, dst, :] = fk[src, :].astype(BF16)
                vn_ref[g, dst, :HEAD_DIM] = fv[src, :].astype(BF16)
        return c

    lax.fori_loop(0, n_blk, gather, 0, unroll=2)

    def scatter_later(dil):
        return dil % 8 == 0

    def block(n, c):
        q0 = pl.multiple_of(n * Q_BLK, Q_BLK)
        for g, dil in enumerate(dils):
            sub_len = seq // dil
            win = min(A_WIN, sub_len)
            sub0 = (n // (sub_len // Q_BLK)) * sub_len
            q_loc = q0 - sub0
            k_loc = jnp.clip(q_loc - HALF_WIN, 0, sub_len - win)
            k0 = pl.multiple_of(sub0 + k_loc, HALF_WIN)
            band = mask_ref[(q_loc - k_loc) // HALF_WIN, :, :win]
            q = qn_ref[g, pl.ds(q0, Q_BLK), :]
            k = kn_ref[g, pl.ds(k0, win), :]
            s = lax.dot_general(q, k, (((1,), (1,)), ((), ())), preferred_element_type=F32) + band
            m = jnp.max(s, axis=-1, keepdims=True)
            p = jnp.exp(s - m).astype(BF16)
            acc = jnp.dot(p, vn_ref[g, pl.ds(k0, win), :], preferred_element_type=F32)
            denom = acc[:, HEAD_DIM:]
            out = acc[:, :HEAD_DIM] * (1.0 / denom)
            lse = m + jnp.log(denom)
            if scatter_later(dil):
                stage[g][0][pl.ds(q0, Q_BLK), :] = out
                stage[g][1][pl.ds(q0, Q_BLK), :] = lse
            else:
                dst = token_rows(n, dil)
                og_ref.at[g][dst, :] = out
                lg_ref.at[g][dst, :] = lse
        return c

    lax.fori_loop(0, n_blk, block, 0, unroll=2)

    def scatter(n, c):
        src = block_rows(n)
        for g, dil in enumerate(dils):
            if scatter_later(dil):
                dst = token_rows(n, dil)
                og_ref.at[g][dst, :] = stage[g][0][src, :]
                lg_ref.at[g][dst, :] = stage[g][1][src, :]
        return c

    if any(scatter_later(dil) for dil in dils):
        lax.fori_loop(0, n_blk, scatter, 0, unroll=2)

    def combine(n, c):
        rows = block_rows(n)
        lses = [lg_ref[g, rows, :] for g in range(n_g)]
        mx = functools.reduce(jnp.maximum, lses)
        es = [jnp.exp(x - mx) for x in lses]
        inv = 1.0 / functools.reduce(lambda a, b: a + b, es)
        oa = (es[0] * inv) * og_ref[0, rows, :]
        for g in range(1, n_g):
            oa = oa + (es[g] * inv) * og_ref[g, rows, :]
        o_ref[rows, :] = oa.astype(o_ref.dtype)
        return c

    lax.fori_loop(0, n_blk, combine, 0, unroll=2)


def _attn_a(proj3, cos, sin, gq, gk):
    batch, seq, _ = proj3.shape
    dils = tuple(d for _, d in DILATED_PAIRS)
    n_g = len(dils)
    n_dilated = sum(d > 1 for d in dils)
    half = HEAD_DIM // 2
    tables = (cos * (gq * SCALE), sin * (jnp.roll(gq, half) * SCALE), cos * gk, sin * jnp.roll(gk, half))
    const = pl.Buffered(1)
    rope_spec = pl.BlockSpec((seq, HEAD_DIM), lambda b, s: (0, 0), pipeline_mode=const)
    masks = _band_masks()
    return pl.pallas_call(
        functools.partial(_attn_a_kernel, seq=seq, dils=dils),
        name="attn_a",
        grid=(batch, HEADS_PER_GROUP),
        in_specs=[pl.BlockSpec((None, seq, SLOT_W), lambda b, s: (b, 0, s)),
                  rope_spec, rope_spec, rope_spec, rope_spec,
                  pl.BlockSpec(masks.shape, lambda b, s: (0, 0, 0), pipeline_mode=const)],
        out_specs=pl.BlockSpec((None, seq, HEAD_DIM), lambda b, s: (b, 0, s)),
        out_shape=jax.ShapeDtypeStruct((batch, seq, GROUP_W), BF16),
        scratch_shapes=[pltpu.VMEM((n_g, seq, HEAD_DIM), BF16)] * 2
        + [pltpu.VMEM((n_g, seq, 2 * HEAD_DIM), BF16)]
        + [pltpu.VMEM((3 * n_dilated, seq, HEAD_DIM), F32)]
        + [pltpu.VMEM((n_g, seq, HEAD_DIM), F32)] * 2,
        compiler_params=pltpu.CompilerParams(
            dimension_semantics=("parallel", "arbitrary"), vmem_limit_bytes=VMEM_LIMIT),
    )(proj3, *tables, masks)


B_QROWS = 4
B_KROWS = B_QROWS + WIN_R
B_NQ = B_QROWS * GRID_W
B_NK = B_KROWS * GRID_W


def _attn_b_kernel(q_ref, k_ref, v_ref, gq_ref, gk_ref, bias_ref, o_ref, qn_ref, kn_ref, *, rows):
    gq = gq_ref[...] * SCALE
    gk = gk_ref[...]
    n_grp = rows // B_QROWS
    mean_w = _mean_weights()

    def norm(raw, gain):
        inv_rms = lax.rsqrt(_lane_meansq(raw, mean_w) + EPS)
        return ((raw.astype(F32) * inv_rms) * gain).astype(BF16)

    def prep(n, c):
        rows_n = pl.ds(pl.multiple_of(n * Q_BLK, Q_BLK), Q_BLK)
        for h in range(N_HEADS_B):
            sl = slice(h * HEAD_DIM, (h + 1) * HEAD_DIM)
            qn_ref[rows_n, sl] = norm(q_ref[rows_n, sl], gq)
            kn_ref[rows_n, sl] = norm(k_ref[rows_n, sl], gk)
        return c

    lax.fori_loop(0, rows * GRID_W // Q_BLK, prep, 0)

    def group(i, c):
        r0 = i * B_QROWS
        ws = jnp.clip(r0 - WIN_R // 2, 0, rows - B_KROWS)
        case = jnp.where(i == 0, 0, jnp.where(i == n_grp - 1, 2, 1))
        q0 = pl.multiple_of(r0 * GRID_W, B_NQ)
        k0 = pl.multiple_of(ws * GRID_W, GRID_W)
        for h in range(N_HEADS_B):
            sl = slice(h * HEAD_DIM, (h + 1) * HEAD_DIM)
            q = qn_ref[pl.ds(q0, B_NQ), sl]
            k = kn_ref[pl.ds(k0, B_NK), sl]
            v = v_ref[pl.ds(k0, B_NK), sl]
            s = lax.dot_general(q, k, (((1,), (1,)), ((), ())), preferred_element_type=F32)
            s = s + bias_ref[h, case]
            m = jnp.max(s, axis=-1, keepdims=True)
            p = jnp.exp(s - m)
            l = jnp.sum(p, axis=-1, keepdims=True)
            acc = jnp.dot(p.astype(BF16), v, preferred_element_type=F32)
            o_ref[pl.ds(q0, B_NQ), sl] = (acc * (1.0 / l)).astype(o_ref.dtype)
        return c

    lax.fori_loop(0, n_grp, group, 0)


def _neighbourhood_bias(rpb, rows):
    n_grp = rows // B_QROWS
    assert rows % B_QROWS == 0 and n_grp >= 3
    n_h = rpb.shape[0]
    c = np.arange(GRID_W)
    dc = np.clip(c[None, :] - c[:, None], -(WIN_C - 1), WIN_C - 1) + (WIN_C - 1)
    col_start = np.clip(c - WIN_C // 2, 0, GRID_W - WIN_C)
    col_ok = (c[None, :] >= col_start[:, None]) & (c[None, :] < col_start[:, None] + WIN_C)
    col_pick = np.zeros((2 * WIN_C - 1, GRID_W * GRID_W), np.float32)
    col_pick[dc.ravel(), np.arange(GRID_W * GRID_W)] = 1.0
    row_pick = np.zeros((3, B_QROWS, B_KROWS, 2 * WIN_R - 1), np.float32)
    row_ok = np.zeros((3, B_QROWS, B_KROWS), bool)
    for case, i in enumerate((0, 1, n_grp - 1)):
        r0 = i * B_QROWS
        ws = int(np.clip(r0 - WIN_R // 2, 0, rows - B_KROWS))
        for rq in range(B_QROWS):
            r = r0 + rq
            rs = int(np.clip(r - WIN_R // 2, 0, rows - WIN_R))
            for jr in range(B_KROWS):
                key_row = ws + jr
                if rs <= key_row < rs + WIN_R:
                    row_pick[case, rq, jr, key_row - r + WIN_R - 1] = 1.0
                    row_ok[case, rq, jr] = True
    tab = jnp.einsum("xa,hab,bc->hxc", row_pick.reshape(-1, 2 * WIN_R - 1), rpb.astype(F32), col_pick,
                     precision=lax.Precision.HIGHEST)
    ok = row_ok.reshape(-1)[:, None] & col_ok.reshape(-1)[None, :]
    tab = jnp.where(ok[None], tab, NEG_INF)
    tab = tab.reshape(n_h, 3, B_QROWS, B_KROWS, GRID_W, GRID_W).transpose(0, 1, 2, 4, 3, 5)
    return tab.reshape(n_h, 3, B_NQ, B_NK)


def _attn_b(proj3, gq, gk, bias):
    batch, seq, _ = proj3.shape
    rows = seq // GRID_W

    def qkv_spec(t):
        return pl.BlockSpec((None, seq, GROUP_W), lambda b: (b, 0, B_BLOCK0 + t))

    gain_spec = pl.BlockSpec((1, HEAD_DIM), lambda b: (0, 0))
    bias_spec = pl.BlockSpec(bias.shape, lambda b: (0, 0, 0, 0), pipeline_mode=pl.Buffered(1))
    return pl.pallas_call(
        functools.partial(_attn_b_kernel, rows=rows),
        name="attn_b",
        grid=(batch,),
        in_specs=[qkv_spec(0), qkv_spec(1), qkv_spec(2), gain_spec, gain_spec, bias_spec],
        out_specs=pl.BlockSpec((None, seq, GROUP_W), lambda b: (b, 0, 0)),
        out_shape=jax.ShapeDtypeStruct((batch, seq, GROUP_W), BF16),
        scratch_shapes=[pltpu.VMEM((seq, GROUP_W), BF16), pltpu.VMEM((seq, GROUP_W), BF16)],
        compiler_params=pltpu.CompilerParams(
            dimension_semantics=("parallel",), vmem_limit_bytes=VMEM_LIMIT),
    )(proj3, proj3, proj3, gq, gk, bias)


def _mix_out_kernel(oa_ref, ob_ref, ga_ref, gb_ref, bg_ref, x_ref, pa_ref, pb_ref, wo_ref, h_ref):
    ya = jnp.dot(oa_ref[...], pa_ref[...], preferred_element_type=F32)
    yb = jnp.dot(ob_ref[...], pb_ref[...], preferred_element_type=F32)
    ga = jax.nn.sigmoid(ga_ref[...].astype(F32) + bg_ref[:, :D_MODEL])
    gb = jax.nn.sigmoid(gb_ref[...].astype(F32) + bg_ref[:, D_MODEL:])
    mixed = (ga * ya + gb * yb).astype(BF16)
    h_ref[...] = x_ref[...] + jnp.dot(mixed, wo_ref[...], preferred_element_type=F32)


def _mix_out(oa, ob, proj, b_gate, x2, pa, pb, wo, tm=512):
    n_tok = x2.shape[0]
    row512 = pl.BlockSpec((tm, GROUP_W), lambda i: (i, 0))
    const = pl.Buffered(1)
    return pl.pallas_call(
        _mix_out_kernel,
        name="mix_out",
        grid=(n_tok // tm,),
        in_specs=[row512, row512,
                  pl.BlockSpec((tm, D_MODEL), lambda i: (i, GATE_BLOCK0)),
                  pl.BlockSpec((tm, D_MODEL), lambda i: (i, GATE_BLOCK0 + 1)),
                  pl.BlockSpec((1, 2 * D_MODEL), lambda i: (0, 0)),
                  pl.BlockSpec((tm, D_MODEL), lambda i: (i, 0)),
                  pl.BlockSpec((GROUP_W, D_MODEL), lambda i: (0, 0), pipeline_mode=const),
                  pl.BlockSpec((GROUP_W, D_MODEL), lambda i: (0, 0), pipeline_mode=const),
                  pl.BlockSpec((D_MODEL, D_MODEL), lambda i: (0, 0), pipeline_mode=const)],
        out_specs=pl.BlockSpec((tm, D_MODEL), lambda i: (i, 0)),
        out_shape=jax.ShapeDtypeStruct((n_tok, D_MODEL), F32),
        compiler_params=pltpu.CompilerParams(
            dimension_semantics=("parallel",), vmem_limit_bytes=VMEM_LIMIT),
    )(oa, ob, proj, proj, b_gate, x2, pa, pb, wo)


def _ffn_kernel(h_ref, g_ref, wu_ref, wd_ref, o_ref, hn_ref):
    @pl.when(pl.program_id(1) == 0)
    def _():
        h = h_ref[...]
        hn_ref[...] = _rms(h, g_ref[...]).astype(BF16)
        o_ref[...] = h

    u = jnp.maximum(jnp.dot(hn_ref[...], wu_ref[...], preferred_element_type=F32), 0.0)
    o_ref[...] += jnp.dot((u * u).astype(BF16), wd_ref[...], preferred_element_type=F32)


def _ffn(h, gain, wu, wd, tm=512, tf=1024):
    n_tok = h.shape[0]
    return pl.pallas_call(
        _ffn_kernel,
        name="ffn",
        grid=(n_tok // tm, D_FF // tf),
        in_specs=[pl.BlockSpec((tm, D_MODEL), lambda i, f: (i, 0)),
                  pl.BlockSpec((1, D_MODEL), lambda i, f: (0, 0)),
                  pl.BlockSpec((D_MODEL, tf), lambda i, f: (0, f)),
                  pl.BlockSpec((tf, D_MODEL), lambda i, f: (f, 0))],
        out_specs=pl.BlockSpec((tm, D_MODEL), lambda i, f: (i, 0)),
        out_shape=jax.ShapeDtypeStruct((n_tok, D_MODEL), F32),
        scratch_shapes=[pltpu.VMEM((tm, D_MODEL), BF16)],
        compiler_params=pltpu.CompilerParams(
            dimension_semantics=("parallel", "arbitrary"), vmem_limit_bytes=VMEM_LIMIT),
    )(h, gain, wu, wd)


def kernel(x, norm_mix, w_in, b_gate, q_norm_a, k_norm_a, q_norm_b, k_norm_b, rpb_b,
           w_proj_a, w_proj_b, w_out, norm_ffn, w_up, w_down):
    batch, seq, d_model = x.shape
    assert d_model == D_MODEL and seq % (DILATED_PAIRS[-1][1] * Q_BLK) == 0
    for win, dil in DILATED_PAIRS:
        assert win // (2 * dil) == HALF_WIN
    depth = norm_mix.shape[0]
    cos, sin = _rope_tables(seq)
    h = x.reshape(batch * seq, D_MODEL)
    for l in range(depth):
        proj = _in_proj(h, norm_mix[l].reshape(1, D_MODEL), _regroup_w_in(w_in[l]))
        proj3 = proj.reshape(batch, seq, D_PROJ)
        oa = _attn_a(proj3, cos, sin, q_norm_a[l].reshape(1, HEAD_DIM), k_norm_a[l].reshape(1, HEAD_DIM))
        bias = _neighbourhood_bias(rpb_b[l], seq // GRID_W)
        ob = _attn_b(proj3, q_norm_b[l].reshape(1, HEAD_DIM), k_norm_b[l].reshape(1, HEAD_DIM), bias)
        h = _mix_out(oa.reshape(batch * seq, GROUP_W), ob.reshape(batch * seq, GROUP_W), proj,
                     b_gate[l].reshape(1, 2 * D_MODEL), h,
                     w_proj_a[l].astype(BF16), w_proj_b[l].astype(BF16), w_out[l].astype(BF16))
        h = _ffn(h, norm_ffn[l].reshape(1, D_MODEL), w_up[l].astype(BF16), w_down[l].astype(BF16))
    return h.reshape(batch, seq, D_MODEL)
```

```python
import functools

import numpy as np
import jax
import jax.numpy as jnp
from jax import lax
from jax.experimental import pallas as pl
from jax.experimental.pallas import tpu as pltpu

D_MODEL = 2048
HEAD_DIM = 128
N_HEADS = 16
N_HEADS_A = 12
N_HEADS_B = 4
DILATED_PAIRS = ((128, 1), (512, 4), (2048, 16))
N_GROUPS_A = len(DILATED_PAIRS)
HEADS_PER_GROUP = 4
GROUP_W = HEADS_PER_GROUP * HEAD_DIM
GRID_W = 64
WIN_R = 8
WIN_C = 16
QKV_W = N_HEADS * HEAD_DIM
D_FF = 4 * D_MODEL
ROPE_THETA = 10000.0
EPS = 1e-6
NEG_INF = -1e30
SCALE = HEAD_DIM ** -0.5

D_PROJ = 3 * QKV_W + 2 * D_MODEL
B_BLOCK0 = N_HEADS_A * HEAD_DIM // GROUP_W
GATE_BLOCK0 = 3 * QKV_W // D_MODEL

VMEM_LIMIT = 56 * 1024 * 1024

F32 = jnp.float32
BF16 = jnp.bfloat16


def _rope_tables(seq):
    pos = np.arange(seq, dtype=np.float64)
    inv = ROPE_THETA ** (-np.arange(0, HEAD_DIM, 2, dtype=np.float64) / HEAD_DIM)
    ang = pos[:, None] * inv[None, :]
    cos = np.concatenate([np.cos(ang), np.cos(ang)], axis=-1)
    sin = np.concatenate([-np.sin(ang), np.sin(ang)], axis=-1)
    return jnp.asarray(cos, F32), jnp.asarray(sin, F32)


def _rms(x, g):
    ms = jnp.mean(x * x, axis=-1, keepdims=True)
    return (x * lax.rsqrt(ms + EPS)) * g


def _mean_weights():
    return jnp.full((2 * HEAD_DIM, HEAD_DIM), 1.0 / HEAD_DIM, BF16)


def _lane_meansq(raw, mean_w):
    xf = raw.astype(F32)
    sq = xf * xf
    hi = sq.astype(BF16)
    lo = (sq - hi.astype(F32)).astype(BF16)
    return jnp.dot(jnp.concatenate([hi, lo], axis=1), mean_w, preferred_element_type=F32)


def _in_proj_kernel(x_ref, g_ref, w_ref, o_ref, xn_ref):
    @pl.when(pl.program_id(1) == 0)
    def _():
        xn_ref[...] = _rms(x_ref[...], g_ref[...]).astype(BF16)

    o_ref[...] = jnp.dot(xn_ref[...], w_ref[...], preferred_element_type=F32).astype(o_ref.dtype)


def _in_proj(x2, gain, w, tm=1024, tn=1024):
    n_tok = x2.shape[0]
    return pl.pallas_call(
        _in_proj_kernel,
        name="in_proj",
        grid=(n_tok // tm, D_PROJ // tn),
        in_specs=[
            pl.BlockSpec((tm, D_MODEL), lambda i, j: (i, 0)),
            pl.BlockSpec((1, D_MODEL), lambda i, j: (0, 0)),
            pl.BlockSpec((D_MODEL, tn), lambda i, j: (0, j)),
        ],
        out_specs=pl.BlockSpec((tm, tn), lambda i, j: (i, j)),
        out_shape=jax.ShapeDtypeStruct((n_tok, D_PROJ), BF16),
        scratch_shapes=[pltpu.VMEM((tm, D_MODEL), BF16)],
        compiler_params=pltpu.CompilerParams(
            dimension_semantics=("parallel", "arbitrary"), vmem_limit_bytes=VMEM_LIMIT),
    )(x2, gain, w)


Q_BLK = 128
HALF_WIN = 64
A_WIN = 2 * Q_BLK


def _band_masks():
    rel = np.arange(A_WIN)[None, :] - np.arange(Q_BLK)[:, None]
    tiles = [np.where(np.abs(rel - off) <= HALF_WIN, 0.0, NEG_INF) for off in (0, HALF_WIN, 2 * HALF_WIN)]
    return jnp.asarray(np.stack(tiles), F32)


def _attn_a_kernel(*refs, seq, dils):
    n_g = len(dils)
    qkv_refs = refs[:3 * n_g]
    (qc_ref, qs_ref, kc_ref, ks_ref, mask_ref, o_ref,
     qn_ref, kn_ref, vn_ref, f_ref, og_ref, lg_ref) = refs[3 * n_g:]
    n_blk = seq // Q_BLK
    lane = lax.broadcasted_iota(jnp.int32, (HEAD_DIM, HEAD_DIM), 1)
    sub = lax.broadcasted_iota(jnp.int32, (HEAD_DIM, HEAD_DIM), 0)
    half_turn = jnp.where(lane == (sub + HEAD_DIM // 2) % HEAD_DIM, 1.0, 0.0).astype(BF16)
    mean_w = _mean_weights()
    vn_ref[:, :, HEAD_DIM:] = jnp.ones((n_g, seq, HEAD_DIM), BF16)

    stage = {}
    for g, dil in enumerate(dils):
        if dil > 1:
            stage[g] = tuple(f_ref.at[3 * len(stage) + t] for t in range(3))

    def norm_rope(raw, cos_g, sin_g):
        inv_rms = lax.rsqrt(_lane_meansq(raw, mean_w) + EPS)
        turned = jnp.dot(raw, half_turn, preferred_element_type=F32)
        return (raw.astype(F32) * cos_g + turned * sin_g) * inv_rms

    def block_rows(n):
        return pl.ds(pl.multiple_of(n * Q_BLK, Q_BLK), Q_BLK)

    def token_rows(n, dil):
        if dil == 1:
            return block_rows(n)
        sub_len = seq // dil
        base = n * Q_BLK
        r = base // sub_len
        return pl.ds(r + (base - r * sub_len) * dil, Q_BLK, stride=dil)

    def prep(n, c):
        rows = block_rows(n)
        for g, dil in enumerate(dils):
            q_ref, k_ref, v_ref = qkv_refs[3 * g:3 * g + 3]
            qh = norm_rope(q_ref[rows, :], qc_ref[rows, :], qs_ref[rows, :])
            kh = norm_rope(k_ref[rows, :], kc_ref[rows, :], ks_ref[rows, :])
            v = v_ref[rows, :]
            if dil == 1:
                qn_ref[g, rows, :] = qh.astype(BF16)
                kn_ref[g, rows, :] = kh.astype(BF16)
                vn_ref[g, rows, :HEAD_DIM] = v
            else:
                fq, fk, fv = stage[g]
                fq[rows, :] = qh
                fk[rows, :] = kh
                fv[rows, :] = v.astype(F32)
        return c

    lax.fori_loop(0, n_blk, prep, 0)

    def gather(n, c):
        dst = block_rows(n)
        for g, dil in enumerate(dils):
            if dil > 1:
                fq, fk, fv = stage[g]
                src = token_rows(n, dil)
                qn_ref[g, dst, :] = fq[src, :].astype(BF16)
                kn_ref[g, dst, :] = fk[src, :].astype(BF16)
                vn_ref[g, dst, :HEAD_DIM] = fv[src, :].astype(BF16)
        return c

    lax.fori_loop(0, n_blk, gather, 0, unroll=2)

    def scatter_later(dil):
        return dil % 8 == 0

    def block(n, c):
        q0 = pl.multiple_of(n * Q_BLK, Q_BLK)
        for g, dil in enumerate(dils):
            sub_len = seq // dil
            win = min(A_WIN, sub_len)
            sub0 = (n // (sub_len // Q_BLK)) * sub_len
            q_loc = q0 - sub0
            k_loc = jnp.clip(q_loc - HALF_WIN, 0, sub_len - win)
            k0 = pl.multiple_of(sub0 + k_loc, HALF_WIN)
            band = mask_ref[(q_loc - k_loc) // HALF_WIN, :, :win]
            q = qn_ref[g, pl.ds(q0, Q_BLK), :]
            k = kn_ref[g, pl.ds(k0, win), :]
            s = lax.dot_general(q, k, (((1,), (1,)), ((), ())), preferred_element_type=F32) + band
            m = jnp.max(s, axis=-1, keepdims=True)
            p = jnp.exp(s - m).astype(BF16)
            acc = jnp.dot(p, vn_ref[g, pl.ds(k0, win), :], preferred_element_type=F32)
            denom = acc[:, HEAD_DIM:]
            out = acc[:, :HEAD_DIM] * (1.0 / denom)
            lse = m + jnp.log(denom)
            if scatter_later(dil):
                stage[g][0][pl.ds(q0, Q_BLK), :] = out
                stage[g][1][pl.ds(q0, Q_BLK), :] = lse
            else:
                dst = token_rows(n, dil)
                og_ref.at[g][dst, :] = out
                lg_ref.at[g][dst, :] = lse
        return c

    lax.fori_loop(0, n_blk, block, 0, unroll=2)

    def scatter(n, c):
        src = block_rows(n)
        for g, dil in enumerate(dils):
            if scatter_later(dil):
                dst = token_rows(n, dil)
                og_ref.at[g][dst, :] = stage[g][0][src, :]
                lg_ref.at[g][dst, :] = stage[g][1][src, :]
        return c

    if any(scatter_later(dil) for dil in dils):
        lax.fori_loop(0, n_blk, scatter, 0, unroll=2)

    def combine(n, c):
        rows = block_rows(n)
        lses = [lg_ref[g, rows, :] for g in range(n_g)]
        mx = functools.reduce(jnp.maximum, lses)
        es = [jnp.exp(x - mx) for x in lses]
        inv = 1.0 / functools.reduce(lambda a, b: a + b, es)
        oa = (es[0] * inv) * og_ref[0, rows, :]
        for g in range(1, n_g):
            oa = oa + (es[g] * inv) * og_ref[g, rows, :]
        o_ref[rows, :] = oa.astype(o_ref.dtype)
        return c

    lax.fori_loop(0, n_blk, combine, 0, unroll=2)


def _attn_a(proj3, cos, sin, gq, gk):
    batch, seq, _ = proj3.shape
    dils = tuple(d for _, d in DILATED_PAIRS)
    n_g = len(dils)
    n_dilated = sum(d > 1 for d in dils)
    half = HEAD_DIM // 2
    tables = (cos * (gq * SCALE), sin * (jnp.roll(gq, half) * SCALE), cos * gk, sin * jnp.roll(gk, half))
    const = pl.Buffered(1)
    rope_spec = pl.BlockSpec((seq, HEAD_DIM), lambda b, s: (0, 0), pipeline_mode=const)
    masks = _band_masks()
    qkv_specs = [pl.BlockSpec((None, seq, HEAD_DIM),
                              lambda b, s, t=t, g=g: (b, 0, t * N_HEADS + g * HEADS_PER_GROUP + s))
                 for g in range(n_g) for t in range(3)]
    return pl.pallas_call(
        functools.partial(_attn_a_kernel, seq=seq, dils=dils),
        name="attn_a",
        grid=(batch, HEADS_PER_GROUP),
        in_specs=qkv_specs + [rope_spec, rope_spec, rope_spec, rope_spec,
                              pl.BlockSpec(masks.shape, lambda b, s: (0, 0, 0), pipeline_mode=const)],
        out_specs=pl.BlockSpec((None, seq, HEAD_DIM), lambda b, s: (b, 0, s)),
        out_shape=jax.ShapeDtypeStruct((batch, seq, GROUP_W), BF16),
        scratch_shapes=[pltpu.VMEM((n_g, seq, HEAD_DIM), BF16)] * 2
        + [pltpu.VMEM((n_g, seq, 2 * HEAD_DIM), BF16)]
        + [pltpu.VMEM((3 * n_dilated, seq, HEAD_DIM), F32)]
        + [pltpu.VMEM((n_g, seq, HEAD_DIM), F32)] * 2,
        compiler_params=pltpu.CompilerParams(
            dimension_semantics=("parallel", "arbitrary"), vmem_limit_bytes=VMEM_LIMIT),
    )(*([proj3] * (3 * n_g)), *tables, masks)


B_QROWS = 4
B_KROWS = B_QROWS + WIN_R
B_NQ = B_QROWS * GRID_W
B_NK = B_KROWS * GRID_W


def _window_row_offsets(rows):
    n_grp = rows // B_QROWS
    assert rows % B_QROWS == 0 and n_grp >= 3
    masked = 2 * WIN_R - 1
    table = []
    for i in (0, 1, n_grp - 1):
        r0 = i * B_QROWS
        ws = int(np.clip(r0 - WIN_R // 2, 0, rows - B_KROWS))
        per_q = []
        for rq in range(B_QROWS):
            r = r0 + rq
            rs = int(np.clip(r - WIN_R // 2, 0, rows - WIN_R))
            per_q.append([ws + jr - r + WIN_R - 1 if rs <= ws + jr < rs + WIN_R else masked
                          for jr in range(B_KROWS)])
        table.append(per_q)
    return table


def _attn_b_kernel(q_ref, k_ref, v_ref, gq_ref, gk_ref, tile_ref, o_ref, qn_ref, kn_ref, bias_ref, *, rows):
    gq = gq_ref[...] * SCALE
    gk = gk_ref[...]
    n_grp = rows // B_QROWS
    mean_w = _mean_weights()

    @pl.when(pl.program_id(0) == 0)
    def _():
        right = lax.broadcasted_iota(jnp.int32, (GRID_W, 2 * GRID_W), 1) >= GRID_W
        for case, per_q in enumerate(_window_row_offsets(rows)):
            for rq, offs in enumerate(per_q):
                for h in range(N_HEADS_B):
                    for j in range(0, B_KROWS, 2):
                        pair = jnp.where(right, tile_ref[h, offs[j + 1]], tile_ref[h, offs[j]])
                        bias_ref[h, case, rq * GRID_W:(rq + 1) * GRID_W, j * GRID_W:(j + 2) * GRID_W] = pair

    def norm(raw, gain):
        inv_rms = lax.rsqrt(_lane_meansq(raw, mean_w) + EPS)
        return ((raw.astype(F32) * inv_rms) * gain).astype(BF16)

    def prep(n, c):
        rows_n = pl.ds(pl.multiple_of(n * Q_BLK, Q_BLK), Q_BLK)
        for h in range(N_HEADS_B):
            sl = slice(h * HEAD_DIM, (h + 1) * HEAD_DIM)
            qn_ref[rows_n, sl] = norm(q_ref[rows_n, sl], gq)
            kn_ref[rows_n, sl] = norm(k_ref[rows_n, sl], gk)
        return c

    lax.fori_loop(0, rows * GRID_W // Q_BLK, prep, 0)

    def group(i, c):
        r0 = i * B_QROWS
        ws = jnp.clip(r0 - WIN_R // 2, 0, rows - B_KROWS)
        case = jnp.where(i == 0, 0, jnp.where(i == n_grp - 1, 2, 1))
        q0 = pl.multiple_of(r0 * GRID_W, B_NQ)
        k0 = pl.multiple_of(ws * GRID_W, GRID_W)
        for h in range(N_HEADS_B):
            sl = slice(h * HEAD_DIM, (h + 1) * HEAD_DIM)
            q = qn_ref[pl.ds(q0, B_NQ), sl]
            k = kn_ref[pl.ds(k0, B_NK), sl]
            v = v_ref[pl.ds(k0, B_NK), sl]
            s = lax.dot_general(q, k, (((1,), (1,)), ((), ())), preferred_element_type=F32)
            s = s + bias_ref[h, case]
            m = jnp.max(s, axis=-1, keepdims=True)
            p = jnp.exp(s - m)
            l = jnp.sum(p, axis=-1, keepdims=True)
            acc = jnp.dot(p.astype(BF16), v, preferred_element_type=F32)
            o_ref[pl.ds(q0, B_NQ), sl] = (acc * (1.0 / l)).astype(o_ref.dtype)
        return c

    lax.fori_loop(0, n_grp, group, 0)


def _neighbourhood_bias_tiles(rpb):
    n_h = rpb.shape[0]
    c = np.arange(GRID_W)
    dc = np.clip(c[None, :] - c[:, None], -(WIN_C - 1), WIN_C - 1) + (WIN_C - 1)
    col_start = np.clip(c - WIN_C // 2, 0, GRID_W - WIN_C)
    col_ok = (c[None, :] >= col_start[:, None]) & (c[None, :] < col_start[:, None] + WIN_C)
    col_pick = np.zeros((2 * WIN_C - 1, GRID_W * GRID_W), np.float32)
    col_pick[dc.ravel(), np.arange(GRID_W * GRID_W)] = 1.0
    tiles = jnp.einsum("hab,bc->hac", rpb.astype(F32), col_pick, precision=lax.Precision.HIGHEST)
    tiles = jnp.where(col_ok.reshape(-1)[None, None], tiles, NEG_INF).reshape(n_h, -1, GRID_W, GRID_W)
    tiles = jnp.concatenate([tiles, jnp.full((n_h, 1, GRID_W, GRID_W), NEG_INF, F32)], axis=1)
    return jnp.concatenate([tiles, tiles], axis=-1)


def _attn_b(proj3, gq, gk, tiles):
    batch, seq, _ = proj3.shape
    rows = seq // GRID_W

    def qkv_spec(t):
        return pl.BlockSpec((None, seq, GROUP_W), lambda b: (b, 0, t * (QKV_W // GROUP_W) + B_BLOCK0))

    gain_spec = pl.BlockSpec((1, HEAD_DIM), lambda b: (0, 0))
    tile_spec = pl.BlockSpec(tiles.shape, lambda b: (0, 0, 0, 0), pipeline_mode=pl.Buffered(1))
    return pl.pallas_call(
        functools.partial(_attn_b_kernel, rows=rows),
        name="attn_b",
        grid=(batch,),
        in_specs=[qkv_spec(0), qkv_spec(1), qkv_spec(2), gain_spec, gain_spec, tile_spec],
        out_specs=pl.BlockSpec((None, seq, GROUP_W), lambda b: (b, 0, 0)),
        out_shape=jax.ShapeDtypeStruct((batch, seq, GROUP_W), BF16),
        scratch_shapes=[pltpu.VMEM((seq, GROUP_W), BF16), pltpu.VMEM((seq, GROUP_W), BF16),
                        pltpu.VMEM((N_HEADS_B, 3, B_NQ, B_NK), F32)],
        compiler_params=pltpu.CompilerParams(
            dimension_semantics=("arbitrary",), vmem_limit_bytes=VMEM_LIMIT),
    )(proj3, proj3, proj3, gq, gk, tiles)


def _mix_out_kernel(oa_ref, ob_ref, ga_ref, gb_ref, bg_ref, x_ref, pa_ref, pb_ref, wo_ref, h_ref):
    ya = jnp.dot(oa_ref[...], pa_ref[...], preferred_element_type=F32)
    yb = jnp.dot(ob_ref[...], pb_ref[...], preferred_element_type=F32)
    ga = jax.nn.sigmoid(ga_ref[...].astype(F32) + bg_ref[:, :D_MODEL])
    gb = jax.nn.sigmoid(gb_ref[...].astype(F32) + bg_ref[:, D_MODEL:])
    mixed = (ga * ya + gb * yb).astype(BF16)
    h_ref[...] = x_ref[...] + jnp.dot(mixed, wo_ref[...], preferred_element_type=F32)


def _mix_out(oa, ob, proj, b_gate, x2, pa, pb, wo, tm=512):
    n_tok = x2.shape[0]
    row512 = pl.BlockSpec((tm, GROUP_W), lambda i: (i, 0))
    const = pl.Buffered(1)
    return pl.pallas_call(
        _mix_out_kernel,
        name="mix_out",
        grid=(n_tok // tm,),
        in_specs=[row512, row512,
                  pl.BlockSpec((tm, D_MODEL), lambda i: (i, GATE_BLOCK0)),
                  pl.BlockSpec((tm, D_MODEL), lambda i: (i, GATE_BLOCK0 + 1)),
                  pl.BlockSpec((1, 2 * D_MODEL), lambda i: (0, 0)),
                  pl.BlockSpec((tm, D_MODEL), lambda i: (i, 0)),
                  pl.BlockSpec((GROUP_W, D_MODEL), lambda i: (0, 0), pipeline_mode=const),
                  pl.BlockSpec((GROUP_W, D_MODEL), lambda i: (0, 0), pipeline_mode=const),
                  pl.BlockSpec((D_MODEL, D_MODEL), lambda i: (0, 0), pipeline_mode=const)],
        out_specs=pl.BlockSpec((tm, D_MODEL), lambda i: (i, 0)),
        out_shape=jax.ShapeDtypeStruct((n_tok, D_MODEL), F32),
        compiler_params=pltpu.CompilerParams(
            dimension_semantics=("parallel",), vmem_limit_bytes=VMEM_LIMIT),
    )(oa, ob, proj, proj, b_gate, x2, pa, pb, wo)


def _ffn_kernel(h_ref, g_ref, wu_ref, wd_ref, o_ref, hn_ref):
    @pl.when(pl.program_id(1) == 0)
    def _():
        h = h_ref[...]
        hn_ref[...] = _rms(h, g_ref[...]).astype(BF16)
        o_ref[...] = h

    u = jnp.maximum(jnp.dot(hn_ref[...], wu_ref[...], preferred_element_type=F32), 0.0)
    o_ref[...] += jnp.dot((u * u).astype(BF16), wd_ref[...], preferred_element_type=F32)


def _ffn(h, gain, wu, wd, tm=512, tf=1024):
    n_tok = h.shape[0]
    return pl.pallas_call(
        _ffn_kernel,
        name="ffn",
        grid=(n_tok // tm, D_FF // tf),
        in_specs=[pl.BlockSpec((tm, D_MODEL), lambda i, f: (i, 0)),
                  pl.BlockSpec((1, D_MODEL), lambda i, f: (0, 0)),
                  pl.BlockSpec((D_MODEL, tf), lambda i, f: (0, f)),
                  pl.BlockSpec((tf, D_MODEL), lambda i, f: (f, 0))],
        out_specs=pl.BlockSpec((tm, D_MODEL), lambda i, f: (i, 0)),
        out_shape=jax.ShapeDtypeStruct((n_tok, D_MODEL), F32),
        scratch_shapes=[pltpu.VMEM((tm, D_MODEL), BF16)],
        compiler_params=pltpu.CompilerParams(
            dimension_semantics=("parallel", "arbitrary"), vmem_limit_bytes=VMEM_LIMIT),
    )(h, gain, wu, wd)


def kernel(x, norm_mix, w_in, b_gate, q_norm_a, k_norm_a, q_norm_b, k_norm_b, rpb_b,
           w_proj_a, w_proj_b, w_out, norm_ffn, w_up, w_down):
    batch, seq, d_model = x.shape
    assert d_model == D_MODEL and seq % (DILATED_PAIRS[-1][1] * Q_BLK) == 0
    for win, dil in DILATED_PAIRS:
        assert win // (2 * dil) == HALF_WIN
    depth = norm_mix.shape[0]
    cos, sin = _rope_tables(seq)
    h = x.reshape(batch * seq, D_MODEL)
    for l in range(depth):
        proj = _in_proj(h, norm_mix[l].reshape(1, D_MODEL), w_in[l].astype(BF16))
        proj3 = proj.reshape(batch, seq, D_PROJ)
        oa = _attn_a(proj3, cos, sin, q_norm_a[l].reshape(1, HEAD_DIM), k_norm_a[l].reshape(1, HEAD_DIM))
        ob = _attn_b(proj3, q_norm_b[l].reshape(1, HEAD_DIM), k_norm_b[l].reshape(1, HEAD_DIM),
                     _neighbourhood_bias_tiles(rpb_b[l]))
        h = _mix_out(oa.reshape(batch * seq, GROUP_W), ob.reshape(batch * seq, GROUP_W), proj,
                     b_gate[l].reshape(1, 2 * D_MODEL), h,
                     w_proj_a[l].astype(BF16), w_proj_b[l].astype(BF16), w_out[l].astype(BF16))
        h = _ffn(h, norm_ffn[l].reshape(1, D_MODEL), w_up[l].astype(BF16), w_down[l].astype(BF16))
    return h.reshape(batch, seq, D_MODEL)
```

```python
import functools

import numpy as np
import jax
import jax.numpy as jnp
from jax import lax
from jax.experimental import pallas as pl
from jax.experimental.pallas import tpu as pltpu

D_MODEL = 2048
HEAD_DIM = 128
N_HEADS = 16
N_HEADS_A = 12
N_HEADS_B = 4
DILATED_PAIRS = ((128, 1), (512, 4), (2048, 16))
N_GROUPS_A = len(DILATED_PAIRS)
HEADS_PER_GROUP = 4
GROUP_W = HEADS_PER_GROUP * HEAD_DIM
GRID_W = 64
WIN_R = 8
WIN_C = 16
QKV_W = N_HEADS * HEAD_DIM
D_FF = 4 * D_MODEL
ROPE_THETA = 10000.0
EPS = 1e-6
NEG_INF = -1e30
SCALE = HEAD_DIM ** -0.5

D_PROJ = 3 * QKV_W + 2 * D_MODEL
B_BLOCK0 = N_HEADS_A * HEAD_DIM // GROUP_W
GATE_BLOCK0 = 3 * QKV_W // D_MODEL

VMEM_LIMIT = 56 * 1024 * 1024

F32 = jnp.float32
BF16 = jnp.bfloat16


def _rope_tables(seq):
    pos = np.arange(seq, dtype=np.float64)
    inv = ROPE_THETA ** (-np.arange(0, HEAD_DIM, 2, dtype=np.float64) / HEAD_DIM)
    ang = pos[:, None] * inv[None, :]
    cos = np.concatenate([np.cos(ang), np.cos(ang)], axis=-1)
    sin = np.concatenate([-np.sin(ang), np.sin(ang)], axis=-1)
    return jnp.asarray(cos, F32), jnp.asarray(sin, F32)


def _rms(x, g):
    ms = jnp.mean(x * x, axis=-1, keepdims=True)
    return (x * lax.rsqrt(ms + EPS)) * g


def _in_proj_kernel(x_ref, g_ref, w_ref, rc_ref, rs_ref, gb_ref, o_ref, xn_ref):
    j = pl.program_id(1)

    @pl.when(j == 0)
    def _():
        xn_ref[...] = _rms(x_ref[...], g_ref[...]).astype(BF16)

    def project():
        return jnp.dot(xn_ref[...], w_ref[...], preferred_element_type=F32)

    @pl.when(j < 2)
    def _():
        acc = project()
        for h in range(N_HEADS):
            sl = slice(h * HEAD_DIM, (h + 1) * HEAD_DIM)
            y = acc[:, sl]
            inv_rms = lax.rsqrt(jnp.mean(y * y, axis=-1, keepdims=True) + EPS)
            if h < N_HEADS_A:
                y = y * rc_ref[...] + pltpu.roll(y, HEAD_DIM // 2, 1) * rs_ref[...]
            else:
                y = y * gb_ref[...]
            o_ref[:, sl] = (y * inv_rms).astype(o_ref.dtype)

    @pl.when(j >= 2)
    def _():
        o_ref[...] = project().astype(o_ref.dtype)


def _in_proj(x2, gain, w, rope_c, rope_s, gain_b, seq, tm=1024):
    n_tok = x2.shape[0]
    tn = QKV_W
    assert seq % tm == 0 and D_PROJ % tn == 0
    tiles_per_seq = seq // tm

    def qk(j):
        return jnp.minimum(j, 1)

    return pl.pallas_call(
        _in_proj_kernel,
        name="in_proj",
        grid=(n_tok // tm, D_PROJ // tn),
        in_specs=[
            pl.BlockSpec((tm, D_MODEL), lambda i, j: (i, 0)),
            pl.BlockSpec((1, D_MODEL), lambda i, j: (0, 0)),
            pl.BlockSpec((D_MODEL, tn), lambda i, j: (0, j)),
            pl.BlockSpec((None, tm, HEAD_DIM), lambda i, j: (qk(j), i % tiles_per_seq, 0)),
            pl.BlockSpec((None, tm, HEAD_DIM), lambda i, j: (qk(j), i % tiles_per_seq, 0)),
            pl.BlockSpec((None, 1, HEAD_DIM), lambda i, j: (qk(j), 0, 0)),
        ],
        out_specs=pl.BlockSpec((tm, tn), lambda i, j: (i, j)),
        out_shape=jax.ShapeDtypeStruct((n_tok, D_PROJ), BF16),
        scratch_shapes=[pltpu.VMEM((tm, D_MODEL), BF16)],
        compiler_params=pltpu.CompilerParams(
            dimension_semantics=("parallel", "arbitrary"), vmem_limit_bytes=VMEM_LIMIT),
    )(x2, gain, w, rope_c, rope_s, gain_b)


Q_BLK = 128
HALF_WIN = 64
A_WIN = 2 * Q_BLK


def _band_masks():
    rel = np.arange(A_WIN)[None, :] - np.arange(Q_BLK)[:, None]
    tiles = [np.where(np.abs(rel - off) <= HALF_WIN, 0.0, NEG_INF) for off in (0, HALF_WIN, 2 * HALF_WIN)]
    return jnp.asarray(np.stack(tiles), F32)


def _attn_a_kernel(*refs, seq, dils):
    n_g = len(dils)
    qkv_refs = refs[:3 * n_g]
    mask_ref, o_ref, qn_ref, kn_ref, vn_ref, f_ref, og_ref, lg_ref = refs[3 * n_g:]
    n_blk = seq // Q_BLK
    vn_ref[:, :, HEAD_DIM:] = jnp.ones((n_g, seq, HEAD_DIM), BF16)

    stage = {}
    for g, dil in enumerate(dils):
        if dil > 1:
            stage[g] = tuple(f_ref.at[3 * len(stage) + t] for t in range(3))

    def block_rows(n):
        return pl.ds(pl.multiple_of(n * Q_BLK, Q_BLK), Q_BLK)

    def token_rows(n, dil):
        if dil == 1:
            return block_rows(n)
        sub_len = seq // dil
        base = n * Q_BLK
        r = base // sub_len
        return pl.ds(r + (base - r * sub_len) * dil, Q_BLK, stride=dil)

    def stage_in(n, c):
        rows = block_rows(n)
        for g, dil in enumerate(dils):
            q_ref, k_ref, v_ref = qkv_refs[3 * g:3 * g + 3]
            if dil == 1:
                qn_ref[g, rows, :] = q_ref[rows, :]
                kn_ref[g, rows, :] = k_ref[rows, :]
                vn_ref[g, rows, :HEAD_DIM] = v_ref[rows, :]
            else:
                fq, fk, fv = stage[g]
                fq[rows, :] = q_ref[rows, :].astype(F32)
                fk[rows, :] = k_ref[rows, :].astype(F32)
                fv[rows, :] = v_ref[rows, :].astype(F32)
        return c

    lax.fori_loop(0, n_blk, stage_in, 0, unroll=2)

    def gather(n, c):
        dst = block_rows(n)
        for g, dil in enumerate(dils):
            if dil > 1:
                fq, fk, fv = stage[g]
                src = token_rows(n, dil)
                qn_ref[g, dst, :] = fq[src, :].astype(BF16)
                kn_ref[g, dst, :] = fk[src, :].astype(BF16)
                vn_ref[g, dst, :HEAD_DIM] = fv[src, :].astype(BF16)
        return c

    lax.fori_loop(0, n_blk, gather, 0, unroll=2)

    def scatter_later(dil):
        return dil % 8 == 0

    def block(n, c):
        q0 = pl.multiple_of(n * Q_BLK, Q_BLK)
        for g, dil in enumerate(dils):
            sub_len = seq // dil
            win = min(A_WIN, sub_len)
            sub0 = (n // (sub_len // Q_BLK)) * sub_len
            q_loc = q0 - sub0
            k_loc = jnp.clip(q_loc - HALF_WIN, 0, sub_len - win)
            k0 = pl.multiple_of(sub0 + k_loc, HALF_WIN)
            band = mask_ref[(q_loc - k_loc) // HALF_WIN, :, :win]
            q = qn_ref[g, pl.ds(q0, Q_BLK), :]
            k = kn_ref[g, pl.ds(k0, win), :]
            s = lax.dot_general(q, k, (((1,), (1,)), ((), ())), preferred_element_type=F32) + band
            m = jnp.max(s, axis=-1, keepdims=True)
            p = jnp.exp(s - m).astype(BF16)
            acc = jnp.dot(p, vn_ref[g, pl.ds(k0, win), :], preferred_element_type=F32)
            denom = acc[:, HEAD_DIM:]
            out = acc[:, :HEAD_DIM] * (1.0 / denom)
            lse = m + jnp.log(denom)
            if scatter_later(dil):
                stage[g][0][pl.ds(q0, Q_BLK), :] = out
                stage[g][1][pl.ds(q0, Q_BLK), :] = lse
            else:
                dst = token_rows(n, dil)
                og_ref.at[g][dst, :] = out
                lg_ref.at[g][dst, :] = lse
        return c

    lax.fori_loop(0, n_blk, block, 0, unroll=2)

    def scatter(n, c):
        src = block_rows(n)
        for g, dil in enumerate(dils):
            if scatter_later(dil):
                dst = token_rows(n, dil)
                og_ref.at[g][dst, :] = stage[g][0][src, :]
                lg_ref.at[g][dst, :] = stage[g][1][src, :]
        return c

    if any(scatter_later(dil) for dil in dils):
        lax.fori_loop(0, n_blk, scatter, 0, unroll=2)

    def combine(n, c):
        rows = block_rows(n)
        lses = [lg_ref[g, rows, :] for g in range(n_g)]
        mx = functools.reduce(jnp.maximum, lses)
        es = [jnp.exp(x - mx) for x in lses]
        inv = 1.0 / functools.reduce(lambda a, b: a + b, es)
        oa = (es[0] * inv) * og_ref[0, rows, :]
        for g in range(1, n_g):
            oa = oa + (es[g] * inv) * og_ref[g, rows, :]
        o_ref[rows, :] = oa.astype(o_ref.dtype)
        return c

    lax.fori_loop(0, n_blk, combine, 0, unroll=2)


def _attn_a(proj3):
    batch, seq, _ = proj3.shape
    dils = tuple(d for _, d in DILATED_PAIRS)
    n_g = len(dils)
    n_dilated = sum(d > 1 for d in dils)
    masks = _band_masks()
    qkv_specs = [pl.BlockSpec((None, seq, HEAD_DIM),
                              lambda b, s, t=t, g=g: (b, 0, t * N_HEADS + g * HEADS_PER_GROUP + s))
                 for g in range(n_g) for t in range(3)]
    return pl.pallas_call(
        functools.partial(_attn_a_kernel, seq=seq, dils=dils),
        name="attn_a",
        grid=(batch, HEADS_PER_GROUP),
        in_specs=qkv_specs + [pl.BlockSpec(masks.shape, lambda b, s: (0, 0, 0),
                                           pipeline_mode=pl.Buffered(1))],
        out_specs=pl.BlockSpec((None, seq, HEAD_DIM), lambda b, s: (b, 0, s)),
        out_shape=jax.ShapeDtypeStruct((batch, seq, GROUP_W), BF16),
        scratch_shapes=[pltpu.VMEM((n_g, seq, HEAD_DIM), BF16)] * 2
        + [pltpu.VMEM((n_g, seq, 2 * HEAD_DIM), BF16)]
        + [pltpu.VMEM((3 * n_dilated, seq, HEAD_DIM), F32)]
        + [pltpu.VMEM((n_g, seq, HEAD_DIM), F32)] * 2,
        compiler_params=pltpu.CompilerParams(
            dimension_semantics=("parallel", "arbitrary"), vmem_limit_bytes=VMEM_LIMIT),
    )(*([proj3] * (3 * n_g)), masks)


B_QROWS = 4
B_KROWS = B_QROWS + WIN_R
B_NQ = B_QROWS * GRID_W
B_NK = B_KROWS * GRID_W


def _window_row_offsets(rows):
    n_grp = rows // B_QROWS
    assert rows % B_QROWS == 0 and n_grp >= 3
    masked = 2 * WIN_R - 1
    table = []
    for i in (0, 1, n_grp - 1):
        r0 = i * B_QROWS
        ws = int(np.clip(r0 - WIN_R // 2, 0, rows - B_KROWS))
        per_q = []
        for rq in range(B_QROWS):
            r = r0 + rq
            rs = int(np.clip(r - WIN_R // 2, 0, rows - WIN_R))
            per_q.append([ws + jr - r + WIN_R - 1 if rs <= ws + jr < rs + WIN_R else masked
                          for jr in range(B_KROWS)])
        table.append(per_q)
    return table


def _attn_b_kernel(q_ref, k_ref, v_ref, tile_ref, o_ref, bias_ref, *, rows):
    n_grp = rows // B_QROWS

    @pl.when(pl.program_id(0) == 0)
    def _():
        right = lax.broadcasted_iota(jnp.int32, (GRID_W, 2 * GRID_W), 1) >= GRID_W
        for case, per_q in enumerate(_window_row_offsets(rows)):
            for rq, offs in enumerate(per_q):
                for h in range(N_HEADS_B):
                    for j in range(0, B_KROWS, 2):
                        pair = jnp.where(right, tile_ref[h, offs[j + 1]], tile_ref[h, offs[j]])
                        bias_ref[h, case, rq * GRID_W:(rq + 1) * GRID_W, j * GRID_W:(j + 2) * GRID_W] = pair

    def group(i, c):
        r0 = i * B_QROWS
        ws = jnp.clip(r0 - WIN_R // 2, 0, rows - B_KROWS)
        case = jnp.where(i == 0, 0, jnp.where(i == n_grp - 1, 2, 1))
        q0 = pl.multiple_of(r0 * GRID_W, B_NQ)
        k0 = pl.multiple_of(ws * GRID_W, GRID_W)
        for h in range(N_HEADS_B):
            sl = slice(h * HEAD_DIM, (h + 1) * HEAD_DIM)
            q = q_ref[pl.ds(q0, B_NQ), sl]
            k = k_ref[pl.ds(k0, B_NK), sl]
            v = v_ref[pl.ds(k0, B_NK), sl]
            s = lax.dot_general(q, k, (((1,), (1,)), ((), ())), preferred_element_type=F32)
            s = s + bias_ref[h, case]
            m = jnp.max(s, axis=-1, keepdims=True)
            p = jnp.exp(s - m)
            l = jnp.sum(p, axis=-1, keepdims=True)
            acc = jnp.dot(p.astype(BF16), v, preferred_element_type=F32)
            o_ref[pl.ds(q0, B_NQ), sl] = (acc * (1.0 / l)).astype(o_ref.dtype)
        return c

    lax.fori_loop(0, n_grp, group, 0)


def _neighbourhood_bias_tiles(rpb):
    n_h = rpb.shape[0]
    c = np.arange(GRID_W)
    dc = np.clip(c[None, :] - c[:, None], -(WIN_C - 1), WIN_C - 1) + (WIN_C - 1)
    col_start = np.clip(c - WIN_C // 2, 0, GRID_W - WIN_C)
    col_ok = (c[None, :] >= col_start[:, None]) & (c[None, :] < col_start[:, None] + WIN_C)
    col_pick = np.zeros((2 * WIN_C - 1, GRID_W * GRID_W), np.float32)
    col_pick[dc.ravel(), np.arange(GRID_W * GRID_W)] = 1.0
    tiles = jnp.einsum("hab,bc->hac", rpb.astype(F32), col_pick, precision=lax.Precision.HIGHEST)
    tiles = jnp.where(col_ok.reshape(-1)[None, None], tiles, NEG_INF).reshape(n_h, -1, GRID_W, GRID_W)
    tiles = jnp.concatenate([tiles, jnp.full((n_h, 1, GRID_W, GRID_W), NEG_INF, F32)], axis=1)
    return jnp.concatenate([tiles, tiles], axis=-1)


def _attn_b(proj3, tiles):
    batch, seq, _ = proj3.shape
    rows = seq // GRID_W

    def qkv_spec(t):
        return pl.BlockSpec((None, seq, GROUP_W), lambda b: (b, 0, t * (QKV_W // GROUP_W) + B_BLOCK0))

    tile_spec = pl.BlockSpec(tiles.shape, lambda b: (0, 0, 0, 0), pipeline_mode=pl.Buffered(1))
    return pl.pallas_call(
        functools.partial(_attn_b_kernel, rows=rows),
        name="attn_b",
        grid=(batch,),
        in_specs=[qkv_spec(0), qkv_spec(1), qkv_spec(2), tile_spec],
        out_specs=pl.BlockSpec((None, seq, GROUP_W), lambda b: (b, 0, 0)),
        out_shape=jax.ShapeDtypeStruct((batch, seq, GROUP_W), BF16),
        scratch_shapes=[pltpu.VMEM((N_HEADS_B, 3, B_NQ, B_NK), F32)],
        compiler_params=pltpu.CompilerParams(
            dimension_semantics=("arbitrary",), vmem_limit_bytes=VMEM_LIMIT),
    )(proj3, proj3, proj3, tiles)


def _mix_out_kernel(oa_ref, ob_ref, ga_ref, gb_ref, bg_ref, x_ref, pa_ref, pb_ref, wo_ref, h_ref):
    ya = jnp.dot(oa_ref[...], pa_ref[...], preferred_element_type=F32)
    yb = jnp.dot(ob_ref[...], pb_ref[...], preferred_element_type=F32)
    ga = jax.nn.sigmoid(ga_ref[...].astype(F32) + bg_ref[:, :D_MODEL])
    gb = jax.nn.sigmoid(gb_ref[...].astype(F32) + bg_ref[:, D_MODEL:])
    mixed = (ga * ya + gb * yb).astype(BF16)
    h_ref[...] = x_ref[...] + jnp.dot(mixed, wo_ref[...], preferred_element_type=F32)


def _mix_out(oa, ob, proj, b_gate, x2, pa, pb, wo, tm=512):
    n_tok = x2.shape[0]
    row512 = pl.BlockSpec((tm, GROUP_W), lambda i: (i, 0))
    const = pl.Buffered(1)
    return pl.pallas_call(
        _mix_out_kernel,
        name="mix_out",
        grid=(n_tok // tm,),
        in_specs=[row512, row512,
                  pl.BlockSpec((tm, D_MODEL), lambda i: (i, GATE_BLOCK0)),
                  pl.BlockSpec((tm, D_MODEL), lambda i: (i, GATE_BLOCK0 + 1)),
                  pl.BlockSpec((1, 2 * D_MODEL), lambda i: (0, 0)),
                  pl.BlockSpec((tm, D_MODEL), lambda i: (i, 0)),
                  pl.BlockSpec((GROUP_W, D_MODEL), lambda i: (0, 0), pipeline_mode=const),
                  pl.BlockSpec((GROUP_W, D_MODEL), lambda i: (0, 0), pipeline_mode=const),
                  pl.BlockSpec((D_MODEL, D_MODEL), lambda i: (0, 0), pipeline_mode=const)],
        out_specs=pl.BlockSpec((tm, D_MODEL), lambda i: (i, 0)),
        out_shape=jax.ShapeDtypeStruct((n_tok, D_MODEL), F32),
        compiler_params=pltpu.CompilerParams(
            dimension_semantics=("parallel",), vmem_limit_bytes=VMEM_LIMIT),
    )(oa, ob, proj, proj, b_gate, x2, pa, pb, wo)


def _ffn_kernel(h_ref, g_ref, wu_ref, wd_ref, o_ref, hn_ref):
    @pl.when(pl.program_id(1) == 0)
    def _():
        h = h_ref[...]
        hn_ref[...] = _rms(h, g_ref[...]).astype(BF16)
        o_ref[...] = h

    u = jnp.maximum(jnp.dot(hn_ref[...], wu_ref[...], preferred_element_type=F32), 0.0)
    o_ref[...] += jnp.dot((u * u).astype(BF16), wd_ref[...], preferred_element_type=F32)


def _ffn(h, gain, wu, wd, tm=1024, tf=512):
    n_tok = h.shape[0]
    return pl.pallas_call(
        _ffn_kernel,
        name="ffn",
        grid=(n_tok // tm, D_FF // tf),
        in_specs=[pl.BlockSpec((tm, D_MODEL), lambda i, f: (i, 0)),
                  pl.BlockSpec((1, D_MODEL), lambda i, f: (0, 0)),
                  pl.BlockSpec((D_MODEL, tf), lambda i, f: (0, f)),
                  pl.BlockSpec((tf, D_MODEL), lambda i, f: (f, 0))],
        out_specs=pl.BlockSpec((tm, D_MODEL), lambda i, f: (i, 0)),
        out_shape=jax.ShapeDtypeStruct((n_tok, D_MODEL), F32),
        scratch_shapes=[pltpu.VMEM((tm, D_MODEL), BF16)],
        compiler_params=pltpu.CompilerParams(
            dimension_semantics=("parallel", "arbitrary"), vmem_limit_bytes=VMEM_LIMIT),
    )(h, gain, wu, wd)


def kernel(x, norm_mix, w_in, b_gate, q_norm_a, k_norm_a, q_norm_b, k_norm_b, rpb_b,
           w_proj_a, w_proj_b, w_out, norm_ffn, w_up, w_down):
    batch, seq, d_model = x.shape
    assert d_model == D_MODEL and seq % (DILATED_PAIRS[-1][1] * Q_BLK) == 0
    for win, dil in DILATED_PAIRS:
        assert win // (2 * dil) == HALF_WIN
    depth = norm_mix.shape[0]
    cos, sin = _rope_tables(seq)
    h = x.reshape(batch * seq, D_MODEL)
    for l in range(depth):
        gains_a = jnp.stack([q_norm_a[l] * SCALE, k_norm_a[l]]).astype(F32)[:, None, :]
        rope_c = cos[None] * gains_a
        rope_s = sin[None] * jnp.roll(gains_a, HEAD_DIM // 2, axis=-1)
        gains_b = jnp.stack([q_norm_b[l] * SCALE, k_norm_b[l]]).astype(F32)[:, None, :]
        proj = _in_proj(h, norm_mix[l].reshape(1, D_MODEL), w_in[l].astype(BF16), rope_c, rope_s, gains_b, seq)
        proj3 = proj.reshape(batch, seq, D_PROJ)
        oa = _attn_a(proj3)
        ob = _attn_b(proj3, _neighbourhood_bias_tiles(rpb_b[l]))
        h = _mix_out(oa.reshape(batch * seq, GROUP_W), ob.reshape(batch * seq, GROUP_W), proj,
                     b_gate[l].reshape(1, 2 * D_MODEL), h,
                     w_proj_a[l].astype(BF16), w_proj_b[l].astype(BF16), w_out[l].astype(BF16))
        h = _ffn(h, norm_ffn[l].reshape(1, D_MODEL), w_up[l].astype(BF16), w_down[l].astype(BF16))
    return h.reshape(batch, seq, D_MODEL)
```

```python
import functools

import numpy as np
import jax
import jax.numpy as jnp
from jax import lax
from jax.experimental import pallas as pl
from jax.experimental.pallas import tpu as pltpu

D_MODEL = 2048
HEAD_DIM = 128
N_HEADS = 16
N_HEADS_A = 12
N_HEADS_B = 4
DILATED_PAIRS = ((128, 1), (512, 4), (2048, 16))
N_GROUPS_A = len(DILATED_PAIRS)
HEADS_PER_GROUP = 4
GROUP_W = HEADS_PER_GROUP * HEAD_DIM
GRID_W = 64
WIN_R = 8
WIN_C = 16
QKV_W = N_HEADS * HEAD_DIM
D_FF = 4 * D_MODEL
ROPE_THETA = 10000.0
EPS = 1e-6
NEG_INF = -1e30
SCALE = HEAD_DIM ** -0.5

D_PROJ = 3 * QKV_W + 2 * D_MODEL
B_BLOCK0 = N_HEADS_A * HEAD_DIM // GROUP_W
GATE_BLOCK0 = 3 * QKV_W // D_MODEL

VMEM_LIMIT = 56 * 1024 * 1024

F32 = jnp.float32
BF16 = jnp.bfloat16


def _rope_tables(seq):
    pos = np.arange(seq, dtype=np.float64)
    inv = ROPE_THETA ** (-np.arange(0, HEAD_DIM, 2, dtype=np.float64) / HEAD_DIM)
    ang = pos[:, None] * inv[None, :]
    cos = np.concatenate([np.cos(ang), np.cos(ang)], axis=-1)
    sin = np.concatenate([-np.sin(ang), np.sin(ang)], axis=-1)
    return jnp.asarray(cos, F32), jnp.asarray(sin, F32)


def _rms(x, g):
    ms = jnp.mean(x * x, axis=-1, keepdims=True)
    return (x * lax.rsqrt(ms + EPS)) * g


def _in_proj_kernel(x_ref, g_ref, w_ref, rc_ref, rs_ref, gb_ref, o_ref, xn_ref):
    j = pl.program_id(1)

    @pl.when(j == 0)
    def _():
        xn_ref[...] = _rms(x_ref[...], g_ref[...]).astype(BF16)

    def project():
        return jnp.dot(xn_ref[...], w_ref[...], preferred_element_type=F32)

    @pl.when(j < 2)
    def _():
        acc = project()
        for h in range(N_HEADS):
            sl = slice(h * HEAD_DIM, (h + 1) * HEAD_DIM)
            y = acc[:, sl]
            inv_rms = lax.rsqrt(jnp.mean(y * y, axis=-1, keepdims=True) + EPS)
            if h < N_HEADS_A:
                y = y * rc_ref[...] + pltpu.roll(y, HEAD_DIM // 2, 1) * rs_ref[...]
            else:
                y = y * gb_ref[...]
            o_ref[:, sl] = (y * inv_rms).astype(o_ref.dtype)

    @pl.when(j >= 2)
    def _():
        o_ref[...] = project().astype(o_ref.dtype)


def _in_proj(x2, gain, w, rope_c, rope_s, gain_b, seq, tm=1024):
    n_tok = x2.shape[0]
    tn = QKV_W
    assert seq % tm == 0 and D_PROJ % tn == 0
    tiles_per_seq = seq // tm

    def qk(j):
        return jnp.minimum(j, 1)

    return pl.pallas_call(
        _in_proj_kernel,
        name="in_proj",
        grid=(n_tok // tm, D_PROJ // tn),
        in_specs=[
            pl.BlockSpec((tm, D_MODEL), lambda i, j: (i, 0)),
            pl.BlockSpec((1, D_MODEL), lambda i, j: (0, 0)),
            pl.BlockSpec((D_MODEL, tn), lambda i, j: (0, j)),
            pl.BlockSpec((None, tm, HEAD_DIM), lambda i, j: (qk(j), i % tiles_per_seq, 0)),
            pl.BlockSpec((None, tm, HEAD_DIM), lambda i, j: (qk(j), i % tiles_per_seq, 0)),
            pl.BlockSpec((None, 1, HEAD_DIM), lambda i, j: (qk(j), 0, 0)),
        ],
        out_specs=pl.BlockSpec((tm, tn), lambda i, j: (i, j)),
        out_shape=jax.ShapeDtypeStruct((n_tok, D_PROJ), BF16),
        scratch_shapes=[pltpu.VMEM((tm, D_MODEL), BF16)],
        compiler_params=pltpu.CompilerParams(
            dimension_semantics=("parallel", "arbitrary"), vmem_limit_bytes=VMEM_LIMIT),
    )(x2, gain, w, rope_c, rope_s, gain_b)


Q_BLK = 128
HALF_WIN = 64
A_WIN = 2 * Q_BLK


def _band_masks():
    rel = np.arange(A_WIN)[None, :] - np.arange(Q_BLK)[:, None]
    tiles = [np.where(np.abs(rel - off) <= HALF_WIN, 0.0, NEG_INF) for off in (0, HALF_WIN, 2 * HALF_WIN)]
    return jnp.asarray(np.stack(tiles), F32)


def _attn_a_kernel(*refs, seq, dils):
    n_g = len(dils)
    qkv_refs = refs[:3 * n_g]
    mask_ref, o_ref, qn_ref, kn_ref, vn_ref, f_ref, og_ref, lg_ref = refs[3 * n_g:]
    n_blk = seq // Q_BLK
    vn_ref[:, :, HEAD_DIM:] = jnp.ones((n_g, seq, HEAD_DIM), BF16)

    stage = {}
    for g, dil in enumerate(dils):
        if dil > 1:
            stage[g] = tuple(f_ref.at[3 * len(stage) + t] for t in range(3))

    def block_rows(n):
        return pl.ds(pl.multiple_of(n * Q_BLK, Q_BLK), Q_BLK)

    def token_rows(n, dil):
        if dil == 1:
            return block_rows(n)
        sub_len = seq // dil
        base = n * Q_BLK
        r = base // sub_len
        return pl.ds(r + (base - r * sub_len) * dil, Q_BLK, stride=dil)

    def stage_in(n, c):
        rows = block_rows(n)
        for g, dil in enumerate(dils):
            q_ref, k_ref, v_ref = qkv_refs[3 * g:3 * g + 3]
            if dil == 1:
                qn_ref[g, rows, :] = q_ref[rows, :]
                kn_ref[g, rows, :] = k_ref[rows, :]
                vn_ref[g, rows, :HEAD_DIM] = v_ref[rows, :]
            else:
                fq, fk, fv = stage[g]
                fq[rows, :] = q_ref[rows, :].astype(F32)
                fk[rows, :] = k_ref[rows, :].astype(F32)
                fv[rows, :] = v_ref[rows, :].astype(F32)
        return c

    lax.fori_loop(0, n_blk, stage_in, 0, unroll=2)

    def gather(n, c):
        dst = block_rows(n)
        for g, dil in enumerate(dils):
            if dil > 1:
                fq, fk, fv = stage[g]
                src = token_rows(n, dil)
                qn_ref[g, dst, :] = fq[src, :].astype(BF16)
                kn_ref[g, dst, :] = fk[src, :].astype(BF16)
                vn_ref[g, dst, :HEAD_DIM] = fv[src, :].astype(BF16)
        return c

    lax.fori_loop(0, n_blk, gather, 0, unroll=2)

    def scatter_later(dil):
        return dil % 8 == 0

    def block(n, c):
        q0 = pl.multiple_of(n * Q_BLK, Q_BLK)
        for g, dil in enumerate(dils):
            sub_len = seq // dil
            win = min(A_WIN, sub_len)
            sub0 = (n // (sub_len // Q_BLK)) * sub_len
            q_loc = q0 - sub0
            k_loc = jnp.clip(q_loc - HALF_WIN, 0, sub_len - win)
            k0 = pl.multiple_of(sub0 + k_loc, HALF_WIN)
            band = mask_ref[(q_loc - k_loc) // HALF_WIN, :, :win]
            q = qn_ref[g, pl.ds(q0, Q_BLK), :]
            k = kn_ref[g, pl.ds(k0, win), :]
            s = lax.dot_general(q, k, (((1,), (1,)), ((), ())), preferred_element_type=F32) + band
            m = jnp.max(s, axis=-1, keepdims=True)
            p = jnp.exp(s - m).astype(BF16)
            acc = jnp.dot(p, vn_ref[g, pl.ds(k0, win), :], preferred_element_type=F32)
            denom = acc[:, HEAD_DIM:]
            out = acc[:, :HEAD_DIM] * (1.0 / denom)
            lse = m + jnp.log(denom)
            if scatter_later(dil):
                stage[g][0][pl.ds(q0, Q_BLK), :] = out
                stage[g][1][pl.ds(q0, Q_BLK), :] = lse
            else:
                dst = token_rows(n, dil)
                og_ref.at[g][dst, :] = out
                lg_ref.at[g][dst, :] = lse
        return c

    lax.fori_loop(0, n_blk, block, 0, unroll=2)

    def scatter(n, c):
        src = block_rows(n)
        for g, dil in enumerate(dils):
            if scatter_later(dil):
                dst = token_rows(n, dil)
                og_ref.at[g][dst, :] = stage[g][0][src, :]
                lg_ref.at[g][dst, :] = stage[g][1][src, :]
        return c

    if any(scatter_later(dil) for dil in dils):
        lax.fori_loop(0, n_blk, scatter, 0, unroll=2)

    def combine(n, c):
        rows = block_rows(n)
        lses = [lg_ref[g, rows, :] for g in range(n_g)]
        mx = functools.reduce(jnp.maximum, lses)
        es = [jnp.exp(x - mx) for x in lses]
        inv = 1.0 / functools.reduce(lambda a, b: a + b, es)
        oa = (es[0] * inv) * og_ref[0, rows, :]
        for g in range(1, n_g):
            oa = oa + (es[g] * inv) * og_ref[g, rows, :]
        o_ref[rows, :] = oa.astype(o_ref.dtype)
        return c

    lax.fori_loop(0, n_blk, combine, 0, unroll=2)


def _attn_a(proj3):
    batch, seq, _ = proj3.shape
    dils = tuple(d for _, d in DILATED_PAIRS)
    n_g = len(dils)
    n_dilated = sum(d > 1 for d in dils)
    masks = _band_masks()
    qkv_specs = [pl.BlockSpec((None, seq, HEAD_DIM),
                              lambda b, s, t=t, g=g: (b, 0, t * N_HEADS + g * HEADS_PER_GROUP + s))
                 for g in range(n_g) for t in range(3)]
    return pl.pallas_call(
        functools.partial(_attn_a_kernel, seq=seq, dils=dils),
        name="attn_a",
        grid=(batch, HEADS_PER_GROUP),
        in_specs=qkv_specs + [pl.BlockSpec(masks.shape, lambda b, s: (0, 0, 0),
                                           pipeline_mode=pl.Buffered(1))],
        out_specs=pl.BlockSpec((None, seq, HEAD_DIM), lambda b, s: (b, 0, s)),
        out_shape=jax.ShapeDtypeStruct((batch, seq, GROUP_W), BF16),
        scratch_shapes=[pltpu.VMEM((n_g, seq, HEAD_DIM), BF16)] * 2
        + [pltpu.VMEM((n_g, seq, 2 * HEAD_DIM), BF16)]
        + [pltpu.VMEM((3 * n_dilated, seq, HEAD_DIM), F32)]
        + [pltpu.VMEM((n_g, seq, HEAD_DIM), F32)] * 2,
        compiler_params=pltpu.CompilerParams(
            dimension_semantics=("parallel", "arbitrary"), vmem_limit_bytes=VMEM_LIMIT),
    )(*([proj3] * (3 * n_g)), masks)


B_QROWS = 4
B_KROWS = B_QROWS + WIN_R
B_NQ = B_QROWS * GRID_W
B_NK = B_KROWS * GRID_W


def _window_row_offsets(rows):
    n_grp = rows // B_QROWS
    assert rows % B_QROWS == 0 and n_grp >= 3
    masked = 2 * WIN_R - 1
    table = []
    for i in (0, 1, n_grp - 1):
        r0 = i * B_QROWS
        ws = int(np.clip(r0 - WIN_R // 2, 0, rows - B_KROWS))
        per_q = []
        for rq in range(B_QROWS):
            r = r0 + rq
            rs = int(np.clip(r - WIN_R // 2, 0, rows - WIN_R))
            per_q.append([ws + jr - r + WIN_R - 1 if rs <= ws + jr < rs + WIN_R else masked
                          for jr in range(B_KROWS)])
        table.append(per_q)
    return table


def _attn_b_kernel(q_ref, k_ref, v_ref, tile_ref, o_ref, bias_ref, *, rows):
    n_grp = rows // B_QROWS

    @pl.when(pl.program_id(0) == 0)
    def _():
        right = lax.broadcasted_iota(jnp.int32, (GRID_W, 2 * GRID_W), 1) >= GRID_W
        for case, per_q in enumerate(_window_row_offsets(rows)):
            for rq, offs in enumerate(per_q):
                for h in range(N_HEADS_B):
                    for j in range(0, B_KROWS, 2):
                        pair = jnp.where(right, tile_ref[h, offs[j + 1]], tile_ref[h, offs[j]])
                        bias_ref[h, case, rq * GRID_W:(rq + 1) * GRID_W, j * GRID_W:(j + 2) * GRID_W] = pair

    def group(i, c):
        r0 = i * B_QROWS
        ws = jnp.clip(r0 - WIN_R // 2, 0, rows - B_KROWS)
        case = jnp.where(i == 0, 0, jnp.where(i == n_grp - 1, 2, 1))
        q0 = pl.multiple_of(r0 * GRID_W, B_NQ)
        k0 = pl.multiple_of(ws * GRID_W, GRID_W)
        for h in range(N_HEADS_B):
            sl = slice(h * HEAD_DIM, (h + 1) * HEAD_DIM)
            q = q_ref[pl.ds(q0, B_NQ), sl]
            k = k_ref[pl.ds(k0, B_NK), sl]
            v = v_ref[pl.ds(k0, B_NK), sl]
            s = lax.dot_general(q, k, (((1,), (1,)), ((), ())), preferred_element_type=F32)
            s = s + bias_ref[h, case]
            m = jnp.max(s, axis=-1, keepdims=True)
            p = jnp.exp(s - m)
            l = jnp.sum(p, axis=-1, keepdims=True)
            acc = jnp.dot(p.astype(BF16), v, preferred_element_type=F32)
            o_ref[pl.ds(q0, B_NQ), sl] = (acc * (1.0 / l)).astype(o_ref.dtype)
        return c

    lax.fori_loop(0, n_grp, group, 0)


def _neighbourhood_bias_tiles(rpb):
    n_h = rpb.shape[0]
    c = np.arange(GRID_W)
    dc = np.clip(c[None, :] - c[:, None], -(WIN_C - 1), WIN_C - 1) + (WIN_C - 1)
    col_start = np.clip(c - WIN_C // 2, 0, GRID_W - WIN_C)
    col_ok = (c[None, :] >= col_start[:, None]) & (c[None, :] < col_start[:, None] + WIN_C)
    col_pick = np.zeros((2 * WIN_C - 1, GRID_W * GRID_W), np.float32)
    col_pick[dc.ravel(), np.arange(GRID_W * GRID_W)] = 1.0
    tiles = jnp.einsum("hab,bc->hac", rpb.astype(F32), col_pick, precision=lax.Precision.HIGHEST)
    tiles = jnp.where(col_ok.reshape(-1)[None, None], tiles, NEG_INF).reshape(n_h, -1, GRID_W, GRID_W)
    tiles = jnp.concatenate([tiles, jnp.full((n_h, 1, GRID_W, GRID_W), NEG_INF, F32)], axis=1)
    return jnp.concatenate([tiles, tiles], axis=-1)


def _attn_b(proj3, tiles):
    batch, seq, _ = proj3.shape
    rows = seq // GRID_W

    def qkv_spec(t):
        return pl.BlockSpec((None, seq, GROUP_W), lambda b: (b, 0, t * (QKV_W // GROUP_W) + B_BLOCK0))

    tile_spec = pl.BlockSpec(tiles.shape, lambda b: (0, 0, 0, 0), pipeline_mode=pl.Buffered(1))
    return pl.pallas_call(
        functools.partial(_attn_b_kernel, rows=rows),
        name="attn_b",
        grid=(batch,),
        in_specs=[qkv_spec(0), qkv_spec(1), qkv_spec(2), tile_spec],
        out_specs=pl.BlockSpec((None, seq, GROUP_W), lambda b: (b, 0, 0)),
        out_shape=jax.ShapeDtypeStruct((batch, seq, GROUP_W), BF16),
        scratch_shapes=[pltpu.VMEM((N_HEADS_B, 3, B_NQ, B_NK), F32)],
        compiler_params=pltpu.CompilerParams(
            dimension_semantics=("arbitrary",), vmem_limit_bytes=VMEM_LIMIT),
    )(proj3, proj3, proj3, tiles)


def _mix_out_kernel(oa_ref, ob_ref, ga_ref, gb_ref, bg_ref, x_ref, pa_ref, pb_ref, wo_ref, h_ref):
    ya = jnp.dot(oa_ref[...], pa_ref[...], preferred_element_type=F32)
    yb = jnp.dot(ob_ref[...], pb_ref[...], preferred_element_type=F32)
    ga = jax.nn.sigmoid(ga_ref[...].astype(F32) + bg_ref[:, :D_MODEL])
    gb = jax.nn.sigmoid(gb_ref[...].astype(F32) + bg_ref[:, D_MODEL:])
    mixed = (ga * ya + gb * yb).astype(BF16)
    h_ref[...] = x_ref[...] + jnp.dot(mixed, wo_ref[...], preferred_element_type=F32)


def _mix_out(oa, ob, proj, b_gate, x2, pa, pb, wo, tm=512):
    n_tok = x2.shape[0]
    row512 = pl.BlockSpec((tm, GROUP_W), lambda i: (i, 0))
    const = pl.Buffered(1)
    return pl.pallas_call(
        _mix_out_kernel,
        name="mix_out",
        grid=(n_tok // tm,),
        in_specs=[row512, row512,
                  pl.BlockSpec((tm, D_MODEL), lambda i: (i, GATE_BLOCK0)),
                  pl.BlockSpec((tm, D_MODEL), lambda i: (i, GATE_BLOCK0 + 1)),
                  pl.BlockSpec((1, 2 * D_MODEL), lambda i: (0, 0)),
                  pl.BlockSpec((tm, D_MODEL), lambda i: (i, 0)),
                  pl.BlockSpec((GROUP_W, D_MODEL), lambda i: (0, 0), pipeline_mode=const),
                  pl.BlockSpec((GROUP_W, D_MODEL), lambda i: (0, 0), pipeline_mode=const),
                  pl.BlockSpec((D_MODEL, D_MODEL), lambda i: (0, 0), pipeline_mode=const)],
        out_specs=pl.BlockSpec((tm, D_MODEL), lambda i: (i, 0)),
        out_shape=jax.ShapeDtypeStruct((n_tok, D_MODEL), F32),
        compiler_params=pltpu.CompilerParams(
            dimension_semantics=("parallel",), vmem_limit_bytes=VMEM_LIMIT),
    )(oa, ob, proj, proj, b_gate, x2, pa, pb, wo)


def _ffn_kernel(h_ref, g_ref, wu_ref, wd_ref, o_ref, hn_ref):
    @pl.when(pl.program_id(1) == 0)
    def _():
        h = h_ref[...]
        hn_ref[...] = _rms(h, g_ref[...]).astype(BF16)
        o_ref[...] = h

    u = jnp.maximum(jnp.dot(hn_ref[...], wu_ref[...], preferred_element_type=F32), 0.0)
    o_ref[...] += jnp.dot((u * u).astype(BF16), wd_ref[...], preferred_element_type=F32)


def _ffn(h, gain, wu, wd, tm=512, tf=2048):
    n_tok = h.shape[0]
    return pl.pallas_call(
        _ffn_kernel,
        name="ffn",
        grid=(n_tok // tm, D_FF // tf),
        in_specs=[pl.BlockSpec((tm, D_MODEL), lambda i, f: (i, 0)),
                  pl.BlockSpec((1, D_MODEL), lambda i, f: (0, 0)),
                  pl.BlockSpec((D_MODEL, tf), lambda i, f: (0, f)),
                  pl.BlockSpec((tf, D_MODEL), lambda i, f: (f, 0))],
        out_specs=pl.BlockSpec((tm, D_MODEL), lambda i, f: (i, 0)),
        out_shape=jax.ShapeDtypeStruct((n_tok, D_MODEL), F32),
        scratch_shapes=[pltpu.VMEM((tm, D_MODEL), BF16)],
        compiler_params=pltpu.CompilerParams(
            dimension_semantics=("parallel", "arbitrary"), vmem_limit_bytes=VMEM_LIMIT),
    )(h, gain, wu, wd)


def kernel(x, norm_mix, w_in, b_gate, q_norm_a, k_norm_a, q_norm_b, k_norm_b, rpb_b,
           w_proj_a, w_proj_b, w_out, norm_ffn, w_up, w_down):
    batch, seq, d_model = x.shape
    assert d_model == D_MODEL and seq % (DILATED_PAIRS[-1][1] * Q_BLK) == 0
    for win, dil in DILATED_PAIRS:
        assert win // (2 * dil) == HALF_WIN
    depth = norm_mix.shape[0]
    cos, sin = _rope_tables(seq)
    h = x.reshape(batch * seq, D_MODEL)
    for l in range(depth):
        gains_a = jnp.stack([q_norm_a[l] * SCALE, k_norm_a[l]]).astype(F32)[:, None, :]
        rope_c = cos[None] * gains_a
        rope_s = sin[None] * jnp.roll(gains_a, HEAD_DIM // 2, axis=-1)
        gains_b = jnp.stack([q_norm_b[l] * SCALE, k_norm_b[l]]).astype(F32)[:, None, :]
        proj = _in_proj(h, norm_mix[l].reshape(1, D_MODEL), w_in[l].astype(BF16), rope_c, rope_s, gains_b, seq)
        proj3 = proj.reshape(batch, seq, D_PROJ)
        oa = _attn_a(proj3)
        ob = _attn_b(proj3, _neighbourhood_bias_tiles(rpb_b[l]))
        h = _mix_out(oa.reshape(batch * seq, GROUP_W), ob.reshape(batch * seq, GROUP_W), proj,
                     b_gate[l].reshape(1, 2 * D_MODEL), h,
                     w_proj_a[l].astype(BF16), w_proj_b[l].astype(BF16), w_out[l].astype(BF16))
        h = _ffn(h, norm_ffn[l].reshape(1, D_MODEL), w_up[l].astype(BF16), w_down[l].astype(BF16))
    return h.reshape(batch, seq, D_MODEL)
```

```python
import functools

import numpy as np
import jax
import jax.numpy as jnp
from jax import lax
from jax.experimental import pallas as pl
from jax.experimental.pallas import tpu as pltpu

D_MODEL = 2048
HEAD_DIM = 128
N_HEADS = 16
N_HEADS_A = 12
N_HEADS_B = 4
DILATED_PAIRS = ((128, 1), (512, 4), (2048, 16))
N_GROUPS_A = len(DILATED_PAIRS)
HEADS_PER_GROUP = 4
GROUP_W = HEADS_PER_GROUP * HEAD_DIM
GRID_W = 64
WIN_R = 8
WIN_C = 16
QKV_W = N_HEADS * HEAD_DIM
D_FF = 4 * D_MODEL
ROPE_THETA = 10000.0
EPS = 1e-6
NEG_INF = -1e30
SCALE = HEAD_DIM ** -0.5

D_PROJ = 3 * QKV_W + 2 * D_MODEL
B_BLOCK0 = N_HEADS_A * HEAD_DIM // GROUP_W
GATE_BLOCK0 = 3 * QKV_W // D_MODEL

VMEM_LIMIT = 56 * 1024 * 1024

F32 = jnp.float32
BF16 = jnp.bfloat16


def _rope_tables(seq):
    pos = np.arange(seq, dtype=np.float64)
    inv = ROPE_THETA ** (-np.arange(0, HEAD_DIM, 2, dtype=np.float64) / HEAD_DIM)
    ang = pos[:, None] * inv[None, :]
    cos = np.concatenate([np.cos(ang), np.cos(ang)], axis=-1)
    sin = np.concatenate([-np.sin(ang), np.sin(ang)], axis=-1)
    return jnp.asarray(cos, F32), jnp.asarray(sin, F32)


def _rms(x, g):
    ms = jnp.mean(x * x, axis=-1, keepdims=True)
    return (x * lax.rsqrt(ms + EPS)) * g


def _in_proj_kernel(x_ref, g_ref, w_ref, rc_ref, rs_ref, gb_ref, o_ref, xn_ref):
    j = pl.program_id(1)

    @pl.when(j == 0)
    def _():
        xn_ref[...] = _rms(x_ref[...], g_ref[...]).astype(BF16)

    def project():
        return jnp.dot(xn_ref[...], w_ref[...], preferred_element_type=F32)

    @pl.when(j < 2)
    def _():
        acc = project()
        for h in range(N_HEADS):
            sl = slice(h * HEAD_DIM, (h + 1) * HEAD_DIM)
            y = acc[:, sl]
            inv_rms = lax.rsqrt(jnp.mean(y * y, axis=-1, keepdims=True) + EPS)
            if h < N_HEADS_A:
                y = y * rc_ref[...] + pltpu.roll(y, HEAD_DIM // 2, 1) * rs_ref[...]
            else:
                y = y * gb_ref[...]
            o_ref[:, sl] = (y * inv_rms).astype(o_ref.dtype)

    @pl.when(j >= 2)
    def _():
        o_ref[...] = project().astype(o_ref.dtype)


def _in_proj(x2, gain, w, rope_c, rope_s, gain_b, seq, tm=1024):
    n_tok = x2.shape[0]
    tn = QKV_W
    assert seq % tm == 0 and D_PROJ % tn == 0
    tiles_per_seq = seq // tm

    def qk(j):
        return jnp.minimum(j, 1)

    return pl.pallas_call(
        _in_proj_kernel,
        name="in_proj",
        grid=(n_tok // tm, D_PROJ // tn),
        in_specs=[
            pl.BlockSpec((tm, D_MODEL), lambda i, j: (i, 0)),
            pl.BlockSpec((1, D_MODEL), lambda i, j: (0, 0)),
            pl.BlockSpec((D_MODEL, tn), lambda i, j: (0, j)),
            pl.BlockSpec((None, tm, HEAD_DIM), lambda i, j: (qk(j), i % tiles_per_seq, 0)),
            pl.BlockSpec((None, tm, HEAD_DIM), lambda i, j: (qk(j), i % tiles_per_seq, 0)),
            pl.BlockSpec((None, 1, HEAD_DIM), lambda i, j: (qk(j), 0, 0)),
        ],
        out_specs=pl.BlockSpec((tm, tn), lambda i, j: (i, j)),
        out_shape=jax.ShapeDtypeStruct((n_tok, D_PROJ), BF16),
        scratch_shapes=[pltpu.VMEM((tm, D_MODEL), BF16)],
        compiler_params=pltpu.CompilerParams(
            dimension_semantics=("parallel", "arbitrary"), vmem_limit_bytes=VMEM_LIMIT),
    )(x2, gain, w, rope_c, rope_s, gain_b)


Q_BLK = 128
HALF_WIN = 64
A_WIN = 2 * Q_BLK
EXP_SPAN_LIMIT = 80.0


def _band_masks():
    rel = np.arange(A_WIN)[None, :] - np.arange(Q_BLK)[:, None]
    tiles = [np.where(np.abs(rel - off) <= HALF_WIN, 0.0, NEG_INF) for off in (0, HALF_WIN, 2 * HALF_WIN)]
    return jnp.asarray(np.stack(tiles), F32)


def _attn_a_kernel(*refs, seq, dils):
    n_g = len(dils)
    qkv_refs = refs[:3 * n_g]
    bound_ref, mask_ref, o_ref, qn_ref, kn_ref, vn_ref, f_ref, og_ref, lg_ref, band_ref = refs[3 * n_g:]
    n_blk = seq // Q_BLK
    bound = bound_ref[0]
    band_ref[...] = mask_ref[...] - bound
    vn_ref[:, :, HEAD_DIM:] = jnp.ones((n_g, seq, HEAD_DIM), BF16)

    def first_stride(dil):
        return 4 if dil % 8 == 0 else dil

    slabs = iter(range(f_ref.shape[0]))
    stage = {g: tuple(f_ref.at[next(slabs)] for _ in range(3)) for g, dil in enumerate(dils) if dil > 1}
    mid = {g: tuple(f_ref.at[next(slabs)] for _ in range(3))
           for g, dil in enumerate(dils) if first_stride(dil) != dil}

    def block_rows(n):
        return pl.ds(pl.multiple_of(n * Q_BLK, Q_BLK), Q_BLK)

    def token_rows(n, dil):
        if dil == 1:
            return block_rows(n)
        sub_len = seq // dil
        base = n * Q_BLK
        r = base // sub_len
        return pl.ds(r + (base - r * sub_len) * dil, Q_BLK, stride=dil)

    def stage_in(n, c):
        rows = block_rows(n)
        for g, dil in enumerate(dils):
            q_ref, k_ref, v_ref = qkv_refs[3 * g:3 * g + 3]
            if dil == 1:
                qn_ref[g, rows, :] = q_ref[rows, :]
                kn_ref[g, rows, :] = k_ref[rows, :]
                vn_ref[g, rows, :HEAD_DIM] = v_ref[rows, :]
            else:
                fq, fk, fv = stage[g]
                fq[rows, :] = q_ref[rows, :].astype(F32)
                fk[rows, :] = k_ref[rows, :].astype(F32)
                fv[rows, :] = v_ref[rows, :].astype(F32)
        return c

    lax.fori_loop(0, n_blk, stage_in, 0, unroll=2)

    def put_operands(g, dst, q, k, v):
        qn_ref[g, dst, :] = q.astype(BF16)
        kn_ref[g, dst, :] = k.astype(BF16)
        vn_ref[g, dst, :HEAD_DIM] = v.astype(BF16)

    def gather(n, c):
        dst = block_rows(n)
        for g, dil in enumerate(dils):
            if dil > 1:
                src = token_rows(n, first_stride(dil))
                q, k, v = (slab[src, :] for slab in stage[g])
                if g in mid:
                    for slab, x in zip(mid[g], (q, k, v)):
                        slab[dst, :] = x
                else:
                    put_operands(g, dst, q, k, v)
        return c

    lax.fori_loop(0, n_blk, gather, 0, unroll=2)

    def gather_second(n, c):
        dst = block_rows(n)
        for g in mid:
            dil = dils[g]
            s1 = first_stride(dil)
            sub_len = seq // dil
            base = n * Q_BLK
            r = base // sub_len
            start = (r % s1) * (seq // s1) + r // s1 + (base - r * sub_len) * (dil // s1)
            src = pl.ds(start, Q_BLK, stride=dil // s1)
            put_operands(g, dst, *(slab[src, :] for slab in mid[g]))
        return c

    if mid:
        lax.fori_loop(0, n_blk, gather_second, 0, unroll=2)

    def scatter_later(dil):
        return dil % 8 == 0

    def block(n, c, row_max):
        q0 = pl.multiple_of(n * Q_BLK, Q_BLK)
        for g, dil in enumerate(dils):
            sub_len = seq // dil
            win = min(A_WIN, sub_len)
            sub0 = (n // (sub_len // Q_BLK)) * sub_len
            q_loc = q0 - sub0
            k_loc = jnp.clip(q_loc - HALF_WIN, 0, sub_len - win)
            k0 = pl.multiple_of(sub0 + k_loc, HALF_WIN)
            band = band_ref[(q_loc - k_loc) // HALF_WIN, :, :win]
            q = qn_ref[g, pl.ds(q0, Q_BLK), :]
            k = kn_ref[g, pl.ds(k0, win), :]
            s = lax.dot_general(q, k, (((1,), (1,)), ((), ())), preferred_element_type=F32) + band
            if row_max:
                m = jnp.max(s, axis=-1, keepdims=True)
                s = s - m
            p = jnp.exp(s).astype(BF16)
            acc = jnp.dot(p, vn_ref[g, pl.ds(k0, win), :], preferred_element_type=F32)
            denom = acc[:, HEAD_DIM:]
            out = acc[:, :HEAD_DIM] * (1.0 / denom)
            lse = jnp.log(denom)
            if row_max:
                lse = lse + m
            if scatter_later(dil):
                stage[g][0][pl.ds(q0, Q_BLK), :] = out
                stage[g][1][pl.ds(q0, Q_BLK), :] = lse
            else:
                dst = token_rows(n, dil)
                og_ref.at[g][dst, :] = out
                lg_ref.at[g][dst, :] = lse
        return c

    @pl.when(2.0 * bound < EXP_SPAN_LIMIT)
    def _():
        lax.fori_loop(0, n_blk, functools.partial(block, row_max=False), 0, unroll=8)

    @pl.when(2.0 * bound >= EXP_SPAN_LIMIT)
    def _():
        lax.fori_loop(0, n_blk, functools.partial(block, row_max=True), 0, unroll=2)

    def scatter(n, c):
        src = block_rows(n)
        for g, dil in enumerate(dils):
            if scatter_later(dil):
                dst = token_rows(n, dil)
                og_ref.at[g][dst, :] = stage[g][0][src, :]
                lg_ref.at[g][dst, :] = stage[g][1][src, :]
        return c

    if any(scatter_later(dil) for dil in dils):
        lax.fori_loop(0, n_blk, scatter, 0, unroll=2)

    def combine(n, c):
        rows = block_rows(n)
        lses = [lg_ref[g, rows, :] for g in range(n_g)]
        mx = functools.reduce(jnp.maximum, lses)
        es = [jnp.exp(x - mx) for x in lses]
        inv = 1.0 / functools.reduce(lambda a, b: a + b, es)
        oa = (es[0] * inv) * og_ref[0, rows, :]
        for g in range(1, n_g):
            oa = oa + (es[g] * inv) * og_ref[g, rows, :]
        o_ref[rows, :] = oa.astype(o_ref.dtype)
        return c

    lax.fori_loop(0, n_blk, combine, 0, unroll=2)


def _score_bound(gq, gk):
    return 1.01 * HEAD_DIM * jnp.max(jnp.abs(gq)) * jnp.max(jnp.abs(gk))


def _attn_a(proj3, bound):
    batch, seq, _ = proj3.shape
    dils = tuple(d for _, d in DILATED_PAIRS)
    n_g = len(dils)
    for d in dils:
        assert d % 8 != 0 or (d % 4 == 0 and (d // 4) % 8 != 0)
    n_slabs = 3 * sum((d > 1) + (d % 8 == 0) for d in dils)
    masks = _band_masks()
    qkv_specs = [pl.BlockSpec((None, seq, HEAD_DIM),
                              lambda b, s, t=t, g=g: (b, 0, t * N_HEADS + g * HEADS_PER_GROUP + s))
                 for g in range(n_g) for t in range(3)]
    return pl.pallas_call(
        functools.partial(_attn_a_kernel, seq=seq, dils=dils),
        name="attn_a",
        grid=(batch, HEADS_PER_GROUP),
        in_specs=qkv_specs + [pl.BlockSpec(memory_space=pltpu.SMEM),
                              pl.BlockSpec(masks.shape, lambda b, s: (0, 0, 0), pipeline_mode=pl.Buffered(1))],
        out_specs=pl.BlockSpec((None, seq, HEAD_DIM), lambda b, s: (b, 0, s)),
        out_shape=jax.ShapeDtypeStruct((batch, seq, GROUP_W), BF16),
        scratch_shapes=[pltpu.VMEM((n_g, seq, HEAD_DIM), BF16)] * 2
        + [pltpu.VMEM((n_g, seq, 2 * HEAD_DIM), BF16)]
        + [pltpu.VMEM((n_slabs, seq, HEAD_DIM), F32)]
        + [pltpu.VMEM((n_g, seq, HEAD_DIM), F32)] * 2
        + [pltpu.VMEM(masks.shape, F32)],
        compiler_params=pltpu.CompilerParams(
            dimension_semantics=("parallel", "arbitrary"), vmem_limit_bytes=VMEM_LIMIT),
    )(*([proj3] * (3 * n_g)), bound.reshape(1).astype(F32), masks)


B_QROWS = 4
B_KROWS = B_QROWS + WIN_R
B_NQ = B_QROWS * GRID_W
B_NK = B_KROWS * GRID_W


def _window_row_offsets(rows):
    n_grp = rows // B_QROWS
    assert rows % B_QROWS == 0 and n_grp >= 3
    masked = 2 * WIN_R - 1
    table = []
    for i in (0, 1, n_grp - 1):
        r0 = i * B_QROWS
        ws = int(np.clip(r0 - WIN_R // 2, 0, rows - B_KROWS))
        per_q = []
        for rq in range(B_QROWS):
            r = r0 + rq
            rs = int(np.clip(r - WIN_R // 2, 0, rows - WIN_R))
            per_q.append([ws + jr - r + WIN_R - 1 if rs <= ws + jr < rs + WIN_R else masked
                          for jr in range(B_KROWS)])
        table.append(per_q)
    return table


def _attn_b_kernel(shift_ref, q_ref, k_ref, v_ref, tile_ref, o_ref, bias_ref, *, rows):
    n_grp = rows // B_QROWS
    shift = shift_ref[0]
    span = shift_ref[1]

    @pl.when(pl.program_id(0) == 0)
    def _():
        right = lax.broadcasted_iota(jnp.int32, (GRID_W, 2 * GRID_W), 1) >= GRID_W
        for case, per_q in enumerate(_window_row_offsets(rows)):
            for rq, offs in enumerate(per_q):
                for h in range(N_HEADS_B):
                    for j in range(0, B_KROWS, 2):
                        pair = jnp.where(right, tile_ref[h, offs[j + 1]], tile_ref[h, offs[j]])
                        bias_ref[h, case, rq * GRID_W:(rq + 1) * GRID_W,
                                 j * GRID_W:(j + 2) * GRID_W] = pair - shift

    def group(i, c, row_max):
        r0 = i * B_QROWS
        ws = jnp.clip(r0 - WIN_R // 2, 0, rows - B_KROWS)
        case = jnp.where(i == 0, 0, jnp.where(i == n_grp - 1, 2, 1))
        q0 = pl.multiple_of(r0 * GRID_W, B_NQ)
        k0 = pl.multiple_of(ws * GRID_W, GRID_W)
        for h in range(N_HEADS_B):
            sl = slice(h * HEAD_DIM, (h + 1) * HEAD_DIM)
            q = q_ref[pl.ds(q0, B_NQ), sl]
            k = k_ref[pl.ds(k0, B_NK), sl]
            v = v_ref[pl.ds(k0, B_NK), sl]
            s = lax.dot_general(q, k, (((1,), (1,)), ((), ())), preferred_element_type=F32)
            s = s + bias_ref[h, case]
            if row_max:
                s = s - jnp.max(s, axis=-1, keepdims=True)
            p = jnp.exp(s)
            l = jnp.sum(p, axis=-1, keepdims=True)
            acc = jnp.dot(p.astype(BF16), v, preferred_element_type=F32)
            o_ref[pl.ds(q0, B_NQ), sl] = (acc * (1.0 / l)).astype(o_ref.dtype)
        return c

    @pl.when(span < EXP_SPAN_LIMIT)
    def _():
        lax.fori_loop(0, n_grp, functools.partial(group, row_max=False), 0)

    @pl.when(span >= EXP_SPAN_LIMIT)
    def _():
        lax.fori_loop(0, n_grp, functools.partial(group, row_max=True), 0)


def _neighbourhood_bias_tiles(rpb):
    n_h = rpb.shape[0]
    c = np.arange(GRID_W)
    dc = np.clip(c[None, :] - c[:, None], -(WIN_C - 1), WIN_C - 1) + (WIN_C - 1)
    col_start = np.clip(c - WIN_C // 2, 0, GRID_W - WIN_C)
    col_ok = (c[None, :] >= col_start[:, None]) & (c[None, :] < col_start[:, None] + WIN_C)
    col_pick = np.zeros((2 * WIN_C - 1, GRID_W * GRID_W), np.float32)
    col_pick[dc.ravel(), np.arange(GRID_W * GRID_W)] = 1.0
    tiles = jnp.einsum("hab,bc->hac", rpb.astype(F32), col_pick, precision=lax.Precision.HIGHEST)
    tiles = jnp.where(col_ok.reshape(-1)[None, None], tiles, NEG_INF).reshape(n_h, -1, GRID_W, GRID_W)
    tiles = jnp.concatenate([tiles, jnp.full((n_h, 1, GRID_W, GRID_W), NEG_INF, F32)], axis=1)
    return jnp.concatenate([tiles, tiles], axis=-1)


def _attn_b(proj3, tiles, qk_bound, rpb):
    batch, seq, _ = proj3.shape
    rows = seq // GRID_W
    hi, lo = jnp.max(rpb).astype(F32), jnp.min(rpb).astype(F32)
    shift = jnp.stack([qk_bound + hi, 2.0 * qk_bound + (hi - lo)]).astype(F32)

    def qkv_spec(t):
        return pl.BlockSpec((None, seq, GROUP_W), lambda b: (b, 0, t * (QKV_W // GROUP_W) + B_BLOCK0))

    tile_spec = pl.BlockSpec(tiles.shape, lambda b: (0, 0, 0, 0), pipeline_mode=pl.Buffered(1))
    return pl.pallas_call(
        functools.partial(_attn_b_kernel, rows=rows),
        name="attn_b",
        grid=(batch,),
        in_specs=[pl.BlockSpec(memory_space=pltpu.SMEM), qkv_spec(0), qkv_spec(1), qkv_spec(2), tile_spec],
        out_specs=pl.BlockSpec((None, seq, GROUP_W), lambda b: (b, 0, 0)),
        out_shape=jax.ShapeDtypeStruct((batch, seq, GROUP_W), BF16),
        scratch_shapes=[pltpu.VMEM((N_HEADS_B, 3, B_NQ, B_NK), F32)],
        compiler_params=pltpu.CompilerParams(
            dimension_semantics=("arbitrary",), vmem_limit_bytes=VMEM_LIMIT),
    )(shift, proj3, proj3, proj3, tiles)


def _mix_out_kernel(oa_ref, ob_ref, ga_ref, gb_ref, bg_ref, x_ref, pa_ref, pb_ref, wo_ref, h_ref):
    ya = jnp.dot(oa_ref[...], pa_ref[...], preferred_element_type=F32)
    yb = jnp.dot(ob_ref[...], pb_ref[...], preferred_element_type=F32)
    ga = jax.nn.sigmoid(ga_ref[...].astype(F32) + bg_ref[:, :D_MODEL])
    gb = jax.nn.sigmoid(gb_ref[...].astype(F32) + bg_ref[:, D_MODEL:])
    mixed = (ga * ya + gb * yb).astype(BF16)
    h_ref[...] = x_ref[...] + jnp.dot(mixed, wo_ref[...], preferred_element_type=F32)


def _mix_out(oa, ob, proj, b_gate, x2, pa, pb, wo, tm=512):
    n_tok = x2.shape[0]
    row512 = pl.BlockSpec((tm, GROUP_W), lambda i: (i, 0))
    const = pl.Buffered(1)
    return pl.pallas_call(
        _mix_out_kernel,
        name="mix_out",
        grid=(n_tok // tm,),
        in_specs=[row512, row512,
                  pl.BlockSpec((tm, D_MODEL), lambda i: (i, GATE_BLOCK0)),
                  pl.BlockSpec((tm, D_MODEL), lambda i: (i, GATE_BLOCK0 + 1)),
                  pl.BlockSpec((1, 2 * D_MODEL), lambda i: (0, 0)),
                  pl.BlockSpec((tm, D_MODEL), lambda i: (i, 0)),
                  pl.BlockSpec((GROUP_W, D_MODEL), lambda i: (0, 0), pipeline_mode=const),
                  pl.BlockSpec((GROUP_W, D_MODEL), lambda i: (0, 0), pipeline_mode=const),
                  pl.BlockSpec((D_MODEL, D_MODEL), lambda i: (0, 0), pipeline_mode=const)],
        out_specs=pl.BlockSpec((tm, D_MODEL), lambda i: (i, 0)),
        out_shape=jax.ShapeDtypeStruct((n_tok, D_MODEL), F32),
        compiler_params=pltpu.CompilerParams(
            dimension_semantics=("parallel",), vmem_limit_bytes=VMEM_LIMIT),
    )(oa, ob, proj, proj, b_gate, x2, pa, pb, wo)


def _ffn_kernel(h_ref, g_ref, wu_ref, wd_ref, o_ref, hn_ref):
    @pl.when(pl.program_id(1) == 0)
    def _():
        h = h_ref[...]
        hn_ref[...] = _rms(h, g_ref[...]).astype(BF16)
        o_ref[...] = h

    u = jnp.maximum(jnp.dot(hn_ref[...], wu_ref[...], preferred_element_type=F32), 0.0)
    o_ref[...] += jnp.dot((u * u).astype(BF16), wd_ref[...], preferred_element_type=F32)


def _ffn(h, gain, wu, wd, tm=512, tf=2048):
    n_tok = h.shape[0]
    return pl.pallas_call(
        _ffn_kernel,
        name="ffn",
        grid=(n_tok // tm, D_FF // tf),
        in_specs=[pl.BlockSpec((tm, D_MODEL), lambda i, f: (i, 0)),
                  pl.BlockSpec((1, D_MODEL), lambda i, f: (0, 0)),
                  pl.BlockSpec((D_MODEL, tf), lambda i, f: (0, f)),
                  pl.BlockSpec((tf, D_MODEL), lambda i, f: (f, 0))],
        out_specs=pl.BlockSpec((tm, D_MODEL), lambda i, f: (i, 0)),
        out_shape=jax.ShapeDtypeStruct((n_tok, D_MODEL), F32),
        scratch_shapes=[pltpu.VMEM((tm, D_MODEL), BF16)],
        compiler_params=pltpu.CompilerParams(
            dimension_semantics=("parallel", "arbitrary"), vmem_limit_bytes=VMEM_LIMIT),
    )(h, gain, wu, wd)


def kernel(x, norm_mix, w_in, b_gate, q_norm_a, k_norm_a, q_norm_b, k_norm_b, rpb_b,
           w_proj_a, w_proj_b, w_out, norm_ffn, w_up, w_down):
    batch, seq, d_model = x.shape
    assert d_model == D_MODEL and seq % (DILATED_PAIRS[-1][1] * Q_BLK) == 0
    for win, dil in DILATED_PAIRS:
        assert win // (2 * dil) == HALF_WIN
    depth = norm_mix.shape[0]
    cos, sin = _rope_tables(seq)
    h = x.reshape(batch * seq, D_MODEL)
    for l in range(depth):
        gains_a = jnp.stack([q_norm_a[l] * SCALE, k_norm_a[l]]).astype(F32)[:, None, :]
        rope_c = cos[None] * gains_a
        rope_s = sin[None] * jnp.roll(gains_a, HEAD_DIM // 2, axis=-1)
        gains_b = jnp.stack([q_norm_b[l] * SCALE, k_norm_b[l]]).astype(F32)[:, None, :]
        proj = _in_proj(h, norm_mix[l].reshape(1, D_MODEL), w_in[l].astype(BF16), rope_c, rope_s, gains_b, seq)
        proj3 = proj.reshape(batch, seq, D_PROJ)
        oa = _attn_a(proj3, _score_bound(gains_a[0], gains_a[1]))
        ob = _attn_b(proj3, _neighbourhood_bias_tiles(rpb_b[l]), _score_bound(gains_b[0], gains_b[1]), rpb_b[l])
        h = _mix_out(oa.reshape(batch * seq, GROUP_W), ob.reshape(batch * seq, GROUP_W), proj,
                     b_gate[l].reshape(1, 2 * D_MODEL), h,
                     w_proj_a[l].astype(BF16), w_proj_b[l].astype(BF16), w_out[l].astype(BF16))
        h = _ffn(h, norm_ffn[l].reshape(1, D_MODEL), w_up[l].astype(BF16), w_down[l].astype(BF16))
    return h.reshape(batch, seq, D_MODEL)
```

```python
import functools

import numpy as np
import jax
import jax.numpy as jnp
from jax import lax
from jax.experimental import pallas as pl
from jax.experimental.pallas import tpu as pltpu

D_MODEL = 2048
HEAD_DIM = 128
N_HEADS = 16
N_HEADS_A = 12
N_HEADS_B = 4
DILATED_PAIRS = ((128, 1), (512, 4), (2048, 16))
N_GROUPS_A = len(DILATED_PAIRS)
HEADS_PER_GROUP = 4
GROUP_W = HEADS_PER_GROUP * HEAD_DIM
GRID_W = 64
WIN_R = 8
WIN_C = 16
QKV_W = N_HEADS * HEAD_DIM
D_FF = 4 * D_MODEL
ROPE_THETA = 10000.0
EPS = 1e-6
NEG_INF = -1e30
SCALE = HEAD_DIM ** -0.5

D_PROJ = 3 * QKV_W + 2 * D_MODEL
B_BLOCK0 = N_HEADS_A * HEAD_DIM // GROUP_W
GATE_BLOCK0 = 3 * QKV_W // D_MODEL

VMEM_LIMIT = 56 * 1024 * 1024

F32 = jnp.float32
BF16 = jnp.bfloat16


def _rope_tables(seq):
    pos = np.arange(seq, dtype=np.float64)
    inv = ROPE_THETA ** (-np.arange(0, HEAD_DIM, 2, dtype=np.float64) / HEAD_DIM)
    ang = pos[:, None] * inv[None, :]
    cos = np.concatenate([np.cos(ang), np.cos(ang)], axis=-1)
    sin = np.concatenate([-np.sin(ang), np.sin(ang)], axis=-1)
    return jnp.asarray(cos, F32), jnp.asarray(sin, F32)


def _rms(x, g):
    ms = jnp.mean(x * x, axis=-1, keepdims=True)
    return (x * lax.rsqrt(ms + EPS)) * g


def _in_proj_kernel(x_ref, g_ref, w_ref, rc_ref, rs_ref, gb_ref, o_ref, xn_ref):
    j = pl.program_id(1)

    @pl.when(j == 0)
    def _():
        xn_ref[...] = _rms(x_ref[...], g_ref[...]).astype(BF16)

    def project():
        return jnp.dot(xn_ref[...], w_ref[...], preferred_element_type=F32)

    @pl.when(j < 2)
    def _():
        acc = project()
        for h in range(N_HEADS):
            sl = slice(h * HEAD_DIM, (h + 1) * HEAD_DIM)
            y = acc[:, sl]
            inv_rms = lax.rsqrt(jnp.mean(y * y, axis=-1, keepdims=True) + EPS)
            if h < N_HEADS_A:
                y = y * rc_ref[...] + pltpu.roll(y, HEAD_DIM // 2, 1) * rs_ref[...]
            else:
                y = y * gb_ref[...]
            o_ref[:, sl] = (y * inv_rms).astype(o_ref.dtype)

    @pl.when(j >= 2)
    def _():
        o_ref[...] = project().astype(o_ref.dtype)


def _in_proj(x2, gain, w, rope_c, rope_s, gain_b, seq, tm=1024):
    n_tok = x2.shape[0]
    tn = QKV_W
    assert seq % tm == 0 and D_PROJ % tn == 0
    tiles_per_seq = seq // tm

    def qk(j):
        return jnp.minimum(j, 1)

    return pl.pallas_call(
        _in_proj_kernel,
        name="in_proj",
        grid=(n_tok // tm, D_PROJ // tn),
        in_specs=[
            pl.BlockSpec((tm, D_MODEL), lambda i, j: (i, 0)),
            pl.BlockSpec((1, D_MODEL), lambda i, j: (0, 0)),
            pl.BlockSpec((D_MODEL, tn), lambda i, j: (0, j)),
            pl.BlockSpec((None, tm, HEAD_DIM), lambda i, j: (qk(j), i % tiles_per_seq, 0)),
            pl.BlockSpec((None, tm, HEAD_DIM), lambda i, j: (qk(j), i % tiles_per_seq, 0)),
            pl.BlockSpec((None, 1, HEAD_DIM), lambda i, j: (qk(j), 0, 0)),
        ],
        out_specs=pl.BlockSpec((tm, tn), lambda i, j: (i, j)),
        out_shape=jax.ShapeDtypeStruct((n_tok, D_PROJ), BF16),
        scratch_shapes=[pltpu.VMEM((tm, D_MODEL), BF16)],
        compiler_params=pltpu.CompilerParams(
            dimension_semantics=("parallel", "arbitrary"), vmem_limit_bytes=VMEM_LIMIT),
    )(x2, gain, w, rope_c, rope_s, gain_b)


Q_BLK = 128
HALF_WIN = 64
A_WIN = 2 * Q_BLK
EXP_SPAN_LIMIT = 80.0


def _band_masks():
    rel = np.arange(A_WIN)[None, :] - np.arange(Q_BLK)[:, None]
    tiles = [np.where(np.abs(rel - off) <= HALF_WIN, 0.0, NEG_INF) for off in (0, HALF_WIN, 2 * HALF_WIN)]
    return jnp.asarray(np.stack(tiles), F32)


def _attn_a_kernel(*refs, seq, dils):
    n_g = len(dils)
    qkv_refs = refs[:3 * n_g]
    bound_ref, mask_ref, o_ref, qn_ref, kn_ref, vn_ref, f_ref, og_ref, lg_ref, band_ref = refs[3 * n_g:]
    n_blk = seq // Q_BLK
    bound = bound_ref[0]
    band_ref[...] = mask_ref[...] - bound
    vn_ref[:, :, HEAD_DIM:] = jnp.ones((n_g, seq, HEAD_DIM), BF16)

    def first_stride(dil):
        return 4 if dil % 8 == 0 else dil

    slabs = iter(range(f_ref.shape[0]))
    stage = {g: tuple(f_ref.at[next(slabs)] for _ in range(3)) for g, dil in enumerate(dils) if dil > 1}
    mid = {g: tuple(f_ref.at[next(slabs)] for _ in range(3))
           for g, dil in enumerate(dils) if first_stride(dil) != dil}

    def block_rows(n):
        return pl.ds(pl.multiple_of(n * Q_BLK, Q_BLK), Q_BLK)

    def token_rows(n, dil):
        if dil == 1:
            return block_rows(n)
        sub_len = seq // dil
        base = n * Q_BLK
        r = base // sub_len
        return pl.ds(r + (base - r * sub_len) * dil, Q_BLK, stride=dil)

    def stage_in(n, c):
        rows = block_rows(n)
        for g, dil in enumerate(dils):
            q_ref, k_ref, v_ref = qkv_refs[3 * g:3 * g + 3]
            if dil == 1:
                vn_ref[g, rows, :HEAD_DIM] = v_ref[rows, :]
            else:
                fq, fk, fv = stage[g]
                fq[rows, :] = q_ref[rows, :].astype(F32)
                fk[rows, :] = k_ref[rows, :].astype(F32)
                fv[rows, :] = v_ref[rows, :].astype(F32)
        return c

    lax.fori_loop(0, n_blk, stage_in, 0, unroll=2)

    def put_operands(g, dst, q, k, v):
        qn_ref[g, dst, :] = q.astype(BF16)
        kn_ref[g, dst, :] = k.astype(BF16)
        vn_ref[g, dst, :HEAD_DIM] = v.astype(BF16)

    def gather(n, c):
        dst = block_rows(n)
        for g, dil in enumerate(dils):
            if dil > 1:
                src = token_rows(n, first_stride(dil))
                q, k, v = (slab[src, :] for slab in stage[g])
                if g in mid:
                    for slab, x in zip(mid[g], (q, k, v)):
                        slab[dst, :] = x
                else:
                    put_operands(g, dst, q, k, v)
        return c

    lax.fori_loop(0, n_blk, gather, 0, unroll=2)

    def second_step_rows(n, g):
        dil = dils[g]
        s1 = first_stride(dil)
        sub_len = seq // dil
        base = n * Q_BLK
        r = base // sub_len
        start = (r % s1) * (seq // s1) + r // s1 + (base - r * sub_len) * (dil // s1)
        return pl.ds(start, Q_BLK, stride=dil // s1)

    def gather_second(n, c):
        for g in mid:
            put_operands(g, block_rows(n), *(slab[second_step_rows(n, g), :] for slab in mid[g]))
        return c

    if mid:
        lax.fori_loop(0, n_blk, gather_second, 0, unroll=2)

    def block(n, c, row_max):
        q0 = pl.multiple_of(n * Q_BLK, Q_BLK)
        for g, dil in enumerate(dils):
            sub_len = seq // dil
            win = min(A_WIN, sub_len)
            sub0 = (n // (sub_len // Q_BLK)) * sub_len
            q_loc = q0 - sub0
            k_loc = jnp.clip(q_loc - HALF_WIN, 0, sub_len - win)
            k0 = pl.multiple_of(sub0 + k_loc, HALF_WIN)
            band = band_ref[(q_loc - k_loc) // HALF_WIN, :, :win]
            q_src, k_src = (qkv_refs[3 * g], qkv_refs[3 * g + 1]) if dil == 1 else (qn_ref.at[g], kn_ref.at[g])
            q = q_src[pl.ds(q0, Q_BLK), :]
            k = k_src[pl.ds(k0, win), :]
            s = lax.dot_general(q, k, (((1,), (1,)), ((), ())), preferred_element_type=F32) + band
            if row_max:
                m = jnp.max(s, axis=-1, keepdims=True)
                s = s - m
            p = jnp.exp(s).astype(BF16)
            acc = jnp.dot(p, vn_ref[g, pl.ds(k0, win), :], preferred_element_type=F32)
            num, denom = acc[:, :HEAD_DIM], acc[:, HEAD_DIM:]
            if row_max:
                first, second = num * (1.0 / denom), jnp.log(denom) + m
            else:
                first, second = num, denom
            if g in mid:
                mid[g][0][pl.ds(q0, Q_BLK), :] = first
                mid[g][1][pl.ds(q0, Q_BLK), :] = second
            else:
                dst = token_rows(n, dil)
                og_ref.at[g][dst, :] = first
                lg_ref.at[g][dst, :] = second
        return c

    def scatter_first(n, c):
        for g in mid:
            dst = second_step_rows(n, g)
            stage[g][0][dst, :] = mid[g][0][block_rows(n), :]
            stage[g][1][dst, :] = mid[g][1][block_rows(n), :]
        return c

    def scatter_second(n, c):
        for g in mid:
            dst = token_rows(n, first_stride(dils[g]))
            og_ref.at[g][dst, :] = stage[g][0][block_rows(n), :]
            lg_ref.at[g][dst, :] = stage[g][1][block_rows(n), :]
        return c

    def combine(n, c, row_max):
        rows = block_rows(n)
        if row_max:
            lses = [lg_ref[g, rows, :] for g in range(n_g)]
            mx = functools.reduce(jnp.maximum, lses)
            es = [jnp.exp(x - mx) for x in lses]
            inv = 1.0 / functools.reduce(lambda a, b: a + b, es)
            oa = (es[0] * inv) * og_ref[0, rows, :]
            for g in range(1, n_g):
                oa = oa + (es[g] * inv) * og_ref[g, rows, :]
        else:
            num = functools.reduce(lambda a, b: a + b, [og_ref[g, rows, :] for g in range(n_g)])
            den = functools.reduce(lambda a, b: a + b, [lg_ref[g, rows, :] for g in range(n_g)])
            oa = num * (1.0 / den)
        o_ref[rows, :] = oa.astype(o_ref.dtype)
        return c

    def attend(row_max, unroll):
        lax.fori_loop(0, n_blk, functools.partial(block, row_max=row_max), 0, unroll=unroll)
        if mid:
            lax.fori_loop(0, n_blk, scatter_first, 0, unroll=2)
            lax.fori_loop(0, n_blk, scatter_second, 0, unroll=2)
        lax.fori_loop(0, n_blk, functools.partial(combine, row_max=row_max), 0, unroll=2)

    @pl.when(2.0 * bound < EXP_SPAN_LIMIT)
    def _():
        attend(row_max=False, unroll=8)

    @pl.when(2.0 * bound >= EXP_SPAN_LIMIT)
    def _():
        attend(row_max=True, unroll=2)


def _score_bound(gq, gk):
    return 1.01 * HEAD_DIM * jnp.max(jnp.abs(gq)) * jnp.max(jnp.abs(gk))


def _attn_a(proj3, bound):
    batch, seq, _ = proj3.shape
    dils = tuple(d for _, d in DILATED_PAIRS)
    n_g = len(dils)
    for d in dils:
        assert d % 8 != 0 or (d % 4 == 0 and (d // 4) % 8 != 0)
    n_slabs = 3 * sum((d > 1) + (d % 8 == 0) for d in dils)
    masks = _band_masks()
    qkv_specs = [pl.BlockSpec((None, seq, HEAD_DIM),
                              lambda b, s, t=t, g=g: (b, 0, t * N_HEADS + g * HEADS_PER_GROUP + s))
                 for g in range(n_g) for t in range(3)]
    return pl.pallas_call(
        functools.partial(_attn_a_kernel, seq=seq, dils=dils),
        name="attn_a",
        grid=(batch, HEADS_PER_GROUP),
        in_specs=qkv_specs + [pl.BlockSpec(memory_space=pltpu.SMEM),
                              pl.BlockSpec(masks.shape, lambda b, s: (0, 0, 0), pipeline_mode=pl.Buffered(1))],
        out_specs=pl.BlockSpec((None, seq, HEAD_DIM), lambda b, s: (b, 0, s)),
        out_shape=jax.ShapeDtypeStruct((batch, seq, GROUP_W), BF16),
        scratch_shapes=[pltpu.VMEM((n_g, seq, HEAD_DIM), BF16)] * 2
        + [pltpu.VMEM((n_g, seq, 2 * HEAD_DIM), BF16)]
        + [pltpu.VMEM((n_slabs, seq, HEAD_DIM), F32)]
        + [pltpu.VMEM((n_g, seq, HEAD_DIM), F32)] * 2
        + [pltpu.VMEM(masks.shape, F32)],
        compiler_params=pltpu.CompilerParams(
            dimension_semantics=("parallel", "arbitrary"), vmem_limit_bytes=VMEM_LIMIT),
    )(*([proj3] * (3 * n_g)), bound.reshape(1).astype(F32), masks)


B_QROWS = 4
B_KROWS = B_QROWS + WIN_R
B_NQ = B_QROWS * GRID_W
B_NK = B_KROWS * GRID_W


def _window_row_offsets(rows):
    n_grp = rows // B_QROWS
    assert rows % B_QROWS == 0 and n_grp >= 3
    masked = 2 * WIN_R - 1
    table = []
    for i in (0, 1, n_grp - 1):
        r0 = i * B_QROWS
        ws = int(np.clip(r0 - WIN_R // 2, 0, rows - B_KROWS))
        per_q = []
        for rq in range(B_QROWS):
            r = r0 + rq
            rs = int(np.clip(r - WIN_R // 2, 0, rows - WIN_R))
            per_q.append([ws + jr - r + WIN_R - 1 if rs <= ws + jr < rs + WIN_R else masked
                          for jr in range(B_KROWS)])
        table.append(per_q)
    return table


def _attn_b_kernel(shift_ref, q_ref, k_ref, v_ref, tile_ref, o_ref, bias_ref, *, rows):
    n_grp = rows // B_QROWS
    shift = shift_ref[0]
    span = shift_ref[1]

    @pl.when(pl.program_id(0) == 0)
    def _():
        right = lax.broadcasted_iota(jnp.int32, (GRID_W, 2 * GRID_W), 1) >= GRID_W
        for case, per_q in enumerate(_window_row_offsets(rows)):
            for rq, offs in enumerate(per_q):
                for h in range(N_HEADS_B):
                    for j in range(0, B_KROWS, 2):
                        pair = jnp.where(right, tile_ref[h, offs[j + 1]], tile_ref[h, offs[j]])
                        bias_ref[h, case, rq * GRID_W:(rq + 1) * GRID_W,
                                 j * GRID_W:(j + 2) * GRID_W] = pair - shift

    def group(i, c, row_max):
        r0 = i * B_QROWS
        ws = jnp.clip(r0 - WIN_R // 2, 0, rows - B_KROWS)
        case = jnp.where(i == 0, 0, jnp.where(i == n_grp - 1, 2, 1))
        q0 = pl.multiple_of(r0 * GRID_W, B_NQ)
        k0 = pl.multiple_of(ws * GRID_W, GRID_W)
        for h in range(N_HEADS_B):
            sl = slice(h * HEAD_DIM, (h + 1) * HEAD_DIM)
            q = q_ref[pl.ds(q0, B_NQ), sl]
            k = k_ref[pl.ds(k0, B_NK), sl]
            v = v_ref[pl.ds(k0, B_NK), sl]
            s = lax.dot_general(q, k, (((1,), (1,)), ((), ())), preferred_element_type=F32)
            s = s + bias_ref[h, case]
            if row_max:
                s = s - jnp.max(s, axis=-1, keepdims=True)
            p = jnp.exp(s)
            l = jnp.sum(p, axis=-1, keepdims=True)
            acc = jnp.dot(p.astype(BF16), v, preferred_element_type=F32)
            o_ref[pl.ds(q0, B_NQ), sl] = (acc * (1.0 / l)).astype(o_ref.dtype)
        return c

    @pl.when(span < EXP_SPAN_LIMIT)
    def _():
        lax.fori_loop(0, n_grp, functools.partial(group, row_max=False), 0)

    @pl.when(span >= EXP_SPAN_LIMIT)
    def _():
        lax.fori_loop(0, n_grp, functools.partial(group, row_max=True), 0)


def _neighbourhood_bias_tiles(rpb):
    n_h = rpb.shape[0]
    c = np.arange(GRID_W)
    dc = np.clip(c[None, :] - c[:, None], -(WIN_C - 1), WIN_C - 1) + (WIN_C - 1)
    col_start = np.clip(c - WIN_C // 2, 0, GRID_W - WIN_C)
    col_ok = (c[None, :] >= col_start[:, None]) & (c[None, :] < col_start[:, None] + WIN_C)
    col_pick = np.zeros((2 * WIN_C - 1, GRID_W * GRID_W), np.float32)
    col_pick[dc.ravel(), np.arange(GRID_W * GRID_W)] = 1.0
    tiles = jnp.einsum("hab,bc->hac", rpb.astype(F32), col_pick, precision=lax.Precision.HIGHEST)
    tiles = jnp.where(col_ok.reshape(-1)[None, None], tiles, NEG_INF).reshape(n_h, -1, GRID_W, GRID_W)
    tiles = jnp.concatenate([tiles, jnp.full((n_h, 1, GRID_W, GRID_W), NEG_INF, F32)], axis=1)
    return jnp.concatenate([tiles, tiles], axis=-1)


def _attn_b(proj3, tiles, qk_bound, rpb):
    batch, seq, _ = proj3.shape
    rows = seq // GRID_W
    hi, lo = jnp.max(rpb).astype(F32), jnp.min(rpb).astype(F32)
    shift = jnp.stack([qk_bound + hi, 2.0 * qk_bound + (hi - lo)]).astype(F32)

    def qkv_spec(t):
        return pl.BlockSpec((None, seq, GROUP_W), lambda b: (b, 0, t * (QKV_W // GROUP_W) + B_BLOCK0))

    tile_spec = pl.BlockSpec(tiles.shape, lambda b: (0, 0, 0, 0), pipeline_mode=pl.Buffered(1))
    return pl.pallas_call(
        functools.partial(_attn_b_kernel, rows=rows),
        name="attn_b",
        grid=(batch,),
        in_specs=[pl.BlockSpec(memory_space=pltpu.SMEM), qkv_spec(0), qkv_spec(1), qkv_spec(2), tile_spec],
        out_specs=pl.BlockSpec((None, seq, GROUP_W), lambda b: (b, 0, 0)),
        out_shape=jax.ShapeDtypeStruct((batch, seq, GROUP_W), BF16),
        scratch_shapes=[pltpu.VMEM((N_HEADS_B, 3, B_NQ, B_NK), F32)],
        compiler_params=pltpu.CompilerParams(
            dimension_semantics=("arbitrary",), vmem_limit_bytes=VMEM_LIMIT),
    )(shift, proj3, proj3, proj3, tiles)


def _mix_out_kernel(oa_ref, ob_ref, ga_ref, gb_ref, bg_ref, x_ref, pa_ref, pb_ref, wo_ref, h_ref):
    ya = jnp.dot(oa_ref[...], pa_ref[...], preferred_element_type=F32)
    yb = jnp.dot(ob_ref[...], pb_ref[...], preferred_element_type=F32)
    ga = jax.nn.sigmoid(ga_ref[...].astype(F32) + bg_ref[:, :D_MODEL])
    gb = jax.nn.sigmoid(gb_ref[...].astype(F32) + bg_ref[:, D_MODEL:])
    mixed = (ga * ya + gb * yb).astype(BF16)
    h_ref[...] = x_ref[...] + jnp.dot(mixed, wo_ref[...], preferred_element_type=F32)


def _mix_out(oa, ob, proj, b_gate, x2, pa, pb, wo, tm=512):
    n_tok = x2.shape[0]
    row512 = pl.BlockSpec((tm, GROUP_W), lambda i: (i, 0))
    const = pl.Buffered(1)
    return pl.pallas_call(
        _mix_out_kernel,
        name="mix_out",
        grid=(n_tok // tm,),
        in_specs=[row512, row512,
                  pl.BlockSpec((tm, D_MODEL), lambda i: (i, GATE_BLOCK0)),
                  pl.BlockSpec((tm, D_MODEL), lambda i: (i, GATE_BLOCK0 + 1)),
                  pl.BlockSpec((1, 2 * D_MODEL), lambda i: (0, 0)),
                  pl.BlockSpec((tm, D_MODEL), lambda i: (i, 0)),
                  pl.BlockSpec((GROUP_W, D_MODEL), lambda i: (0, 0), pipeline_mode=const),
                  pl.BlockSpec((GROUP_W, D_MODEL), lambda i: (0, 0), pipeline_mode=const),
                  pl.BlockSpec((D_MODEL, D_MODEL), lambda i: (0, 0), pipeline_mode=const)],
        out_specs=pl.BlockSpec((tm, D_MODEL), lambda i: (i, 0)),
        out_shape=jax.ShapeDtypeStruct((n_tok, D_MODEL), F32),
        compiler_params=pltpu.CompilerParams(
            dimension_semantics=("parallel",), vmem_limit_bytes=VMEM_LIMIT),
    )(oa, ob, proj, proj, b_gate, x2, pa, pb, wo)


def _ffn_kernel(h_ref, g_ref, wu_ref, wd_ref, o_ref, hn_ref):
    @pl.when(pl.program_id(1) == 0)
    def _():
        h = h_ref[...]
        hn_ref[...] = _rms(h, g_ref[...]).astype(BF16)
        o_ref[...] = h

    u = jnp.maximum(jnp.dot(hn_ref[...], wu_ref[...], preferred_element_type=F32), 0.0)
    o_ref[...] += jnp.dot((u * u).astype(BF16), wd_ref[...], preferred_element_type=F32)


def _ffn(h, gain, wu, wd, tm=512, tf=2048):
    n_tok = h.shape[0]
    return pl.pallas_call(
        _ffn_kernel,
        name="ffn",
        grid=(n_tok // tm, D_FF // tf),
        in_specs=[pl.BlockSpec((tm, D_MODEL), lambda i, f: (i, 0)),
                  pl.BlockSpec((1, D_MODEL), lambda i, f: (0, 0)),
                  pl.BlockSpec((D_MODEL, tf), lambda i, f: (0, f)),
                  pl.BlockSpec((tf, D_MODEL), lambda i, f: (f, 0))],
        out_specs=pl.BlockSpec((tm, D_MODEL), lambda i, f: (i, 0)),
        out_shape=jax.ShapeDtypeStruct((n_tok, D_MODEL), F32),
        scratch_shapes=[pltpu.VMEM((tm, D_MODEL), BF16)],
        compiler_params=pltpu.CompilerParams(
            dimension_semantics=("parallel", "arbitrary"), vmem_limit_bytes=VMEM_LIMIT),
    )(h, gain, wu, wd)


def kernel(x, norm_mix, w_in, b_gate, q_norm_a, k_norm_a, q_norm_b, k_norm_b, rpb_b,
           w_proj_a, w_proj_b, w_out, norm_ffn, w_up, w_down):
    batch, seq, d_model = x.shape
    assert d_model == D_MODEL and seq % (DILATED_PAIRS[-1][1] * Q_BLK) == 0
    for win, dil in DILATED_PAIRS:
        assert win // (2 * dil) == HALF_WIN
    depth = norm_mix.shape[0]
    cos, sin = _rope_tables(seq)
    h = x.reshape(batch * seq, D_MODEL)
    for l in range(depth):
        gains_a = jnp.stack([q_norm_a[l] * SCALE, k_norm_a[l]]).astype(F32)[:, None, :]
        rope_c = cos[None] * gains_a
        rope_s = sin[None] * jnp.roll(gains_a, HEAD_DIM // 2, axis=-1)
        gains_b = jnp.stack([q_norm_b[l] * SCALE, k_norm_b[l]]).astype(F32)[:, None, :]
        proj = _in_proj(h, norm_mix[l].reshape(1, D_MODEL), w_in[l].astype(BF16), rope_c, rope_s, gains_b, seq)
        proj3 = proj.reshape(batch, seq, D_PROJ)
        oa = _attn_a(proj3, _score_bound(gains_a[0], gains_a[1]))
        ob = _attn_b(proj3, _neighbourhood_bias_tiles(rpb_b[l]), _score_bound(gains_b[0], gains_b[1]), rpb_b[l])
        h = _mix_out(oa.reshape(batch * seq, GROUP_W), ob.reshape(batch * seq, GROUP_W), proj,
                     b_gate[l].reshape(1, 2 * D_MODEL), h,
                     w_proj_a[l].astype(BF16), w_proj_b[l].astype(BF16), w_out[l].astype(BF16))
        h = _ffn(h, norm_ffn[l].reshape(1, D_MODEL), w_up[l].astype(BF16), w_down[l].astype(BF16))
    return h.reshape(batch, seq, D_MODEL)
```

```python
import functools

import numpy as np
import jax
import jax.numpy as jnp
from jax import lax
from jax.experimental import pallas as pl
from jax.experimental.pallas import tpu as pltpu

D_MODEL = 2048
HEAD_DIM = 128
N_HEADS = 16
N_HEADS_A = 12
N_HEADS_B = 4
DILATED_PAIRS = ((128, 1), (512, 4), (2048, 16))
N_GROUPS_A = len(DILATED_PAIRS)
HEADS_PER_GROUP = 4
GROUP_W = HEADS_PER_GROUP * HEAD_DIM
GRID_W = 64
WIN_R = 8
WIN_C = 16
QKV_W = N_HEADS * HEAD_DIM
D_FF = 4 * D_MODEL
ROPE_THETA = 10000.0
EPS = 1e-6
NEG_INF = -1e30
SCALE = HEAD_DIM ** -0.5

D_PROJ = 3 * QKV_W + 2 * D_MODEL
B_BLOCK0 = N_HEADS_A * HEAD_DIM // GROUP_W
GATE_BLOCK0 = 3 * QKV_W // D_MODEL

VMEM_LIMIT = 56 * 1024 * 1024

F32 = jnp.float32
BF16 = jnp.bfloat16


def _rope_tables(seq):
    pos = np.arange(seq, dtype=np.float64)
    inv = ROPE_THETA ** (-np.arange(0, HEAD_DIM, 2, dtype=np.float64) / HEAD_DIM)
    ang = pos[:, None] * inv[None, :]
    cos = np.concatenate([np.cos(ang), np.cos(ang)], axis=-1)
    sin = np.concatenate([-np.sin(ang), np.sin(ang)], axis=-1)
    return jnp.asarray(cos, F32), jnp.asarray(sin, F32)


def _rms(x, g):
    ms = jnp.mean(x * x, axis=-1, keepdims=True)
    return (x * lax.rsqrt(ms + EPS)) * g


def _in_proj_kernel(*refs, n_casts):
    x_ref, g_ref, w_ref, rc_ref, rs_ref, gb_ref = refs[:6]
    cast_srcs = refs[6:6 + n_casts]
    o_ref = refs[6 + n_casts]
    cast_dsts = refs[7 + n_casts:7 + 2 * n_casts]
    xn_ref = refs[7 + 2 * n_casts]
    j = pl.program_id(1)

    @pl.when(pl.program_id(0) * pl.num_programs(1) + j < IN_PROJ_CAST_CHUNKS)
    def _():
        for src, dst in zip(cast_srcs, cast_dsts):
            dst[...] = src[...].astype(dst.dtype)

    @pl.when(j == 0)
    def _():
        xn_ref[...] = _rms(x_ref[...], g_ref[...]).astype(BF16)

    def project():
        return jnp.dot(xn_ref[...], w_ref[...], preferred_element_type=F32)

    @pl.when(j < 2)
    def _():
        acc = project()
        for h in range(N_HEADS):
            sl = slice(h * HEAD_DIM, (h + 1) * HEAD_DIM)
            y = acc[:, sl]
            inv_rms = lax.rsqrt(jnp.mean(y * y, axis=-1, keepdims=True) + EPS)
            if h < N_HEADS_A:
                y = y * rc_ref[...] + pltpu.roll(y, HEAD_DIM // 2, 1) * rs_ref[...]
            else:
                y = y * gb_ref[...]
            o_ref[:, sl] = (y * inv_rms).astype(o_ref.dtype)

    @pl.when(j >= 2)
    def _():
        o_ref[...] = project().astype(o_ref.dtype)


def _row_chunk_spec(a, n_chunks, step):
    assert a.shape[0] % (n_chunks * 16) == 0
    return pl.BlockSpec((a.shape[0] // n_chunks, a.shape[1]),
                        lambda *ids: (jnp.minimum(step(*ids), n_chunks - 1), 0))


IN_PROJ_CAST_CHUNKS = 64


def _in_proj(x2, gain, w, rope_c, rope_s, gain_b, seq, side_casts, tm=1024):
    n_tok = x2.shape[0]
    tn = QKV_W
    assert seq % tm == 0 and D_PROJ % tn == 0
    tiles_per_seq = seq // tm
    n_i, n_j = n_tok // tm, D_PROJ // tn
    assert n_i * n_j >= IN_PROJ_CAST_CHUNKS

    def qk(j):
        return jnp.minimum(j, 1)

    cast_specs = [_row_chunk_spec(a, IN_PROJ_CAST_CHUNKS, lambda i, j: i * n_j + j) for a in side_casts]
    return pl.pallas_call(
        functools.partial(_in_proj_kernel, n_casts=len(side_casts)),
        name="in_proj",
        grid=(n_i, n_j),
        in_specs=[
            pl.BlockSpec((tm, D_MODEL), lambda i, j: (i, 0)),
            pl.BlockSpec((1, D_MODEL), lambda i, j: (0, 0)),
            pl.BlockSpec((D_MODEL, tn), lambda i, j: (0, j)),
            pl.BlockSpec((None, tm, HEAD_DIM), lambda i, j: (qk(j), i % tiles_per_seq, 0)),
            pl.BlockSpec((None, tm, HEAD_DIM), lambda i, j: (qk(j), i % tiles_per_seq, 0)),
            pl.BlockSpec((None, 1, HEAD_DIM), lambda i, j: (qk(j), 0, 0)),
        ] + cast_specs,
        out_specs=[pl.BlockSpec((tm, tn), lambda i, j: (i, j))] + cast_specs,
        out_shape=[jax.ShapeDtypeStruct((n_tok, D_PROJ), BF16)]
        + [jax.ShapeDtypeStruct(a.shape, BF16) for a in side_casts],
        scratch_shapes=[pltpu.VMEM((tm, D_MODEL), BF16)],
        compiler_params=pltpu.CompilerParams(
            dimension_semantics=("arbitrary", "arbitrary"), vmem_limit_bytes=VMEM_LIMIT),
    )(x2, gain, w, rope_c, rope_s, gain_b, *side_casts)


Q_BLK = 128
HALF_WIN = 64
A_WIN = 2 * Q_BLK
EXP_SPAN_LIMIT = 80.0


def _band_masks():
    rel = np.arange(A_WIN)[None, :] - np.arange(Q_BLK)[:, None]
    tiles = [np.where(np.abs(rel - off) <= HALF_WIN, 0.0, NEG_INF) for off in (0, HALF_WIN, 2 * HALF_WIN)]
    return jnp.asarray(np.stack(tiles), F32)


def _attn_a_kernel(*refs, seq, dils):
    n_g = len(dils)
    qkv_refs = refs[:3 * n_g]
    bound_ref, mask_ref, o_ref, qn_ref, kn_ref, vn_ref, f_ref, og_ref, lg_ref, band_ref = refs[3 * n_g:]
    n_blk = seq // Q_BLK
    bound = bound_ref[0]
    band_ref[...] = mask_ref[...] - bound
    vn_ref[:, :, HEAD_DIM:] = jnp.ones((n_g, seq, HEAD_DIM), BF16)

    def first_stride(dil):
        return 4 if dil % 8 == 0 else dil

    slabs = iter(range(f_ref.shape[0]))
    stage = {g: tuple(f_ref.at[next(slabs)] for _ in range(3)) for g, dil in enumerate(dils) if dil > 1}
    mid = {g: tuple(f_ref.at[next(slabs)] for _ in range(3))
           for g, dil in enumerate(dils) if first_stride(dil) != dil}

    def block_rows(n):
        return pl.ds(pl.multiple_of(n * Q_BLK, Q_BLK), Q_BLK)

    def token_rows(n, dil):
        if dil == 1:
            return block_rows(n)
        sub_len = seq // dil
        base = n * Q_BLK
        r = base // sub_len
        return pl.ds(r + (base - r * sub_len) * dil, Q_BLK, stride=dil)

    def stage_in(n, c):
        rows = block_rows(n)
        for g, dil in enumerate(dils):
            q_ref, k_ref, v_ref = qkv_refs[3 * g:3 * g + 3]
            if dil == 1:
                vn_ref[g, rows, :HEAD_DIM] = v_ref[rows, :]
            else:
                fq, fk, fv = stage[g]
                fq[rows, :] = q_ref[rows, :].astype(F32)
                fk[rows, :] = k_ref[rows, :].astype(F32)
                fv[rows, :] = v_ref[rows, :].astype(F32)
        return c

    lax.fori_loop(0, n_blk, stage_in, 0, unroll=2)

    def put_operands(g, dst, q, k, v):
        qn_ref[g, dst, :] = q.astype(BF16)
        kn_ref[g, dst, :] = k.astype(BF16)
        vn_ref[g, dst, :HEAD_DIM] = v.astype(BF16)

    def gather(n, c):
        dst = block_rows(n)
        for g, dil in enumerate(dils):
            if dil > 1:
                src = token_rows(n, first_stride(dil))
                q, k, v = (slab[src, :] for slab in stage[g])
                if g in mid:
                    for slab, x in zip(mid[g], (q, k, v)):
                        slab[dst, :] = x
                else:
                    put_operands(g, dst, q, k, v)
        return c

    lax.fori_loop(0, n_blk, gather, 0, unroll=2)

    def second_step_rows(n, g):
        dil = dils[g]
        s1 = first_stride(dil)
        sub_len = seq // dil
        base = n * Q_BLK
        r = base // sub_len
        start = (r % s1) * (seq // s1) + r // s1 + (base - r * sub_len) * (dil // s1)
        return pl.ds(start, Q_BLK, stride=dil // s1)

    def gather_second(n, c):
        for g in mid:
            put_operands(g, block_rows(n), *(slab[second_step_rows(n, g), :] for slab in mid[g]))
        return c

    if mid:
        lax.fori_loop(0, n_blk, gather_second, 0, unroll=2)

    def block(n, c, row_max):
        q0 = pl.multiple_of(n * Q_BLK, Q_BLK)
        for g, dil in enumerate(dils):
            sub_len = seq // dil
            win = min(A_WIN, sub_len)
            sub0 = (n // (sub_len // Q_BLK)) * sub_len
            q_loc = q0 - sub0
            k_loc = jnp.clip(q_loc - HALF_WIN, 0, sub_len - win)
            k0 = pl.multiple_of(sub0 + k_loc, HALF_WIN)
            band = band_ref[(q_loc - k_loc) // HALF_WIN, :, :win]
            q_src, k_src = (qkv_refs[3 * g], qkv_refs[3 * g + 1]) if dil == 1 else (qn_ref.at[g], kn_ref.at[g])
            q = q_src[pl.ds(q0, Q_BLK), :]
            k = k_src[pl.ds(k0, win), :]
            s = lax.dot_general(q, k, (((1,), (1,)), ((), ())), preferred_element_type=F32) + band
            if row_max:
                m = jnp.max(s, axis=-1, keepdims=True)
                s = s - m
            p = jnp.exp(s).astype(BF16)
            acc = jnp.dot(p, vn_ref[g, pl.ds(k0, win), :], preferred_element_type=F32)
            num, denom = acc[:, :HEAD_DIM], acc[:, HEAD_DIM:]
            if row_max:
                first, second = num * (1.0 / denom), jnp.log(denom) + m
            else:
                first, second = num, denom
            if g in mid:
                mid[g][0][pl.ds(q0, Q_BLK), :] = first
                mid[g][1][pl.ds(q0, Q_BLK), :] = second
            else:
                dst = token_rows(n, dil)
                og_ref.at[g][dst, :] = first
                lg_ref.at[g][dst, :] = second
        return c

    def scatter_first(n, c):
        for g in mid:
            dst = second_step_rows(n, g)
            stage[g][0][dst, :] = mid[g][0][block_rows(n), :]
            stage[g][1][dst, :] = mid[g][1][block_rows(n), :]
        return c

    def scatter_second(n, c):
        for g in mid:
            dst = token_rows(n, first_stride(dils[g]))
            og_ref.at[g][dst, :] = stage[g][0][block_rows(n), :]
            lg_ref.at[g][dst, :] = stage[g][1][block_rows(n), :]
        return c

    def combine(n, c, row_max):
        rows = block_rows(n)
        if row_max:
            lses = [lg_ref[g, rows, :] for g in range(n_g)]
            mx = functools.reduce(jnp.maximum, lses)
            es = [jnp.exp(x - mx) for x in lses]
            inv = 1.0 / functools.reduce(lambda a, b: a + b, es)
            oa = (es[0] * inv) * og_ref[0, rows, :]
            for g in range(1, n_g):
                oa = oa + (es[g] * inv) * og_ref[g, rows, :]
        else:
            num = functools.reduce(lambda a, b: a + b, [og_ref[g, rows, :] for g in range(n_g)])
            den = functools.reduce(lambda a, b: a + b, [lg_ref[g, rows, :] for g in range(n_g)])
            oa = num * (1.0 / den)
        o_ref[rows, :] = oa.astype(o_ref.dtype)
        return c

    def attend(row_max, unroll):
        lax.fori_loop(0, n_blk, functools.partial(block, row_max=row_max), 0, unroll=unroll)
        if mid:
            lax.fori_loop(0, n_blk, scatter_first, 0, unroll=2)
            lax.fori_loop(0, n_blk, scatter_second, 0, unroll=2)
        lax.fori_loop(0, n_blk, functools.partial(combine, row_max=row_max), 0, unroll=2)

    @pl.when(2.0 * bound < EXP_SPAN_LIMIT)
    def _():
        attend(row_max=False, unroll=8)

    @pl.when(2.0 * bound >= EXP_SPAN_LIMIT)
    def _():
        attend(row_max=True, unroll=2)


def _score_bound(gq, gk):
    return 1.01 * HEAD_DIM * jnp.max(jnp.abs(gq)) * jnp.max(jnp.abs(gk))


def _attn_a(proj3, bound):
    batch, seq, _ = proj3.shape
    dils = tuple(d for _, d in DILATED_PAIRS)
    n_g = len(dils)
    for d in dils:
        assert d % 8 != 0 or (d % 4 == 0 and (d // 4) % 8 != 0)
    n_slabs = 3 * sum((d > 1) + (d % 8 == 0) for d in dils)
    masks = _band_masks()
    qkv_specs = [pl.BlockSpec((None, seq, HEAD_DIM),
                              lambda b, s, t=t, g=g: (b, 0, t * N_HEADS + g * HEADS_PER_GROUP + s))
                 for g in range(n_g) for t in range(3)]
    return pl.pallas_call(
        functools.partial(_attn_a_kernel, seq=seq, dils=dils),
        name="attn_a",
        grid=(batch, HEADS_PER_GROUP),
        in_specs=qkv_specs + [pl.BlockSpec(memory_space=pltpu.SMEM),
                              pl.BlockSpec(masks.shape, lambda b, s: (0, 0, 0), pipeline_mode=pl.Buffered(1))],
        out_specs=pl.BlockSpec((None, seq, HEAD_DIM), lambda b, s: (b, 0, s)),
        out_shape=jax.ShapeDtypeStruct((batch, seq, GROUP_W), BF16),
        scratch_shapes=[pltpu.VMEM((n_g, seq, HEAD_DIM), BF16)] * 2
        + [pltpu.VMEM((n_g, seq, 2 * HEAD_DIM), BF16)]
        + [pltpu.VMEM((n_slabs, seq, HEAD_DIM), F32)]
        + [pltpu.VMEM((n_g, seq, HEAD_DIM), F32)] * 2
        + [pltpu.VMEM(masks.shape, F32)],
        compiler_params=pltpu.CompilerParams(
            dimension_semantics=("parallel", "arbitrary"), vmem_limit_bytes=VMEM_LIMIT),
    )(*([proj3] * (3 * n_g)), bound.reshape(1).astype(F32), masks)


B_QROWS = 4
B_KROWS = B_QROWS + WIN_R
B_NQ = B_QROWS * GRID_W
B_NK = B_KROWS * GRID_W


def _window_row_offsets(rows):
    n_grp = rows // B_QROWS
    assert rows % B_QROWS == 0 and n_grp >= 3
    masked = 2 * WIN_R - 1
    table = []
    for i in (0, 1, n_grp - 1):
        r0 = i * B_QROWS
        ws = int(np.clip(r0 - WIN_R // 2, 0, rows - B_KROWS))
        per_q = []
        for rq in range(B_QROWS):
            r = r0 + rq
            rs = int(np.clip(r - WIN_R // 2, 0, rows - WIN_R))
            per_q.append([ws + jr - r + WIN_R - 1 if rs <= ws + jr < rs + WIN_R else masked
                          for jr in range(B_KROWS)])
        table.append(per_q)
    return table


def _attn_b_kernel(shift_ref, q_ref, k_ref, v_ref, tile_ref, o_ref, bias_ref, *, rows):
    n_grp = rows // B_QROWS
    shift = shift_ref[0]
    span = shift_ref[1]

    @pl.when(pl.program_id(0) == 0)
    def _():
        right = lax.broadcasted_iota(jnp.int32, (GRID_W, 2 * GRID_W), 1) >= GRID_W
        for case, per_q in enumerate(_window_row_offsets(rows)):
            for rq, offs in enumerate(per_q):
                for h in range(N_HEADS_B):
                    for j in range(0, B_KROWS, 2):
                        pair = jnp.where(right, tile_ref[h, offs[j + 1]], tile_ref[h, offs[j]])
                        bias_ref[h, case, rq * GRID_W:(rq + 1) * GRID_W,
                                 j * GRID_W:(j + 2) * GRID_W] = pair - shift

    def group(i, c, row_max):
        r0 = i * B_QROWS
        ws = jnp.clip(r0 - WIN_R // 2, 0, rows - B_KROWS)
        case = jnp.where(i == 0, 0, jnp.where(i == n_grp - 1, 2, 1))
        q0 = pl.multiple_of(r0 * GRID_W, B_NQ)
        k0 = pl.multiple_of(ws * GRID_W, GRID_W)
        for h in range(N_HEADS_B):
            sl = slice(h * HEAD_DIM, (h + 1) * HEAD_DIM)
            q = q_ref[pl.ds(q0, B_NQ), sl]
            k = k_ref[pl.ds(k0, B_NK), sl]
            v = v_ref[pl.ds(k0, B_NK), sl]
            s = lax.dot_general(q, k, (((1,), (1,)), ((), ())), preferred_element_type=F32)
            s = s + bias_ref[h, case]
            if row_max:
                s = s - jnp.max(s, axis=-1, keepdims=True)
            p = jnp.exp(s)
            l = jnp.sum(p, axis=-1, keepdims=True)
            acc = jnp.dot(p.astype(BF16), v, preferred_element_type=F32)
            o_ref[pl.ds(q0, B_NQ), sl] = (acc * (1.0 / l)).astype(o_ref.dtype)
        return c

    @pl.when(span < EXP_SPAN_LIMIT)
    def _():
        lax.fori_loop(0, n_grp, functools.partial(group, row_max=False), 0)

    @pl.when(span >= EXP_SPAN_LIMIT)
    def _():
        lax.fori_loop(0, n_grp, functools.partial(group, row_max=True), 0)


def _neighbourhood_bias_tiles(rpb):
    n_h = rpb.shape[0]
    c = np.arange(GRID_W)
    dc = np.clip(c[None, :] - c[:, None], -(WIN_C - 1), WIN_C - 1) + (WIN_C - 1)
    col_start = np.clip(c - WIN_C // 2, 0, GRID_W - WIN_C)
    col_ok = (c[None, :] >= col_start[:, None]) & (c[None, :] < col_start[:, None] + WIN_C)
    col_pick = np.zeros((2 * WIN_C - 1, GRID_W * GRID_W), np.float32)
    col_pick[dc.ravel(), np.arange(GRID_W * GRID_W)] = 1.0
    tiles = jnp.einsum("hab,bc->hac", rpb.astype(F32), col_pick, precision=lax.Precision.HIGHEST)
    tiles = jnp.where(col_ok.reshape(-1)[None, None], tiles, NEG_INF).reshape(n_h, -1, GRID_W, GRID_W)
    tiles = jnp.concatenate([tiles, jnp.full((n_h, 1, GRID_W, GRID_W), NEG_INF, F32)], axis=1)
    return jnp.concatenate([tiles, tiles], axis=-1)


def _attn_b(proj3, tiles, qk_bound, rpb):
    batch, seq, _ = proj3.shape
    rows = seq // GRID_W
    hi, lo = jnp.max(rpb).astype(F32), jnp.min(rpb).astype(F32)
    shift = jnp.stack([qk_bound + hi, 2.0 * qk_bound + (hi - lo)]).astype(F32)

    def qkv_spec(t):
        return pl.BlockSpec((None, seq, GROUP_W), lambda b: (b, 0, t * (QKV_W // GROUP_W) + B_BLOCK0))

    tile_spec = pl.BlockSpec(tiles.shape, lambda b: (0, 0, 0, 0), pipeline_mode=pl.Buffered(1))
    return pl.pallas_call(
        functools.partial(_attn_b_kernel, rows=rows),
        name="attn_b",
        grid=(batch,),
        in_specs=[pl.BlockSpec(memory_space=pltpu.SMEM), qkv_spec(0), qkv_spec(1), qkv_spec(2), tile_spec],
        out_specs=pl.BlockSpec((None, seq, GROUP_W), lambda b: (b, 0, 0)),
        out_shape=jax.ShapeDtypeStruct((batch, seq, GROUP_W), BF16),
        scratch_shapes=[pltpu.VMEM((N_HEADS_B, 3, B_NQ, B_NK), F32)],
        compiler_params=pltpu.CompilerParams(
            dimension_semantics=("arbitrary",), vmem_limit_bytes=VMEM_LIMIT),
    )(shift, proj3, proj3, proj3, tiles)


def _mix_out_kernel(oa_ref, ob_ref, ga_ref, gb_ref, bg_ref, x_ref, pa_ref, pb_ref, wo_ref, cast_src,
                    h_ref, cast_dst):
    cast_dst[...] = cast_src[...].astype(cast_dst.dtype)
    ya = jnp.dot(oa_ref[...], pa_ref[...], preferred_element_type=F32)
    yb = jnp.dot(ob_ref[...], pb_ref[...], preferred_element_type=F32)
    ga = jax.nn.sigmoid(ga_ref[...].astype(F32) + bg_ref[:, :D_MODEL])
    gb = jax.nn.sigmoid(gb_ref[...].astype(F32) + bg_ref[:, D_MODEL:])
    mixed = (ga * ya + gb * yb).astype(BF16)
    h_ref[...] = x_ref[...] + jnp.dot(mixed, wo_ref[...], preferred_element_type=F32)


def _mix_out(oa, ob, proj, b_gate, x2, pa, pb, wo, side_cast, tm=512):
    n_tok = x2.shape[0]
    row512 = pl.BlockSpec((tm, GROUP_W), lambda i: (i, 0))
    const = pl.Buffered(1)
    cast_spec = _row_chunk_spec(side_cast, n_tok // tm, lambda i: i)
    return pl.pallas_call(
        _mix_out_kernel,
        name="mix_out",
        grid=(n_tok // tm,),
        in_specs=[row512, row512,
                  pl.BlockSpec((tm, D_MODEL), lambda i: (i, GATE_BLOCK0)),
                  pl.BlockSpec((tm, D_MODEL), lambda i: (i, GATE_BLOCK0 + 1)),
                  pl.BlockSpec((1, 2 * D_MODEL), lambda i: (0, 0)),
                  pl.BlockSpec((tm, D_MODEL), lambda i: (i, 0)),
                  pl.BlockSpec((GROUP_W, D_MODEL), lambda i: (0, 0), pipeline_mode=const),
                  pl.BlockSpec((GROUP_W, D_MODEL), lambda i: (0, 0), pipeline_mode=const),
                  pl.BlockSpec((D_MODEL, D_MODEL), lambda i: (0, 0), pipeline_mode=const),
                  cast_spec],
        out_specs=[pl.BlockSpec((tm, D_MODEL), lambda i: (i, 0)), cast_spec],
        out_shape=[jax.ShapeDtypeStruct((n_tok, D_MODEL), F32), jax.ShapeDtypeStruct(side_cast.shape, BF16)],
        compiler_params=pltpu.CompilerParams(
            dimension_semantics=("parallel",), vmem_limit_bytes=VMEM_LIMIT),
    )(oa, ob, proj, proj, b_gate, x2, pa, pb, wo, side_cast)


def _ffn_kernel(h_ref, g_ref, wu_ref, wd_ref, o_ref, hn_ref):
    @pl.when(pl.program_id(1) == 0)
    def _():
        h = h_ref[...]
        hn_ref[...] = _rms(h, g_ref[...]).astype(BF16)
        o_ref[...] = h

    u = jnp.maximum(jnp.dot(hn_ref[...], wu_ref[...], preferred_element_type=F32), 0.0)
    o_ref[...] += jnp.dot((u * u).astype(BF16), wd_ref[...], preferred_element_type=F32)


def _ffn(h, gain, wu, wd, tm=512, tf=2048):
    n_tok = h.shape[0]
    return pl.pallas_call(
        _ffn_kernel,
        name="ffn",
        grid=(n_tok // tm, D_FF // tf),
        in_specs=[pl.BlockSpec((tm, D_MODEL), lambda i, f: (i, 0)),
                  pl.BlockSpec((1, D_MODEL), lambda i, f: (0, 0)),
                  pl.BlockSpec((D_MODEL, tf), lambda i, f: (0, f)),
                  pl.BlockSpec((tf, D_MODEL), lambda i, f: (f, 0))],
        out_specs=pl.BlockSpec((tm, D_MODEL), lambda i, f: (i, 0)),
        out_shape=jax.ShapeDtypeStruct((n_tok, D_MODEL), F32),
        scratch_shapes=[pltpu.VMEM((tm, D_MODEL), BF16)],
        compiler_params=pltpu.CompilerParams(
            dimension_semantics=("parallel", "arbitrary"), vmem_limit_bytes=VMEM_LIMIT),
    )(h, gain, wu, wd)


def kernel(x, norm_mix, w_in, b_gate, q_norm_a, k_norm_a, q_norm_b, k_norm_b, rpb_b,
           w_proj_a, w_proj_b, w_out, norm_ffn, w_up, w_down):
    batch, seq, d_model = x.shape
    assert d_model == D_MODEL and seq % (DILATED_PAIRS[-1][1] * Q_BLK) == 0
    for win, dil in DILATED_PAIRS:
        assert win // (2 * dil) == HALF_WIN
    depth = norm_mix.shape[0]
    cos, sin = _rope_tables(seq)
    h = x.reshape(batch * seq, D_MODEL)
    for l in range(depth):
        gains_a = jnp.stack([q_norm_a[l] * SCALE, k_norm_a[l]]).astype(F32)[:, None, :]
        rope_c = cos[None] * gains_a
        rope_s = sin[None] * jnp.roll(gains_a, HEAD_DIM // 2, axis=-1)
        gains_b = jnp.stack([q_norm_b[l] * SCALE, k_norm_b[l]]).astype(F32)[:, None, :]
        proj, w_up_l = _in_proj(h, norm_mix[l].reshape(1, D_MODEL), w_in[l].astype(BF16),
                                rope_c, rope_s, gains_b, seq, side_casts=(w_up[l],))
        proj3 = proj.reshape(batch, seq, D_PROJ)
        oa = _attn_a(proj3, _score_bound(gains_a[0], gains_a[1]))
        ob = _attn_b(proj3, _neighbourhood_bias_tiles(rpb_b[l]), _score_bound(gains_b[0], gains_b[1]), rpb_b[l])
        h, w_down_l = _mix_out(oa.reshape(batch * seq, GROUP_W), ob.reshape(batch * seq, GROUP_W), proj,
                               b_gate[l].reshape(1, 2 * D_MODEL), h,
                               w_proj_a[l].astype(BF16), w_proj_b[l].astype(BF16), w_out[l].astype(BF16),
                               side_cast=w_down[l])
        h = _ffn(h, norm_ffn[l].reshape(1, D_MODEL), w_up_l, w_down_l)
    return h.reshape(batch, seq, D_MODEL)
```

```python
import functools

import numpy as np
import jax
import jax.numpy as jnp
from jax import lax
from jax.experimental import pallas as pl
from jax.experimental.pallas import tpu as pltpu

D_MODEL = 2048
HEAD_DIM = 128
N_HEADS = 16
N_HEADS_A = 12
N_HEADS_B = 4
DILATED_PAIRS = ((128, 1), (512, 4), (2048, 16))
N_GROUPS_A = len(DILATED_PAIRS)
HEADS_PER_GROUP = 4
GROUP_W = HEADS_PER_GROUP * HEAD_DIM
GRID_W = 64
WIN_R = 8
WIN_C = 16
QKV_W = N_HEADS * HEAD_DIM
D_FF = 4 * D_MODEL
ROPE_THETA = 10000.0
EPS = 1e-6
NEG_INF = -1e30
SCALE = HEAD_DIM ** -0.5

D_PROJ = 3 * QKV_W + 2 * D_MODEL
B_BLOCK0 = N_HEADS_A * HEAD_DIM // GROUP_W
GATE_BLOCK0 = 3 * QKV_W // D_MODEL

VMEM_LIMIT = 56 * 1024 * 1024

F32 = jnp.float32
BF16 = jnp.bfloat16


def _rope_tables(seq):
    pos = np.arange(seq, dtype=np.float64)
    inv = ROPE_THETA ** (-np.arange(0, HEAD_DIM, 2, dtype=np.float64) / HEAD_DIM)
    ang = pos[:, None] * inv[None, :]
    cos = np.concatenate([np.cos(ang), np.cos(ang)], axis=-1)
    sin = np.concatenate([-np.sin(ang), np.sin(ang)], axis=-1)
    return jnp.asarray(cos, F32), jnp.asarray(sin, F32)


def _rms(x, g):
    ms = jnp.mean(x * x, axis=-1, keepdims=True)
    return (x * lax.rsqrt(ms + EPS)) * g


def _in_proj_kernel(*refs, n_casts):
    x_ref, g_ref, w_ref, rc_ref, rs_ref, gb_ref = refs[:6]
    cast_srcs = refs[6:6 + n_casts]
    o_ref = refs[6 + n_casts]
    cast_dsts = refs[7 + n_casts:7 + 2 * n_casts]
    xn_ref = refs[7 + 2 * n_casts]
    j = pl.program_id(1)

    @pl.when(pl.program_id(0) * pl.num_programs(1) + j < IN_PROJ_CAST_CHUNKS)
    def _():
        for src, dst in zip(cast_srcs, cast_dsts):
            dst[...] = src[...].astype(dst.dtype)

    def project(xn):
        return jnp.dot(xn, w_ref[...], preferred_element_type=F32)

    def qk_tile(xn):
        acc = project(xn)
        for h in range(N_HEADS):
            sl = slice(h * HEAD_DIM, (h + 1) * HEAD_DIM)
            y = acc[:, sl]
            inv_rms = lax.rsqrt(jnp.mean(y * y, axis=-1, keepdims=True) + EPS)
            if h < N_HEADS_A:
                y = y * rc_ref[...] + pltpu.roll(y, HEAD_DIM // 2, 1) * rs_ref[...]
            else:
                y = y * gb_ref[...]
            o_ref[:, sl] = (y * inv_rms).astype(o_ref.dtype)

    @pl.when(j == 0)
    def _():
        xn = _rms(x_ref[...], g_ref[...]).astype(BF16)
        xn_ref[...] = xn
        qk_tile(xn)

    @pl.when(j == 1)
    def _():
        qk_tile(xn_ref[...])

    @pl.when(j >= 2)
    def _():
        o_ref[...] = project(xn_ref[...]).astype(o_ref.dtype)


def _row_chunk_spec(a, n_chunks, step):
    assert a.shape[0] % (n_chunks * 16) == 0
    return pl.BlockSpec((a.shape[0] // n_chunks, a.shape[1]),
                        lambda *ids: (jnp.minimum(step(*ids), n_chunks - 1), 0))


IN_PROJ_CAST_CHUNKS = 64


def _in_proj(x2, gain, w, rope_c, rope_s, gain_b, seq, side_casts, tm=1024):
    n_tok = x2.shape[0]
    tn = QKV_W
    assert seq % tm == 0 and D_PROJ % tn == 0
    tiles_per_seq = seq // tm
    n_i, n_j = n_tok // tm, D_PROJ // tn
    assert n_i * n_j >= IN_PROJ_CAST_CHUNKS

    def qk(j):
        return jnp.minimum(j, 1)

    cast_specs = [_row_chunk_spec(a, IN_PROJ_CAST_CHUNKS, lambda i, j: i * n_j + j) for a in side_casts]
    return pl.pallas_call(
        functools.partial(_in_proj_kernel, n_casts=len(side_casts)),
        name="in_proj",
        grid=(n_i, n_j),
        in_specs=[
            pl.BlockSpec((tm, D_MODEL), lambda i, j: (i, 0)),
            pl.BlockSpec((1, D_MODEL), lambda i, j: (0, 0)),
            pl.BlockSpec((D_MODEL, tn), lambda i, j: (0, j)),
            pl.BlockSpec((None, tm, HEAD_DIM), lambda i, j: (qk(j), i % tiles_per_seq, 0)),
            pl.BlockSpec((None, tm, HEAD_DIM), lambda i, j: (qk(j), i % tiles_per_seq, 0)),
            pl.BlockSpec((None, 1, HEAD_DIM), lambda i, j: (qk(j), 0, 0)),
        ] + cast_specs,
        out_specs=[pl.BlockSpec((tm, tn), lambda i, j: (i, j))] + cast_specs,
        out_shape=[jax.ShapeDtypeStruct((n_tok, D_PROJ), BF16)]
        + [jax.ShapeDtypeStruct(a.shape, BF16) for a in side_casts],
        scratch_shapes=[pltpu.VMEM((tm, D_MODEL), BF16)],
        compiler_params=pltpu.CompilerParams(
            dimension_semantics=("arbitrary", "arbitrary"), vmem_limit_bytes=VMEM_LIMIT),
    )(x2, gain, w, rope_c, rope_s, gain_b, *side_casts)


Q_BLK = 128
HALF_WIN = 64
A_WIN = 2 * Q_BLK
EXP_SPAN_LIMIT = 80.0


def _band_masks():
    rel = np.arange(A_WIN)[None, :] - np.arange(Q_BLK)[:, None]
    tiles = [np.where(np.abs(rel - off) <= HALF_WIN, 0.0, NEG_INF) for off in (0, HALF_WIN, 2 * HALF_WIN)]
    return jnp.asarray(np.stack(tiles), F32)


def _attn_a_kernel(*refs, seq, dils):
    n_g = len(dils)
    qkv_refs = refs[:3 * n_g]
    bound_ref, mask_ref, o_ref, qn_ref, kn_ref, vn_ref, f_ref, og_ref, lg_ref, band_ref = refs[3 * n_g:]
    n_blk = seq // Q_BLK
    bound = bound_ref[0]
    band_ref[...] = mask_ref[...] - bound
    vn_ref[:, :, HEAD_DIM:] = jnp.ones((n_g, seq, HEAD_DIM), BF16)

    def first_stride(dil):
        return 4 if dil % 8 == 0 else dil

    slabs = iter(range(f_ref.shape[0]))
    stage = {g: tuple(f_ref.at[next(slabs)] for _ in range(3)) for g, dil in enumerate(dils) if dil > 1}
    mid = {g: tuple(f_ref.at[next(slabs)] for _ in range(3))
           for g, dil in enumerate(dils) if first_stride(dil) != dil}

    def block_rows(n):
        return pl.ds(pl.multiple_of(n * Q_BLK, Q_BLK), Q_BLK)

    def token_rows(n, dil):
        if dil == 1:
            return block_rows(n)
        sub_len = seq // dil
        base = n * Q_BLK
        r = base // sub_len
        return pl.ds(r + (base - r * sub_len) * dil, Q_BLK, stride=dil)

    def stage_in(n, c):
        rows = block_rows(n)
        for g, dil in enumerate(dils):
            q_ref, k_ref, v_ref = qkv_refs[3 * g:3 * g + 3]
            if dil == 1:
                vn_ref[g, rows, :HEAD_DIM] = v_ref[rows, :]
            else:
                fq, fk, fv = stage[g]
                fq[rows, :] = q_ref[rows, :].astype(F32)
                fk[rows, :] = k_ref[rows, :].astype(F32)
                fv[rows, :] = v_ref[rows, :].astype(F32)
        return c

    lax.fori_loop(0, n_blk, stage_in, 0, unroll=2)

    def put_operands(g, dst, q, k, v):
        qn_ref[g, dst, :] = q.astype(BF16)
        kn_ref[g, dst, :] = k.astype(BF16)
        vn_ref[g, dst, :HEAD_DIM] = v.astype(BF16)

    def gather(n, c):
        dst = block_rows(n)
        for g, dil in enumerate(dils):
            if dil > 1:
                src = token_rows(n, first_stride(dil))
                q, k, v = (slab[src, :] for slab in stage[g])
                if g in mid:
                    for slab, x in zip(mid[g], (q, k, v)):
                        slab[dst, :] = x
                else:
                    put_operands(g, dst, q, k, v)
        return c

    lax.fori_loop(0, n_blk, gather, 0, unroll=2)

    def second_step_rows(n, g):
        dil = dils[g]
        s1 = first_stride(dil)
        sub_len = seq // dil
        base = n * Q_BLK
        r = base // sub_len
        start = (r % s1) * (seq // s1) + r // s1 + (base - r * sub_len) * (dil // s1)
        return pl.ds(start, Q_BLK, stride=dil // s1)

    def gather_second(n, c):
        for g in mid:
            put_operands(g, block_rows(n), *(slab[second_step_rows(n, g), :] for slab in mid[g]))
        return c

    if mid:
        lax.fori_loop(0, n_blk, gather_second, 0, unroll=2)

    def block(n, c, row_max):
        q0 = pl.multiple_of(n * Q_BLK, Q_BLK)
        for g, dil in enumerate(dils):
            sub_len = seq // dil
            win = min(A_WIN, sub_len)
            sub0 = (n // (sub_len // Q_BLK)) * sub_len
            q_loc = q0 - sub0
            k_loc = jnp.clip(q_loc - HALF_WIN, 0, sub_len - win)
            k0 = pl.multiple_of(sub0 + k_loc, HALF_WIN)
            band = band_ref[(q_loc - k_loc) // HALF_WIN, :, :win]
            q_src, k_src = (qkv_refs[3 * g], qkv_refs[3 * g + 1]) if dil == 1 else (qn_ref.at[g], kn_ref.at[g])
            q = q_src[pl.ds(q0, Q_BLK), :]
            k = k_src[pl.ds(k0, win), :]
            s = lax.dot_general(q, k, (((1,), (1,)), ((), ())), preferred_element_type=F32) + band
            if row_max:
                m = jnp.max(s, axis=-1, keepdims=True)
                s = s - m
            p = jnp.exp(s).astype(BF16)
            acc = jnp.dot(p, vn_ref[g, pl.ds(k0, win), :], preferred_element_type=F32)
            num, denom = acc[:, :HEAD_DIM], acc[:, HEAD_DIM:]
            if row_max:
                first, second = num * (1.0 / denom), jnp.log(denom) + m
            else:
                first, second = num, denom
            if g in mid:
                mid[g][0][pl.ds(q0, Q_BLK), :] = first
                mid[g][1][pl.ds(q0, Q_BLK), :] = second
            else:
                dst = token_rows(n, dil)
                og_ref.at[g][dst, :] = first
                lg_ref.at[g][dst, :] = second
        return c

    def scatter_first(n, c):
        for g in mid:
            dst = second_step_rows(n, g)
            stage[g][0][dst, :] = mid[g][0][block_rows(n), :]
            stage[g][1][dst, :] = mid[g][1][block_rows(n), :]
        return c

    def scatter_second(n, c):
        for g in mid:
            dst = token_rows(n, first_stride(dils[g]))
            og_ref.at[g][dst, :] = stage[g][0][block_rows(n), :]
            lg_ref.at[g][dst, :] = stage[g][1][block_rows(n), :]
        return c

    def combine(n, c, row_max):
        rows = block_rows(n)
        if row_max:
            lses = [lg_ref[g, rows, :] for g in range(n_g)]
            mx = functools.reduce(jnp.maximum, lses)
            es = [jnp.exp(x - mx) for x in lses]
            inv = 1.0 / functools.reduce(lambda a, b: a + b, es)
            oa = (es[0] * inv) * og_ref[0, rows, :]
            for g in range(1, n_g):
                oa = oa + (es[g] * inv) * og_ref[g, rows, :]
        else:
            num = functools.reduce(lambda a, b: a + b, [og_ref[g, rows, :] for g in range(n_g)])
            den = functools.reduce(lambda a, b: a + b, [lg_ref[g, rows, :] for g in range(n_g)])
            oa = num * (1.0 / den)
        o_ref[rows, :] = oa.astype(o_ref.dtype)
        return c

    def attend(row_max, unroll):
        lax.fori_loop(0, n_blk, functools.partial(block, row_max=row_max), 0, unroll=unroll)
        if mid:
            lax.fori_loop(0, n_blk, scatter_first, 0, unroll=2)
            lax.fori_loop(0, n_blk, scatter_second, 0, unroll=2)
        lax.fori_loop(0, n_blk, functools.partial(combine, row_max=row_max), 0, unroll=2)

    @pl.when(2.0 * bound < EXP_SPAN_LIMIT)
    def _():
        attend(row_max=False, unroll=8)

    @pl.when(2.0 * bound >= EXP_SPAN_LIMIT)
    def _():
        attend(row_max=True, unroll=2)


def _score_bound(gq, gk):
    return 1.01 * HEAD_DIM * jnp.max(jnp.abs(gq)) * jnp.max(jnp.abs(gk))


def _attn_a(proj3, bound):
    batch, seq, _ = proj3.shape
    dils = tuple(d for _, d in DILATED_PAIRS)
    n_g = len(dils)
    for d in dils:
        assert d % 8 != 0 or (d % 4 == 0 and (d // 4) % 8 != 0)
    n_slabs = 3 * sum((d > 1) + (d % 8 == 0) for d in dils)
    masks = _band_masks()
    qkv_specs = [pl.BlockSpec((None, seq, HEAD_DIM),
                              lambda b, s, t=t, g=g: (b, 0, t * N_HEADS + g * HEADS_PER_GROUP + s))
                 for g in range(n_g) for t in range(3)]
    return pl.pallas_call(
        functools.partial(_attn_a_kernel, seq=seq, dils=dils),
        name="attn_a",
        grid=(batch, HEADS_PER_GROUP),
        in_specs=qkv_specs + [pl.BlockSpec(memory_space=pltpu.SMEM),
                              pl.BlockSpec(masks.shape, lambda b, s: (0, 0, 0), pipeline_mode=pl.Buffered(1))],
        out_specs=pl.BlockSpec((None, seq, HEAD_DIM), lambda b, s: (b, 0, s)),
        out_shape=jax.ShapeDtypeStruct((batch, seq, GROUP_W), BF16),
        scratch_shapes=[pltpu.VMEM((n_g, seq, HEAD_DIM), BF16)] * 2
        + [pltpu.VMEM((n_g, seq, 2 * HEAD_DIM), BF16)]
        + [pltpu.VMEM((n_slabs, seq, HEAD_DIM), F32)]
        + [pltpu.VMEM((n_g, seq, HEAD_DIM), F32)] * 2
        + [pltpu.VMEM(masks.shape, F32)],
        compiler_params=pltpu.CompilerParams(
            dimension_semantics=("parallel", "arbitrary"), vmem_limit_bytes=VMEM_LIMIT),
    )(*([proj3] * (3 * n_g)), bound.reshape(1).astype(F32), masks)


B_QROWS = 4
B_KROWS = B_QROWS + WIN_R
B_NQ = B_QROWS * GRID_W
B_NK = B_KROWS * GRID_W


def _window_row_offsets(rows):
    n_grp = rows // B_QROWS
    assert rows % B_QROWS == 0 and n_grp >= 3
    masked = 2 * WIN_R - 1
    table = []
    for i in (0, 1, n_grp - 1):
        r0 = i * B_QROWS
        ws = int(np.clip(r0 - WIN_R // 2, 0, rows - B_KROWS))
        per_q = []
        for rq in range(B_QROWS):
            r = r0 + rq
            rs = int(np.clip(r - WIN_R // 2, 0, rows - WIN_R))
            per_q.append([ws + jr - r + WIN_R - 1 if rs <= ws + jr < rs + WIN_R else masked
                          for jr in range(B_KROWS)])
        table.append(per_q)
    return table


def _attn_b_kernel(shift_ref, q_ref, k_ref, v_ref, tile_ref, o_ref, bias_ref, *, rows):
    n_grp = rows // B_QROWS
    shift = shift_ref[0]
    span = shift_ref[1]

    @pl.when(pl.program_id(0) == 0)
    def _():
        right = lax.broadcasted_iota(jnp.int32, (GRID_W, 2 * GRID_W), 1) >= GRID_W
        for case, per_q in enumerate(_window_row_offsets(rows)):
            for rq, offs in enumerate(per_q):
                for h in range(N_HEADS_B):
                    for j in range(0, B_KROWS, 2):
                        pair = jnp.where(right, tile_ref[h, offs[j + 1]], tile_ref[h, offs[j]])
                        bias_ref[h, case, rq * GRID_W:(rq + 1) * GRID_W,
                                 j * GRID_W:(j + 2) * GRID_W] = pair - shift

    def group(i, c, row_max):
        r0 = i * B_QROWS
        ws = jnp.clip(r0 - WIN_R // 2, 0, rows - B_KROWS)
        case = jnp.where(i == 0, 0, jnp.where(i == n_grp - 1, 2, 1))
        q0 = pl.multiple_of(r0 * GRID_W, B_NQ)
        k0 = pl.multiple_of(ws * GRID_W, GRID_W)
        for h in range(N_HEADS_B):
            sl = slice(h * HEAD_DIM, (h + 1) * HEAD_DIM)
            q = q_ref[pl.ds(q0, B_NQ), sl]
            k = k_ref[pl.ds(k0, B_NK), sl]
            v = v_ref[pl.ds(k0, B_NK), sl]
            s = lax.dot_general(q, k, (((1,), (1,)), ((), ())), preferred_element_type=F32)
            s = s + bias_ref[h, case]
            if row_max:
                s = s - jnp.max(s, axis=-1, keepdims=True)
            p = jnp.exp(s)
            l = jnp.sum(p, axis=-1, keepdims=True)
            acc = jnp.dot(p.astype(BF16), v, preferred_element_type=F32)
            o_ref[pl.ds(q0, B_NQ), sl] = (acc * (1.0 / l)).astype(o_ref.dtype)
        return c

    @pl.when(span < EXP_SPAN_LIMIT)
    def _():
        lax.fori_loop(0, n_grp, functools.partial(group, row_max=False), 0)

    @pl.when(span >= EXP_SPAN_LIMIT)
    def _():
        lax.fori_loop(0, n_grp, functools.partial(group, row_max=True), 0)


def _neighbourhood_bias_tiles(rpb):
    n_h = rpb.shape[0]
    c = np.arange(GRID_W)
    dc = np.clip(c[None, :] - c[:, None], -(WIN_C - 1), WIN_C - 1) + (WIN_C - 1)
    col_start = np.clip(c - WIN_C // 2, 0, GRID_W - WIN_C)
    col_ok = (c[None, :] >= col_start[:, None]) & (c[None, :] < col_start[:, None] + WIN_C)
    col_pick = np.zeros((2 * WIN_C - 1, GRID_W * GRID_W), np.float32)
    col_pick[dc.ravel(), np.arange(GRID_W * GRID_W)] = 1.0
    tiles = jnp.einsum("hab,bc->hac", rpb.astype(F32), col_pick, precision=lax.Precision.HIGHEST)
    tiles = jnp.where(col_ok.reshape(-1)[None, None], tiles, NEG_INF).reshape(n_h, -1, GRID_W, GRID_W)
    tiles = jnp.concatenate([tiles, jnp.full((n_h, 1, GRID_W, GRID_W), NEG_INF, F32)], axis=1)
    return jnp.concatenate([tiles, tiles], axis=-1)


def _attn_b(proj3, tiles, qk_bound, rpb):
    batch, seq, _ = proj3.shape
    rows = seq // GRID_W
    hi, lo = jnp.max(rpb).astype(F32), jnp.min(rpb).astype(F32)
    shift = jnp.stack([qk_bound + hi, 2.0 * qk_bound + (hi - lo)]).astype(F32)

    def qkv_spec(t):
        return pl.BlockSpec((None, seq, GROUP_W), lambda b: (b, 0, t * (QKV_W // GROUP_W) + B_BLOCK0))

    tile_spec = pl.BlockSpec(tiles.shape, lambda b: (0, 0, 0, 0), pipeline_mode=pl.Buffered(1))
    return pl.pallas_call(
        functools.partial(_attn_b_kernel, rows=rows),
        name="attn_b",
        grid=(batch,),
        in_specs=[pl.BlockSpec(memory_space=pltpu.SMEM), qkv_spec(0), qkv_spec(1), qkv_spec(2), tile_spec],
        out_specs=pl.BlockSpec((None, seq, GROUP_W), lambda b: (b, 0, 0)),
        out_shape=jax.ShapeDtypeStruct((batch, seq, GROUP_W), BF16),
        scratch_shapes=[pltpu.VMEM((N_HEADS_B, 3, B_NQ, B_NK), F32)],
        compiler_params=pltpu.CompilerParams(
            dimension_semantics=("arbitrary",), vmem_limit_bytes=VMEM_LIMIT),
    )(shift, proj3, proj3, proj3, tiles)


def _mix_out_kernel(oa_ref, ob_ref, ga_ref, gb_ref, bg_ref, x_ref, pa_ref, pb_ref, wo_ref, cast_src,
                    h_ref, cast_dst):
    cast_dst[...] = cast_src[...].astype(cast_dst.dtype)
    ya = jnp.dot(oa_ref[...], pa_ref[...], preferred_element_type=F32)
    yb = jnp.dot(ob_ref[...], pb_ref[...], preferred_element_type=F32)
    ga = jax.nn.sigmoid(ga_ref[...].astype(F32) + bg_ref[:, :D_MODEL])
    gb = jax.nn.sigmoid(gb_ref[...].astype(F32) + bg_ref[:, D_MODEL:])
    mixed = (ga * ya + gb * yb).astype(BF16)
    h_ref[...] = x_ref[...] + jnp.dot(mixed, wo_ref[...], preferred_element_type=F32)


def _mix_out(oa, ob, proj, b_gate, x2, pa, pb, wo, side_cast, tm=512):
    n_tok = x2.shape[0]
    row512 = pl.BlockSpec((tm, GROUP_W), lambda i: (i, 0))
    const = pl.Buffered(1)
    cast_spec = _row_chunk_spec(side_cast, n_tok // tm, lambda i: i)
    return pl.pallas_call(
        _mix_out_kernel,
        name="mix_out",
        grid=(n_tok // tm,),
        in_specs=[row512, row512,
                  pl.BlockSpec((tm, D_MODEL), lambda i: (i, GATE_BLOCK0)),
                  pl.BlockSpec((tm, D_MODEL), lambda i: (i, GATE_BLOCK0 + 1)),
                  pl.BlockSpec((1, 2 * D_MODEL), lambda i: (0, 0)),
                  pl.BlockSpec((tm, D_MODEL), lambda i: (i, 0)),
                  pl.BlockSpec((GROUP_W, D_MODEL), lambda i: (0, 0), pipeline_mode=const),
                  pl.BlockSpec((GROUP_W, D_MODEL), lambda i: (0, 0), pipeline_mode=const),
                  pl.BlockSpec((D_MODEL, D_MODEL), lambda i: (0, 0), pipeline_mode=const),
                  cast_spec],
        out_specs=[pl.BlockSpec((tm, D_MODEL), lambda i: (i, 0)), cast_spec],
        out_shape=[jax.ShapeDtypeStruct((n_tok, D_MODEL), F32), jax.ShapeDtypeStruct(side_cast.shape, BF16)],
        compiler_params=pltpu.CompilerParams(
            dimension_semantics=("parallel",), vmem_limit_bytes=VMEM_LIMIT),
    )(oa, ob, proj, proj, b_gate, x2, pa, pb, wo, side_cast)


def _ffn_kernel(h_ref, g_ref, wu_ref, wd_ref, o_ref, hn_ref):
    def partial_out(hn):
        u = jnp.maximum(jnp.dot(hn, wu_ref[...], preferred_element_type=F32), 0.0)
        return jnp.dot((u * u).astype(BF16), wd_ref[...], preferred_element_type=F32)

    @pl.when(pl.program_id(1) == 0)
    def _():
        h = h_ref[...]
        hn = _rms(h, g_ref[...]).astype(BF16)
        hn_ref[...] = hn
        o_ref[...] = h + partial_out(hn)

    @pl.when(pl.program_id(1) > 0)
    def _():
        o_ref[...] += partial_out(hn_ref[...])


def _ffn(h, gain, wu, wd, tm=512, tf=2048):
    n_tok = h.shape[0]
    return pl.pallas_call(
        _ffn_kernel,
        name="ffn",
        grid=(n_tok // tm, D_FF // tf),
        in_specs=[pl.BlockSpec((tm, D_MODEL), lambda i, f: (i, 0)),
                  pl.BlockSpec((1, D_MODEL), lambda i, f: (0, 0)),
                  pl.BlockSpec((D_MODEL, tf), lambda i, f: (0, f)),
                  pl.BlockSpec((tf, D_MODEL), lambda i, f: (f, 0))],
        out_specs=pl.BlockSpec((tm, D_MODEL), lambda i, f: (i, 0)),
        out_shape=jax.ShapeDtypeStruct((n_tok, D_MODEL), F32),
        scratch_shapes=[pltpu.VMEM((tm, D_MODEL), BF16)],
        compiler_params=pltpu.CompilerParams(
            dimension_semantics=("parallel", "arbitrary"), vmem_limit_bytes=VMEM_LIMIT),
    )(h, gain, wu, wd)


def kernel(x, norm_mix, w_in, b_gate, q_norm_a, k_norm_a, q_norm_b, k_norm_b, rpb_b,
           w_proj_a, w_proj_b, w_out, norm_ffn, w_up, w_down):
    batch, seq, d_model = x.shape
    assert d_model == D_MODEL and seq % (DILATED_PAIRS[-1][1] * Q_BLK) == 0
    for win, dil in DILATED_PAIRS:
        assert win // (2 * dil) == HALF_WIN
    depth = norm_mix.shape[0]
    cos, sin = _rope_tables(seq)
    h = x.reshape(batch * seq, D_MODEL)
    for l in range(depth):
        gains_a = jnp.stack([q_norm_a[l] * SCALE, k_norm_a[l]]).astype(F32)[:, None, :]
        rope_c = cos[None] * gains_a
        rope_s = sin[None] * jnp.roll(gains_a, HEAD_DIM // 2, axis=-1)
        gains_b = jnp.stack([q_norm_b[l] * SCALE, k_norm_b[l]]).astype(F32)[:, None, :]
        proj, w_up_l = _in_proj(h, norm_mix[l].reshape(1, D_MODEL), w_in[l].astype(BF16),
                                rope_c, rope_s, gains_b, seq, side_casts=(w_up[l],))
        proj3 = proj.reshape(batch, seq, D_PROJ)
        oa = _attn_a(proj3, _score_bound(gains_a[0], gains_a[1]))
        ob = _attn_b(proj3, _neighbourhood_bias_tiles(rpb_b[l]), _score_bound(gains_b[0], gains_b[1]), rpb_b[l])
        h, w_down_l = _mix_out(oa.reshape(batch * seq, GROUP_W), ob.reshape(batch * seq, GROUP_W), proj,
                               b_gate[l].reshape(1, 2 * D_MODEL), h,
                               w_proj_a[l].astype(BF16), w_proj_b[l].astype(BF16), w_out[l].astype(BF16),
                               side_cast=w_down[l])
        h = _ffn(h, norm_ffn[l].reshape(1, D_MODEL), w_up_l, w_down_l)
    return h.reshape(batch, seq, D_MODEL)
```

```python
import functools

import numpy as np
import jax
import jax.numpy as jnp
from jax import lax
from jax.experimental import pallas as pl
from jax.experimental.pallas import tpu as pltpu

D_MODEL = 2048
HEAD_DIM = 128
N_HEADS = 16
N_HEADS_A = 12
N_HEADS_B = 4
DILATED_PAIRS = ((128, 1), (512, 4), (2048, 16))
N_GROUPS_A = len(DILATED_PAIRS)
HEADS_PER_GROUP = 4
GROUP_W = HEADS_PER_GROUP * HEAD_DIM
GRID_W = 64
WIN_R = 8
WIN_C = 16
QKV_W = N_HEADS * HEAD_DIM
D_FF = 4 * D_MODEL
ROPE_THETA = 10000.0
EPS = 1e-6
NEG_INF = -1e30
SCALE = HEAD_DIM ** -0.5

D_PROJ = 3 * QKV_W + 2 * D_MODEL
B_BLOCK0 = N_HEADS_A * HEAD_DIM // GROUP_W
GATE_BLOCK0 = 3 * QKV_W // D_MODEL

VMEM_LIMIT = 56 * 1024 * 1024

F32 = jnp.float32
BF16 = jnp.bfloat16


def _rope_tables(seq):
    pos = np.arange(seq, dtype=np.float64)
    inv = ROPE_THETA ** (-np.arange(0, HEAD_DIM, 2, dtype=np.float64) / HEAD_DIM)
    ang = pos[:, None] * inv[None, :]
    cos = np.concatenate([np.cos(ang), np.cos(ang)], axis=-1)
    sin = np.concatenate([-np.sin(ang), np.sin(ang)], axis=-1)
    return jnp.asarray(cos, F32), jnp.asarray(sin, F32)


def _rms(x, g):
    ms = jnp.mean(x * x, axis=-1, keepdims=True)
    return (x * lax.rsqrt(ms + EPS)) * g


def _in_proj_kernel(*refs, n_casts):
    x_ref, g_ref, w_ref, rc_ref, rs_ref, gb_ref = refs[:6]
    cast_srcs = refs[6:6 + n_casts]
    o_ref = refs[6 + n_casts]
    cast_dsts = refs[7 + n_casts:7 + 2 * n_casts]
    xn_ref = refs[7 + 2 * n_casts]
    j = pl.program_id(1)

    @pl.when(pl.program_id(0) * pl.num_programs(1) + j < IN_PROJ_CAST_CHUNKS)
    def _():
        for src, dst in zip(cast_srcs, cast_dsts):
            dst[...] = src[...].astype(dst.dtype)

    def project(xn):
        return jnp.dot(xn, w_ref[...], preferred_element_type=F32)

    def qk_tile(xn):
        acc = project(xn)
        for h in range(N_HEADS):
            sl = slice(h * HEAD_DIM, (h + 1) * HEAD_DIM)
            y = acc[:, sl]
            inv_rms = lax.rsqrt(jnp.mean(y * y, axis=-1, keepdims=True) + EPS)
            if h < N_HEADS_A:
                y = y * rc_ref[...] + pltpu.roll(y, HEAD_DIM // 2, 1) * rs_ref[...]
            else:
                y = y * gb_ref[...]
            o_ref[:, sl] = (y * inv_rms).astype(o_ref.dtype)

    @pl.when(j == 0)
    def _():
        xn = _rms(x_ref[...], g_ref[...]).astype(BF16)
        xn_ref[...] = xn
        qk_tile(xn)

    @pl.when(j == 1)
    def _():
        qk_tile(xn_ref[...])

    @pl.when(j >= 2)
    def _():
        o_ref[...] = project(xn_ref[...]).astype(o_ref.dtype)


def _row_chunk_spec(a, n_chunks, step):
    assert a.shape[0] % (n_chunks * 16) == 0
    return pl.BlockSpec((a.shape[0] // n_chunks, a.shape[1]),
                        lambda *ids: (jnp.minimum(step(*ids), n_chunks - 1), 0))


IN_PROJ_CAST_CHUNKS = 64


def _in_proj(x2, gain, w, rope_c, rope_s, gain_b, seq, side_casts, tm=1024):
    n_tok = x2.shape[0]
    tn = QKV_W
    assert seq % tm == 0 and D_PROJ % tn == 0
    tiles_per_seq = seq // tm
    n_i, n_j = n_tok // tm, D_PROJ // tn
    assert n_i * n_j >= IN_PROJ_CAST_CHUNKS

    def qk(j):
        return jnp.minimum(j, 1)

    cast_specs = [_row_chunk_spec(a, IN_PROJ_CAST_CHUNKS, lambda i, j: i * n_j + j) for a in side_casts]
    return pl.pallas_call(
        functools.partial(_in_proj_kernel, n_casts=len(side_casts)),
        name="in_proj",
        grid=(n_i, n_j),
        in_specs=[
            pl.BlockSpec((tm, D_MODEL), lambda i, j: (i, 0)),
            pl.BlockSpec((1, D_MODEL), lambda i, j: (0, 0)),
            pl.BlockSpec((D_MODEL, tn), lambda i, j: (0, j)),
            pl.BlockSpec((None, tm, HEAD_DIM), lambda i, j: (qk(j), i % tiles_per_seq, 0)),
            pl.BlockSpec((None, tm, HEAD_DIM), lambda i, j: (qk(j), i % tiles_per_seq, 0)),
            pl.BlockSpec((None, 1, HEAD_DIM), lambda i, j: (qk(j), 0, 0)),
        ] + cast_specs,
        out_specs=[pl.BlockSpec((tm, tn), lambda i, j: (i, j))] + cast_specs,
        out_shape=[jax.ShapeDtypeStruct((n_tok, D_PROJ), BF16)]
        + [jax.ShapeDtypeStruct(a.shape, BF16) for a in side_casts],
        scratch_shapes=[pltpu.VMEM((tm, D_MODEL), BF16)],
        compiler_params=pltpu.CompilerParams(
            dimension_semantics=("arbitrary", "arbitrary"), vmem_limit_bytes=VMEM_LIMIT),
    )(x2, gain, w, rope_c, rope_s, gain_b, *side_casts)


Q_BLK = 128
HALF_WIN = 64
A_WIN = 2 * Q_BLK
EXP_SPAN_LIMIT = 80.0


def _band_masks():
    rel = np.arange(A_WIN)[None, :] - np.arange(Q_BLK)[:, None]
    tiles = [np.where(np.abs(rel - off) <= HALF_WIN, 0.0, NEG_INF) for off in (0, HALF_WIN, 2 * HALF_WIN)]
    return jnp.asarray(np.stack(tiles), F32)


def _attn_a_kernel(*refs, seq, dils, n_casts):
    n_g = len(dils)
    qkv_refs = refs[:3 * n_g]
    bound_ref, mask_ref = refs[3 * n_g:3 * n_g + 2]
    rest = refs[3 * n_g + 2:]
    cast_srcs, o_ref, cast_dsts = rest[:n_casts], rest[n_casts], rest[n_casts + 1:2 * n_casts + 1]
    qn_ref, kn_ref, vn_ref, f_ref, og_ref, lg_ref, band_ref = rest[2 * n_casts + 1:]
    n_blk = seq // Q_BLK
    for src, dst in zip(cast_srcs, cast_dsts):
        dst[...] = src[...].astype(dst.dtype)
    bound = bound_ref[0]
    band_ref[...] = mask_ref[...] - bound
    vn_ref[:, :, HEAD_DIM:] = jnp.ones((n_g, seq, HEAD_DIM), BF16)

    def first_stride(dil):
        return 4 if dil % 8 == 0 else dil

    slabs = iter(range(f_ref.shape[0]))
    stage = {g: tuple(f_ref.at[next(slabs)] for _ in range(3)) for g, dil in enumerate(dils) if dil > 1}
    mid = {g: tuple(f_ref.at[next(slabs)] for _ in range(3))
           for g, dil in enumerate(dils) if first_stride(dil) != dil}

    def block_rows(n):
        return pl.ds(pl.multiple_of(n * Q_BLK, Q_BLK), Q_BLK)

    def token_rows(n, dil):
        if dil == 1:
            return block_rows(n)
        sub_len = seq // dil
        base = n * Q_BLK
        r = base // sub_len
        return pl.ds(r + (base - r * sub_len) * dil, Q_BLK, stride=dil)

    def stage_in(n, c):
        rows = block_rows(n)
        for g, dil in enumerate(dils):
            q_ref, k_ref, v_ref = qkv_refs[3 * g:3 * g + 3]
            if dil == 1:
                vn_ref[g, rows, :HEAD_DIM] = v_ref[rows, :]
            else:
                fq, fk, fv = stage[g]
                fq[rows, :] = q_ref[rows, :].astype(F32)
                fk[rows, :] = k_ref[rows, :].astype(F32)
                fv[rows, :] = v_ref[rows, :].astype(F32)
        return c

    lax.fori_loop(0, n_blk, stage_in, 0, unroll=2)

    def put_operands(g, dst, q, k, v):
        qn_ref[g, dst, :] = q.astype(BF16)
        kn_ref[g, dst, :] = k.astype(BF16)
        vn_ref[g, dst, :HEAD_DIM] = v.astype(BF16)

    def gather(n, c):
        dst = block_rows(n)
        for g, dil in enumerate(dils):
            if dil > 1:
                src = token_rows(n, first_stride(dil))
                q, k, v = (slab[src, :] for slab in stage[g])
                if g in mid:
                    for slab, x in zip(mid[g], (q, k, v)):
                        slab[dst, :] = x
                else:
                    put_operands(g, dst, q, k, v)
        return c

    lax.fori_loop(0, n_blk, gather, 0, unroll=2)

    def second_step_rows(n, g):
        dil = dils[g]
        s1 = first_stride(dil)
        sub_len = seq // dil
        base = n * Q_BLK
        r = base // sub_len
        start = (r % s1) * (seq // s1) + r // s1 + (base - r * sub_len) * (dil // s1)
        return pl.ds(start, Q_BLK, stride=dil // s1)

    def gather_second(n, c):
        for g in mid:
            put_operands(g, block_rows(n), *(slab[second_step_rows(n, g), :] for slab in mid[g]))
        return c

    if mid:
        lax.fori_loop(0, n_blk, gather_second, 0, unroll=2)

    def block(n, c, row_max):
        q0 = pl.multiple_of(n * Q_BLK, Q_BLK)
        for g, dil in enumerate(dils):
            sub_len = seq // dil
            win = min(A_WIN, sub_len)
            sub0 = (n // (sub_len // Q_BLK)) * sub_len
            q_loc = q0 - sub0
            k_loc = jnp.clip(q_loc - HALF_WIN, 0, sub_len - win)
            k0 = pl.multiple_of(sub0 + k_loc, HALF_WIN)
            band = band_ref[(q_loc - k_loc) // HALF_WIN, :, :win]
            q_src, k_src = (qkv_refs[3 * g], qkv_refs[3 * g + 1]) if dil == 1 else (qn_ref.at[g], kn_ref.at[g])
            q = q_src[pl.ds(q0, Q_BLK), :]
            k = k_src[pl.ds(k0, win), :]
            s = lax.dot_general(q, k, (((1,), (1,)), ((), ())), preferred_element_type=F32) + band
            if row_max:
                m = jnp.max(s, axis=-1, keepdims=True)
                s = s - m
            p = jnp.exp(s).astype(BF16)
            acc = jnp.dot(p, vn_ref[g, pl.ds(k0, win), :], preferred_element_type=F32)
            num, denom = acc[:, :HEAD_DIM], acc[:, HEAD_DIM:]
            if row_max:
                first, second = num * (1.0 / denom), jnp.log(denom) + m
            else:
                first, second = num, denom
            if g in mid:
                mid[g][0][pl.ds(q0, Q_BLK), :] = first
                mid[g][1][pl.ds(q0, Q_BLK), :] = second
            else:
                dst = token_rows(n, dil)
                og_ref.at[g][dst, :] = first
                lg_ref.at[g][dst, :] = second
        return c

    def scatter_first(n, c):
        for g in mid:
            dst = second_step_rows(n, g)
            stage[g][0][dst, :] = mid[g][0][block_rows(n), :]
            stage[g][1][dst, :] = mid[g][1][block_rows(n), :]
        return c

    def scatter_second(n, c):
        for g in mid:
            dst = token_rows(n, first_stride(dils[g]))
            og_ref.at[g][dst, :] = stage[g][0][block_rows(n), :]
            lg_ref.at[g][dst, :] = stage[g][1][block_rows(n), :]
        return c

    def combine(n, c, row_max):
        rows = block_rows(n)
        if row_max:
            lses = [lg_ref[g, rows, :] for g in range(n_g)]
            mx = functools.reduce(jnp.maximum, lses)
            es = [jnp.exp(x - mx) for x in lses]
            inv = 1.0 / functools.reduce(lambda a, b: a + b, es)
            oa = (es[0] * inv) * og_ref[0, rows, :]
            for g in range(1, n_g):
                oa = oa + (es[g] * inv) * og_ref[g, rows, :]
        else:
            num = functools.reduce(lambda a, b: a + b, [og_ref[g, rows, :] for g in range(n_g)])
            den = functools.reduce(lambda a, b: a + b, [lg_ref[g, rows, :] for g in range(n_g)])
            oa = num * (1.0 / den)
        o_ref[rows, :] = oa.astype(o_ref.dtype)
        return c

    def attend(row_max, unroll):
        lax.fori_loop(0, n_blk, functools.partial(block, row_max=row_max), 0, unroll=unroll)
        if mid:
            lax.fori_loop(0, n_blk, scatter_first, 0, unroll=2)
            lax.fori_loop(0, n_blk, scatter_second, 0, unroll=2)
        lax.fori_loop(0, n_blk, functools.partial(combine, row_max=row_max), 0, unroll=2)

    @pl.when(2.0 * bound < EXP_SPAN_LIMIT)
    def _():
        attend(row_max=False, unroll=16)

    @pl.when(2.0 * bound >= EXP_SPAN_LIMIT)
    def _():
        attend(row_max=True, unroll=2)


def _score_bound(gq, gk):
    return 1.01 * HEAD_DIM * jnp.max(jnp.abs(gq)) * jnp.max(jnp.abs(gk))


def _attn_a(proj3, bound, side_casts):
    batch, seq, _ = proj3.shape
    dils = tuple(d for _, d in DILATED_PAIRS)
    n_g = len(dils)
    for d in dils:
        assert d % 8 != 0 or (d % 4 == 0 and (d // 4) % 8 != 0)
    n_slabs = 3 * sum((d > 1) + (d % 8 == 0) for d in dils)
    masks = _band_masks()
    qkv_specs = [pl.BlockSpec((None, seq, HEAD_DIM),
                              lambda b, s, t=t, g=g: (b, 0, t * N_HEADS + g * HEADS_PER_GROUP + s))
                 for g in range(n_g) for t in range(3)]
    cast_specs = [_row_chunk_spec(a, batch * HEADS_PER_GROUP, lambda b, s: b * HEADS_PER_GROUP + s)
                  for a in side_casts]
    return pl.pallas_call(
        functools.partial(_attn_a_kernel, seq=seq, dils=dils, n_casts=len(side_casts)),
        name="attn_a",
        grid=(batch, HEADS_PER_GROUP),
        in_specs=qkv_specs + [pl.BlockSpec(memory_space=pltpu.SMEM),
                              pl.BlockSpec(masks.shape, lambda b, s: (0, 0, 0), pipeline_mode=pl.Buffered(1))]
        + cast_specs,
        out_specs=[pl.BlockSpec((None, seq, HEAD_DIM), lambda b, s: (b, 0, s))] + cast_specs,
        out_shape=[jax.ShapeDtypeStruct((batch, seq, GROUP_W), BF16)]
        + [jax.ShapeDtypeStruct(a.shape, BF16) for a in side_casts],
        scratch_shapes=[pltpu.VMEM((n_g, seq, HEAD_DIM), BF16)] * 2
        + [pltpu.VMEM((n_g, seq, 2 * HEAD_DIM), BF16)]
        + [pltpu.VMEM((n_slabs, seq, HEAD_DIM), F32)]
        + [pltpu.VMEM((n_g, seq, HEAD_DIM), F32)] * 2
        + [pltpu.VMEM(masks.shape, F32)],
        compiler_params=pltpu.CompilerParams(
            dimension_semantics=("parallel", "arbitrary"), vmem_limit_bytes=VMEM_LIMIT),
    )(*([proj3] * (3 * n_g)), bound.reshape(1).astype(F32), masks, *side_casts)


B_QROWS = 4
B_KROWS = B_QROWS + WIN_R
B_NQ = B_QROWS * GRID_W
B_NK = B_KROWS * GRID_W


def _window_row_offsets(rows):
    n_grp = rows // B_QROWS
    assert rows % B_QROWS == 0 and n_grp >= 3
    masked = 2 * WIN_R - 1
    table = []
    for i in (0, 1, n_grp - 1):
        r0 = i * B_QROWS
        ws = int(np.clip(r0 - WIN_R // 2, 0, rows - B_KROWS))
        per_q = []
        for rq in range(B_QROWS):
            r = r0 + rq
            rs = int(np.clip(r - WIN_R // 2, 0, rows - WIN_R))
            per_q.append([ws + jr - r + WIN_R - 1 if rs <= ws + jr < rs + WIN_R else masked
                          for jr in range(B_KROWS)])
        table.append(per_q)
    return table


def _attn_b_kernel(shift_ref, q_ref, k_ref, v_ref, tile_ref, o_ref, bias_ref, *, rows):
    n_grp = rows // B_QROWS
    shift = shift_ref[0]
    span = shift_ref[1]

    @pl.when(pl.program_id(0) == 0)
    def _():
        right = lax.broadcasted_iota(jnp.int32, (GRID_W, 2 * GRID_W), 1) >= GRID_W
        for case, per_q in enumerate(_window_row_offsets(rows)):
            for rq, offs in enumerate(per_q):
                for h in range(N_HEADS_B):
                    for j in range(0, B_KROWS, 2):
                        pair = jnp.where(right, tile_ref[h, offs[j + 1]], tile_ref[h, offs[j]])
                        bias_ref[h, case, rq * GRID_W:(rq + 1) * GRID_W,
                                 j * GRID_W:(j + 2) * GRID_W] = pair - shift

    def group(i, c, row_max):
        r0 = i * B_QROWS
        ws = jnp.clip(r0 - WIN_R // 2, 0, rows - B_KROWS)
        case = jnp.where(i == 0, 0, jnp.where(i == n_grp - 1, 2, 1))
        q0 = pl.multiple_of(r0 * GRID_W, B_NQ)
        k0 = pl.multiple_of(ws * GRID_W, GRID_W)
        for h in range(N_HEADS_B):
            sl = slice(h * HEAD_DIM, (h + 1) * HEAD_DIM)
            q = q_ref[pl.ds(q0, B_NQ), sl]
            k = k_ref[pl.ds(k0, B_NK), sl]
            v = v_ref[pl.ds(k0, B_NK), sl]
            s = lax.dot_general(q, k, (((1,), (1,)), ((), ())), preferred_element_type=F32)
            s = s + bias_ref[h, case]
            if row_max:
                s = s - jnp.max(s, axis=-1, keepdims=True)
            p = jnp.exp(s)
            l = jnp.sum(p, axis=-1, keepdims=True)
            acc = jnp.dot(p.astype(BF16), v, preferred_element_type=F32)
            o_ref[pl.ds(q0, B_NQ), sl] = (acc * (1.0 / l)).astype(o_ref.dtype)
        return c

    @pl.when(span < EXP_SPAN_LIMIT)
    def _():
        lax.fori_loop(0, n_grp, functools.partial(group, row_max=False), 0)

    @pl.when(span >= EXP_SPAN_LIMIT)
    def _():
        lax.fori_loop(0, n_grp, functools.partial(group, row_max=True), 0)


def _neighbourhood_bias_tiles(rpb):
    n_h = rpb.shape[0]
    c = np.arange(GRID_W)
    dc = np.clip(c[None, :] - c[:, None], -(WIN_C - 1), WIN_C - 1) + (WIN_C - 1)
    col_start = np.clip(c - WIN_C // 2, 0, GRID_W - WIN_C)
    col_ok = (c[None, :] >= col_start[:, None]) & (c[None, :] < col_start[:, None] + WIN_C)
    col_pick = np.zeros((2 * WIN_C - 1, GRID_W * GRID_W), np.float32)
    col_pick[dc.ravel(), np.arange(GRID_W * GRID_W)] = 1.0
    tiles = jnp.einsum("hab,bc->hac", rpb.astype(F32), col_pick, precision=lax.Precision.HIGHEST)
    tiles = jnp.where(col_ok.reshape(-1)[None, None], tiles, NEG_INF).reshape(n_h, -1, GRID_W, GRID_W)
    tiles = jnp.concatenate([tiles, jnp.full((n_h, 1, GRID_W, GRID_W), NEG_INF, F32)], axis=1)
    return jnp.concatenate([tiles, tiles], axis=-1)


def _attn_b(proj3, tiles, qk_bound, rpb):
    batch, seq, _ = proj3.shape
    rows = seq // GRID_W
    hi, lo = jnp.max(rpb).astype(F32), jnp.min(rpb).astype(F32)
    shift = jnp.stack([qk_bound + hi, 2.0 * qk_bound + (hi - lo)]).astype(F32)

    def qkv_spec(t):
        return pl.BlockSpec((None, seq, GROUP_W), lambda b: (b, 0, t * (QKV_W // GROUP_W) + B_BLOCK0))

    tile_spec = pl.BlockSpec(tiles.shape, lambda b: (0, 0, 0, 0), pipeline_mode=pl.Buffered(1))
    return pl.pallas_call(
        functools.partial(_attn_b_kernel, rows=rows),
        name="attn_b",
        grid=(batch,),
        in_specs=[pl.BlockSpec(memory_space=pltpu.SMEM), qkv_spec(0), qkv_spec(1), qkv_spec(2), tile_spec],
        out_specs=pl.BlockSpec((None, seq, GROUP_W), lambda b: (b, 0, 0)),
        out_shape=jax.ShapeDtypeStruct((batch, seq, GROUP_W), BF16),
        scratch_shapes=[pltpu.VMEM((N_HEADS_B, 3, B_NQ, B_NK), F32)],
        compiler_params=pltpu.CompilerParams(
            dimension_semantics=("arbitrary",), vmem_limit_bytes=VMEM_LIMIT),
    )(shift, proj3, proj3, proj3, tiles)


def _mix_out_kernel(oa_ref, ob_ref, ga_ref, gb_ref, bg_ref, x_ref, pa_ref, pb_ref, wo_ref, cast_src,
                    h_ref, cast_dst):
    cast_dst[...] = cast_src[...].astype(cast_dst.dtype)
    ya = jnp.dot(oa_ref[...], pa_ref[...], preferred_element_type=F32)
    yb = jnp.dot(ob_ref[...], pb_ref[...], preferred_element_type=F32)
    ga = jax.nn.sigmoid(ga_ref[...].astype(F32) + bg_ref[:, :D_MODEL])
    gb = jax.nn.sigmoid(gb_ref[...].astype(F32) + bg_ref[:, D_MODEL:])
    mixed = (ga * ya + gb * yb).astype(BF16)
    h_ref[...] = x_ref[...] + jnp.dot(mixed, wo_ref[...], preferred_element_type=F32)


def _mix_out(oa, ob, proj, b_gate, x2, pa, pb, wo, side_cast, tm=512):
    n_tok = x2.shape[0]
    row512 = pl.BlockSpec((tm, GROUP_W), lambda i: (i, 0))
    const = pl.Buffered(1)
    cast_spec = _row_chunk_spec(side_cast, n_tok // tm, lambda i: i)
    return pl.pallas_call(
        _mix_out_kernel,
        name="mix_out",
        grid=(n_tok // tm,),
        in_specs=[row512, row512,
                  pl.BlockSpec((tm, D_MODEL), lambda i: (i, GATE_BLOCK0)),
                  pl.BlockSpec((tm, D_MODEL), lambda i: (i, GATE_BLOCK0 + 1)),
                  pl.BlockSpec((1, 2 * D_MODEL), lambda i: (0, 0)),
                  pl.BlockSpec((tm, D_MODEL), lambda i: (i, 0)),
                  pl.BlockSpec((GROUP_W, D_MODEL), lambda i: (0, 0), pipeline_mode=const),
                  pl.BlockSpec((GROUP_W, D_MODEL), lambda i: (0, 0), pipeline_mode=const),
                  pl.BlockSpec((D_MODEL, D_MODEL), lambda i: (0, 0), pipeline_mode=const),
                  cast_spec],
        out_specs=[pl.BlockSpec((tm, D_MODEL), lambda i: (i, 0)), cast_spec],
        out_shape=[jax.ShapeDtypeStruct((n_tok, D_MODEL), F32), jax.ShapeDtypeStruct(side_cast.shape, BF16)],
        compiler_params=pltpu.CompilerParams(
            dimension_semantics=("parallel",), vmem_limit_bytes=VMEM_LIMIT),
    )(oa, ob, proj, proj, b_gate, x2, pa, pb, wo, side_cast)


def _ffn_kernel(h_ref, g_ref, wu_ref, wd_ref, o_ref, hn_ref):
    def partial_out(hn):
        u = jnp.maximum(jnp.dot(hn, wu_ref[...], preferred_element_type=F32), 0.0)
        return jnp.dot((u * u).astype(BF16), wd_ref[...], preferred_element_type=F32)

    @pl.when(pl.program_id(1) == 0)
    def _():
        h = h_ref[...]
        hn = _rms(h, g_ref[...]).astype(BF16)
        hn_ref[...] = hn
        o_ref[...] = h + partial_out(hn)

    @pl.when(pl.program_id(1) > 0)
    def _():
        o_ref[...] += partial_out(hn_ref[...])


def _ffn(h, gain, wu, wd, tm=512, tf=2048):
    n_tok = h.shape[0]
    return pl.pallas_call(
        _ffn_kernel,
        name="ffn",
        grid=(n_tok // tm, D_FF // tf),
        in_specs=[pl.BlockSpec((tm, D_MODEL), lambda i, f: (i, 0)),
                  pl.BlockSpec((1, D_MODEL), lambda i, f: (0, 0)),
                  pl.BlockSpec((D_MODEL, tf), lambda i, f: (0, f)),
                  pl.BlockSpec((tf, D_MODEL), lambda i, f: (f, 0))],
        out_specs=pl.BlockSpec((tm, D_MODEL), lambda i, f: (i, 0)),
        out_shape=jax.ShapeDtypeStruct((n_tok, D_MODEL), F32),
        scratch_shapes=[pltpu.VMEM((tm, D_MODEL), BF16)],
        compiler_params=pltpu.CompilerParams(
            dimension_semantics=("parallel", "arbitrary"), vmem_limit_bytes=VMEM_LIMIT),
    )(h, gain, wu, wd)


def kernel(x, norm_mix, w_in, b_gate, q_norm_a, k_norm_a, q_norm_b, k_norm_b, rpb_b,
           w_proj_a, w_proj_b, w_out, norm_ffn, w_up, w_down):
    batch, seq, d_model = x.shape
    assert d_model == D_MODEL and seq % (DILATED_PAIRS[-1][1] * Q_BLK) == 0
    for win, dil in DILATED_PAIRS:
        assert win // (2 * dil) == HALF_WIN
    depth = norm_mix.shape[0]
    cos, sin = _rope_tables(seq)
    h = x.reshape(batch * seq, D_MODEL)
    for l in range(depth):
        gains_a = jnp.stack([q_norm_a[l] * SCALE, k_norm_a[l]]).astype(F32)[:, None, :]
        rope_c = cos[None] * gains_a
        rope_s = sin[None] * jnp.roll(gains_a, HEAD_DIM // 2, axis=-1)
        gains_b = jnp.stack([q_norm_b[l] * SCALE, k_norm_b[l]]).astype(F32)[:, None, :]
        proj, w_up_l = _in_proj(h, norm_mix[l].reshape(1, D_MODEL), w_in[l].astype(BF16),
                                rope_c, rope_s, gains_b, seq, side_casts=(w_up[l],))
        proj3 = proj.reshape(batch, seq, D_PROJ)
        oa, pa_l, pb_l, wo_l = _attn_a(proj3, _score_bound(gains_a[0], gains_a[1]),
                                       side_casts=(w_proj_a[l], w_proj_b[l], w_out[l]))
        ob = _attn_b(proj3, _neighbourhood_bias_tiles(rpb_b[l]), _score_bound(gains_b[0], gains_b[1]), rpb_b[l])
        h, w_down_l = _mix_out(oa.reshape(batch * seq, GROUP_W), ob.reshape(batch * seq, GROUP_W), proj,
                               b_gate[l].reshape(1, 2 * D_MODEL), h,
                               pa_l, pb_l, wo_l, side_cast=w_down[l])
        h = _ffn(h, norm_ffn[l].reshape(1, D_MODEL), w_up_l, w_down_l)
    return h.reshape(batch, seq, D_MODEL)
```

```python
import functools

import numpy as np
import jax
import jax.numpy as jnp
from jax import lax
from jax.experimental import pallas as pl
from jax.experimental.pallas import tpu as pltpu

D_MODEL = 2048
HEAD_DIM = 128
N_HEADS = 16
N_HEADS_A = 12
N_HEADS_B = 4
DILATED_PAIRS = ((128, 1), (512, 4), (2048, 16))
N_GROUPS_A = len(DILATED_PAIRS)
HEADS_PER_GROUP = 4
GROUP_W = HEADS_PER_GROUP * HEAD_DIM
GRID_W = 64
WIN_R = 8
WIN_C = 16
QKV_W = N_HEADS * HEAD_DIM
D_FF = 4 * D_MODEL
ROPE_THETA = 10000.0
EPS = 1e-6
NEG_INF = -1e30
SCALE = HEAD_DIM ** -0.5

D_PROJ = 3 * QKV_W + 2 * D_MODEL
B_BLOCK0 = N_HEADS_A * HEAD_DIM // GROUP_W
GATE_BLOCK0 = 3 * QKV_W // D_MODEL

VMEM_LIMIT = 56 * 1024 * 1024

F32 = jnp.float32
BF16 = jnp.bfloat16


def _rope_tables(seq):
    pos = np.arange(seq, dtype=np.float64)
    inv = ROPE_THETA ** (-np.arange(0, HEAD_DIM, 2, dtype=np.float64) / HEAD_DIM)
    ang = pos[:, None] * inv[None, :]
    cos = np.concatenate([np.cos(ang), np.cos(ang)], axis=-1)
    sin = np.concatenate([-np.sin(ang), np.sin(ang)], axis=-1)
    return jnp.asarray(cos, F32), jnp.asarray(sin, F32)


def _rms(x, g):
    ms = jnp.mean(x * x, axis=-1, keepdims=True)
    return (x * lax.rsqrt(ms + EPS)) * g


QK_ROW_CHUNKS = 4


def _in_proj_kernel(*refs, n_casts):
    x_ref, g_ref, w_ref, rc_ref, rs_ref, gb_ref = refs[:6]
    cast_srcs = refs[6:6 + n_casts]
    o_ref = refs[6 + n_casts]
    cast_dsts = refs[7 + n_casts:7 + 2 * n_casts]
    xn_ref = refs[7 + 2 * n_casts]
    j = pl.program_id(1)

    @pl.when(pl.program_id(0) * pl.num_programs(1) + j < IN_PROJ_CAST_CHUNKS)
    def _():
        for src, dst in zip(cast_srcs, cast_dsts):
            dst[...] = src[...].astype(dst.dtype)

    def project(xn):
        return jnp.dot(xn, w_ref[...], preferred_element_type=F32)

    def qk_tile(xn):
        rows_per_chunk = xn.shape[0] // QK_ROW_CHUNKS
        for c in range(QK_ROW_CHUNKS):
            rows = slice(c * rows_per_chunk, (c + 1) * rows_per_chunk)
            acc = project(xn[rows, :])
            for h in range(N_HEADS):
                sl = slice(h * HEAD_DIM, (h + 1) * HEAD_DIM)
                y = acc[:, sl]
                inv_rms = lax.rsqrt(jnp.mean(y * y, axis=-1, keepdims=True) + EPS)
                if h < N_HEADS_A:
                    y = y * rc_ref[rows, :] + pltpu.roll(y, HEAD_DIM // 2, 1) * rs_ref[rows, :]
                else:
                    y = y * gb_ref[...]
                o_ref[rows, sl] = (y * inv_rms).astype(o_ref.dtype)

    @pl.when(j == 0)
    def _():
        xn = _rms(x_ref[...], g_ref[...]).astype(BF16)
        xn_ref[...] = xn
        qk_tile(xn)

    @pl.when(j == 1)
    def _():
        qk_tile(xn_ref[...])

    @pl.when(j >= 2)
    def _():
        o_ref[...] = project(xn_ref[...]).astype(o_ref.dtype)


def _row_chunk_spec(a, n_chunks, step):
    assert a.shape[0] % (n_chunks * 16) == 0
    return pl.BlockSpec((a.shape[0] // n_chunks, a.shape[1]),
                        lambda *ids: (jnp.minimum(step(*ids), n_chunks - 1), 0))


IN_PROJ_CAST_CHUNKS = 64


def _in_proj(x2, gain, w, rope_c, rope_s, gain_b, seq, side_casts, tm=1024):
    n_tok = x2.shape[0]
    tn = QKV_W
    assert seq % tm == 0 and D_PROJ % tn == 0
    tiles_per_seq = seq // tm
    n_i, n_j = n_tok // tm, D_PROJ // tn
    assert n_i * n_j >= IN_PROJ_CAST_CHUNKS

    def qk(j):
        return jnp.minimum(j, 1)

    cast_specs = [_row_chunk_spec(a, IN_PROJ_CAST_CHUNKS, lambda i, j: i * n_j + j) for a in side_casts]
    return pl.pallas_call(
        functools.partial(_in_proj_kernel, n_casts=len(side_casts)),
        name="in_proj",
        grid=(n_i, n_j),
        in_specs=[
            pl.BlockSpec((tm, D_MODEL), lambda i, j: (i, 0)),
            pl.BlockSpec((1, D_MODEL), lambda i, j: (0, 0)),
            pl.BlockSpec((D_MODEL, tn), lambda i, j: (0, j)),
            pl.BlockSpec((None, tm, HEAD_DIM), lambda i, j: (qk(j), i % tiles_per_seq, 0)),
            pl.BlockSpec((None, tm, HEAD_DIM), lambda i, j: (qk(j), i % tiles_per_seq, 0)),
            pl.BlockSpec((None, 1, HEAD_DIM), lambda i, j: (qk(j), 0, 0)),
        ] + cast_specs,
        out_specs=[pl.BlockSpec((tm, tn), lambda i, j: (i, j))] + cast_specs,
        out_shape=[jax.ShapeDtypeStruct((n_tok, D_PROJ), BF16)]
        + [jax.ShapeDtypeStruct(a.shape, BF16) for a in side_casts],
        scratch_shapes=[pltpu.VMEM((tm, D_MODEL), BF16)],
        compiler_params=pltpu.CompilerParams(
            dimension_semantics=("arbitrary", "arbitrary"), vmem_limit_bytes=VMEM_LIMIT),
    )(x2, gain, w, rope_c, rope_s, gain_b, *side_casts)


Q_BLK = 128
HALF_WIN = 64
A_WIN = 2 * Q_BLK
EXP_SPAN_LIMIT = 80.0


def _band_masks():
    rel = np.arange(A_WIN)[None, :] - np.arange(Q_BLK)[:, None]
    tiles = [np.where(np.abs(rel - off) <= HALF_WIN, 0.0, NEG_INF) for off in (0, HALF_WIN, 2 * HALF_WIN)]
    return jnp.asarray(np.stack(tiles), F32)


def _attn_a_kernel(*refs, seq, dils, n_casts):
    n_g = len(dils)
    qkv_refs = refs[:3 * n_g]
    bound_ref, mask_ref = refs[3 * n_g:3 * n_g + 2]
    rest = refs[3 * n_g + 2:]
    cast_srcs, o_ref, cast_dsts = rest[:n_casts], rest[n_casts], rest[n_casts + 1:2 * n_casts + 1]
    qn_ref, kn_ref, vn_ref, f_ref, og_ref, lg_ref, band_ref = rest[2 * n_casts + 1:]
    n_blk = seq // Q_BLK
    for src, dst in zip(cast_srcs, cast_dsts):
        dst[...] = src[...].astype(dst.dtype)
    bound = bound_ref[0]
    band_ref[...] = mask_ref[...] - bound
    vn_ref[:, :, HEAD_DIM:] = jnp.ones((n_g, seq, HEAD_DIM), BF16)

    def first_stride(dil):
        return 4 if dil % 8 == 0 else dil

    slabs = iter(range(f_ref.shape[0]))
    stage = {g: tuple(f_ref.at[next(slabs)] for _ in range(3)) for g, dil in enumerate(dils) if dil > 1}
    mid = {g: tuple(f_ref.at[next(slabs)] for _ in range(3))
           for g, dil in enumerate(dils) if first_stride(dil) != dil}

    def block_rows(n):
        return pl.ds(pl.multiple_of(n * Q_BLK, Q_BLK), Q_BLK)

    def token_rows(n, dil):
        if dil == 1:
            return block_rows(n)
        sub_len = seq // dil
        base = n * Q_BLK
        r = base // sub_len
        return pl.ds(r + (base - r * sub_len) * dil, Q_BLK, stride=dil)

    def stage_in(n, c):
        rows = block_rows(n)
        for g, dil in enumerate(dils):
            q_ref, k_ref, v_ref = qkv_refs[3 * g:3 * g + 3]
            if dil == 1:
                vn_ref[g, rows, :HEAD_DIM] = v_ref[rows, :]
            else:
                fq, fk, fv = stage[g]
                fq[rows, :] = q_ref[rows, :].astype(F32)
                fk[rows, :] = k_ref[rows, :].astype(F32)
                fv[rows, :] = v_ref[rows, :].astype(F32)
        return c

    lax.fori_loop(0, n_blk, stage_in, 0, unroll=2)

    def put_operands(g, dst, q, k, v):
        qn_ref[g, dst, :] = q.astype(BF16)
        kn_ref[g, dst, :] = k.astype(BF16)
        vn_ref[g, dst, :HEAD_DIM] = v.astype(BF16)

    def gather(n, c):
        dst = block_rows(n)
        for g, dil in enumerate(dils):
            if dil > 1:
                src = token_rows(n, first_stride(dil))
                q, k, v = (slab[src, :] for slab in stage[g])
                if g in mid:
                    for slab, x in zip(mid[g], (q, k, v)):
                        slab[dst, :] = x
                else:
                    put_operands(g, dst, q, k, v)
        return c

    lax.fori_loop(0, n_blk, gather, 0, unroll=2)

    def second_step_rows(n, g):
        dil = dils[g]
        s1 = first_stride(dil)
        sub_len = seq // dil
        base = n * Q_BLK
        r = base // sub_len
        start = (r % s1) * (seq // s1) + r // s1 + (base - r * sub_len) * (dil // s1)
        return pl.ds(start, Q_BLK, stride=dil // s1)

    def gather_second(n, c):
        for g in mid:
            put_operands(g, block_rows(n), *(slab[second_step_rows(n, g), :] for slab in mid[g]))
        return c

    if mid:
        lax.fori_loop(0, n_blk, gather_second, 0, unroll=2)

    def block(n, c, row_max):
        q0 = pl.multiple_of(n * Q_BLK, Q_BLK)
        for g, dil in enumerate(dils):
            sub_len = seq // dil
            win = min(A_WIN, sub_len)
            sub0 = (n // (sub_len // Q_BLK)) * sub_len
            q_loc = q0 - sub0
            k_loc = jnp.clip(q_loc - HALF_WIN, 0, sub_len - win)
            k0 = pl.multiple_of(sub0 + k_loc, HALF_WIN)
            band = band_ref[(q_loc - k_loc) // HALF_WIN, :, :win]
            q_src, k_src = (qkv_refs[3 * g], qkv_refs[3 * g + 1]) if dil == 1 else (qn_ref.at[g], kn_ref.at[g])
            q = q_src[pl.ds(q0, Q_BLK), :]
            k = k_src[pl.ds(k0, win), :]
            s = lax.dot_general(q, k, (((1,), (1,)), ((), ())), preferred_element_type=F32) + band
            if row_max:
                m = jnp.max(s, axis=-1, keepdims=True)
                s = s - m
            p = jnp.exp(s).astype(BF16)
            acc = jnp.dot(p, vn_ref[g, pl.ds(k0, win), :], preferred_element_type=F32)
            num, denom = acc[:, :HEAD_DIM], acc[:, HEAD_DIM:]
            if row_max:
                first, second = num * (1.0 / denom), jnp.log(denom) + m
            else:
                first, second = num, denom
            if g in mid:
                mid[g][0][pl.ds(q0, Q_BLK), :] = first
                mid[g][1][pl.ds(q0, Q_BLK), :] = second
            else:
                dst = token_rows(n, dil)
                og_ref.at[g][dst, :] = first
                lg_ref.at[g][dst, :] = second
        return c

    def scatter_first(n, c):
        for g in mid:
            dst = second_step_rows(n, g)
            stage[g][0][dst, :] = mid[g][0][block_rows(n), :]
            stage[g][1][dst, :] = mid[g][1][block_rows(n), :]
        return c

    def scatter_second(n, c):
        for g in mid:
            dst = token_rows(n, first_stride(dils[g]))
            og_ref.at[g][dst, :] = stage[g][0][block_rows(n), :]
            lg_ref.at[g][dst, :] = stage[g][1][block_rows(n), :]
        return c

    def combine(n, c, row_max):
        rows = block_rows(n)
        if row_max:
            lses = [lg_ref[g, rows, :] for g in range(n_g)]
            mx = functools.reduce(jnp.maximum, lses)
            es = [jnp.exp(x - mx) for x in lses]
            inv = 1.0 / functools.reduce(lambda a, b: a + b, es)
            oa = (es[0] * inv) * og_ref[0, rows, :]
            for g in range(1, n_g):
                oa = oa + (es[g] * inv) * og_ref[g, rows, :]
        else:
            num = functools.reduce(lambda a, b: a + b, [og_ref[g, rows, :] for g in range(n_g)])
            den = functools.reduce(lambda a, b: a + b, [lg_ref[g, rows, :] for g in range(n_g)])
            oa = num * (1.0 / den)
        o_ref[rows, :] = oa.astype(o_ref.dtype)
        return c

    def attend(row_max, unroll):
        lax.fori_loop(0, n_blk, functools.partial(block, row_max=row_max), 0, unroll=unroll)
        if mid:
            lax.fori_loop(0, n_blk, scatter_first, 0, unroll=2)
            lax.fori_loop(0, n_blk, scatter_second, 0, unroll=2)
        lax.fori_loop(0, n_blk, functools.partial(combine, row_max=row_max), 0, unroll=2)

    @pl.when(2.0 * bound < EXP_SPAN_LIMIT)
    def _():
        attend(row_max=False, unroll=16)

    @pl.when(2.0 * bound >= EXP_SPAN_LIMIT)
    def _():
        attend(row_max=True, unroll=2)


def _score_bound(gq, gk):
    return 1.01 * HEAD_DIM * jnp.max(jnp.abs(gq)) * jnp.max(jnp.abs(gk))


def _attn_a(proj3, bound, side_casts):
    batch, seq, _ = proj3.shape
    dils = tuple(d for _, d in DILATED_PAIRS)
    n_g = len(dils)
    for d in dils:
        assert d % 8 != 0 or (d % 4 == 0 and (d // 4) % 8 != 0)
    n_slabs = 3 * sum((d > 1) + (d % 8 == 0) for d in dils)
    masks = _band_masks()
    qkv_specs = [pl.BlockSpec((None, seq, HEAD_DIM),
                              lambda b, s, t=t, g=g: (b, 0, t * N_HEADS + g * HEADS_PER_GROUP + s))
                 for g in range(n_g) for t in range(3)]
    cast_specs = [_row_chunk_spec(a, batch * HEADS_PER_GROUP, lambda b, s: b * HEADS_PER_GROUP + s)
                  for a in side_casts]
    return pl.pallas_call(
        functools.partial(_attn_a_kernel, seq=seq, dils=dils, n_casts=len(side_casts)),
        name="attn_a",
        grid=(batch, HEADS_PER_GROUP),
        in_specs=qkv_specs + [pl.BlockSpec(memory_space=pltpu.SMEM),
                              pl.BlockSpec(masks.shape, lambda b, s: (0, 0, 0), pipeline_mode=pl.Buffered(1))]
        + cast_specs,
        out_specs=[pl.BlockSpec((None, seq, HEAD_DIM), lambda b, s: (b, 0, s))] + cast_specs,
        out_shape=[jax.ShapeDtypeStruct((batch, seq, GROUP_W), BF16)]
        + [jax.ShapeDtypeStruct(a.shape, BF16) for a in side_casts],
        scratch_shapes=[pltpu.VMEM((n_g, seq, HEAD_DIM), BF16)] * 2
        + [pltpu.VMEM((n_g, seq, 2 * HEAD_DIM), BF16)]
        + [pltpu.VMEM((n_slabs, seq, HEAD_DIM), F32)]
        + [pltpu.VMEM((n_g, seq, HEAD_DIM), F32)] * 2
        + [pltpu.VMEM(masks.shape, F32)],
        compiler_params=pltpu.CompilerParams(
            dimension_semantics=("parallel", "arbitrary"), vmem_limit_bytes=VMEM_LIMIT),
    )(*([proj3] * (3 * n_g)), bound.reshape(1).astype(F32), masks, *side_casts)


B_QROWS = 4
B_KROWS = B_QROWS + WIN_R
B_NQ = B_QROWS * GRID_W
B_NK = B_KROWS * GRID_W


def _window_row_offsets(rows):
    n_grp = rows // B_QROWS
    assert rows % B_QROWS == 0 and n_grp >= 3
    masked = 2 * WIN_R - 1
    table = []
    for i in (0, 1, n_grp - 1):
        r0 = i * B_QROWS
        ws = int(np.clip(r0 - WIN_R // 2, 0, rows - B_KROWS))
        per_q = []
        for rq in range(B_QROWS):
            r = r0 + rq
            rs = int(np.clip(r - WIN_R // 2, 0, rows - WIN_R))
            per_q.append([ws + jr - r + WIN_R - 1 if rs <= ws + jr < rs + WIN_R else masked
                          for jr in range(B_KROWS)])
        table.append(per_q)
    return table


def _attn_b_kernel(shift_ref, q_ref, k_ref, v_ref, tile_ref, o_ref, bias_ref, *, rows):
    n_grp = rows // B_QROWS
    shift = shift_ref[0]
    span = shift_ref[1]

    @pl.when(pl.program_id(0) == 0)
    def _():
        right = lax.broadcasted_iota(jnp.int32, (GRID_W, 2 * GRID_W), 1) >= GRID_W
        for case, per_q in enumerate(_window_row_offsets(rows)):
            for rq, offs in enumerate(per_q):
                for h in range(N_HEADS_B):
                    for j in range(0, B_KROWS, 2):
                        pair = jnp.where(right, tile_ref[h, offs[j + 1]], tile_ref[h, offs[j]])
                        bias_ref[h, case, rq * GRID_W:(rq + 1) * GRID_W,
                                 j * GRID_W:(j + 2) * GRID_W] = pair - shift

    def group(i, c, row_max):
        r0 = i * B_QROWS
        ws = jnp.clip(r0 - WIN_R // 2, 0, rows - B_KROWS)
        case = jnp.where(i == 0, 0, jnp.where(i == n_grp - 1, 2, 1))
        q0 = pl.multiple_of(r0 * GRID_W, B_NQ)
        k0 = pl.multiple_of(ws * GRID_W, GRID_W)
        for h in range(N_HEADS_B):
            sl = slice(h * HEAD_DIM, (h + 1) * HEAD_DIM)
            q = q_ref[pl.ds(q0, B_NQ), sl]
            k = k_ref[pl.ds(k0, B_NK), sl]
            v = v_ref[pl.ds(k0, B_NK), sl]
            s = lax.dot_general(q, k, (((1,), (1,)), ((), ())), preferred_element_type=F32)
            s = s + bias_ref[h, case]
            if row_max:
                s = s - jnp.max(s, axis=-1, keepdims=True)
            p = jnp.exp(s)
            l = jnp.sum(p, axis=-1, keepdims=True)
            acc = jnp.dot(p.astype(BF16), v, preferred_element_type=F32)
            o_ref[pl.ds(q0, B_NQ), sl] = (acc * (1.0 / l)).astype(o_ref.dtype)
        return c

    @pl.when(span < EXP_SPAN_LIMIT)
    def _():
        lax.fori_loop(0, n_grp, functools.partial(group, row_max=False), 0)

    @pl.when(span >= EXP_SPAN_LIMIT)
    def _():
        lax.fori_loop(0, n_grp, functools.partial(group, row_max=True), 0)


def _neighbourhood_bias_tiles(rpb):
    n_h = rpb.shape[0]
    c = np.arange(GRID_W)
    dc = np.clip(c[None, :] - c[:, None], -(WIN_C - 1), WIN_C - 1) + (WIN_C - 1)
    col_start = np.clip(c - WIN_C // 2, 0, GRID_W - WIN_C)
    col_ok = (c[None, :] >= col_start[:, None]) & (c[None, :] < col_start[:, None] + WIN_C)
    col_pick = np.zeros((2 * WIN_C - 1, GRID_W * GRID_W), np.float32)
    col_pick[dc.ravel(), np.arange(GRID_W * GRID_W)] = 1.0
    tiles = jnp.einsum("hab,bc->hac", rpb.astype(F32), col_pick, precision=lax.Precision.HIGHEST)
    tiles = jnp.where(col_ok.reshape(-1)[None, None], tiles, NEG_INF).reshape(n_h, -1, GRID_W, GRID_W)
    tiles = jnp.concatenate([tiles, jnp.full((n_h, 1, GRID_W, GRID_W), NEG_INF, F32)], axis=1)
    return jnp.concatenate([tiles, tiles], axis=-1)


def _attn_b(proj3, tiles, qk_bound, rpb):
    batch, seq, _ = proj3.shape
    rows = seq // GRID_W
    hi, lo = jnp.max(rpb).astype(F32), jnp.min(rpb).astype(F32)
    shift = jnp.stack([qk_bound + hi, 2.0 * qk_bound + (hi - lo)]).astype(F32)

    def qkv_spec(t):
        return pl.BlockSpec((None, seq, GROUP_W), lambda b: (b, 0, t * (QKV_W // GROUP_W) + B_BLOCK0))

    tile_spec = pl.BlockSpec(tiles.shape, lambda b: (0, 0, 0, 0), pipeline_mode=pl.Buffered(1))
    return pl.pallas_call(
        functools.partial(_attn_b_kernel, rows=rows),
        name="attn_b",
        grid=(batch,),
        in_specs=[pl.BlockSpec(memory_space=pltpu.SMEM), qkv_spec(0), qkv_spec(1), qkv_spec(2), tile_spec],
        out_specs=pl.BlockSpec((None, seq, GROUP_W), lambda b: (b, 0, 0)),
        out_shape=jax.ShapeDtypeStruct((batch, seq, GROUP_W), BF16),
        scratch_shapes=[pltpu.VMEM((N_HEADS_B, 3, B_NQ, B_NK), F32)],
        compiler_params=pltpu.CompilerParams(
            dimension_semantics=("arbitrary",), vmem_limit_bytes=VMEM_LIMIT),
    )(shift, proj3, proj3, proj3, tiles)


def _mix_out_kernel(oa_ref, ob_ref, ga_ref, gb_ref, bg_ref, x_ref, pa_ref, pb_ref, wo_ref, cast_src,
                    h_ref, cast_dst):
    cast_dst[...] = cast_src[...].astype(cast_dst.dtype)
    ya = jnp.dot(oa_ref[...], pa_ref[...], preferred_element_type=F32)
    yb = jnp.dot(ob_ref[...], pb_ref[...], preferred_element_type=F32)
    ga = jax.nn.sigmoid(ga_ref[...].astype(F32) + bg_ref[:, :D_MODEL])
    gb = jax.nn.sigmoid(gb_ref[...].astype(F32) + bg_ref[:, D_MODEL:])
    mixed = (ga * ya + gb * yb).astype(BF16)
    h_ref[...] = x_ref[...] + jnp.dot(mixed, wo_ref[...], preferred_element_type=F32)


def _mix_out(oa, ob, proj, b_gate, x2, pa, pb, wo, side_cast, tm=512):
    n_tok = x2.shape[0]
    row512 = pl.BlockSpec((tm, GROUP_W), lambda i: (i, 0))
    const = pl.Buffered(1)
    cast_spec = _row_chunk_spec(side_cast, n_tok // tm, lambda i: i)
    return pl.pallas_call(
        _mix_out_kernel,
        name="mix_out",
        grid=(n_tok // tm,),
        in_specs=[row512, row512,
                  pl.BlockSpec((tm, D_MODEL), lambda i: (i, GATE_BLOCK0)),
                  pl.BlockSpec((tm, D_MODEL), lambda i: (i, GATE_BLOCK0 + 1)),
                  pl.BlockSpec((1, 2 * D_MODEL), lambda i: (0, 0)),
                  pl.BlockSpec((tm, D_MODEL), lambda i: (i, 0)),
                  pl.BlockSpec((GROUP_W, D_MODEL), lambda i: (0, 0), pipeline_mode=const),
                  pl.BlockSpec((GROUP_W, D_MODEL), lambda i: (0, 0), pipeline_mode=const),
                  pl.BlockSpec((D_MODEL, D_MODEL), lambda i: (0, 0), pipeline_mode=const),
                  cast_spec],
        out_specs=[pl.BlockSpec((tm, D_MODEL), lambda i: (i, 0)), cast_spec],
        out_shape=[jax.ShapeDtypeStruct((n_tok, D_MODEL), F32), jax.ShapeDtypeStruct(side_cast.shape, BF16)],
        compiler_params=pltpu.CompilerParams(
            dimension_semantics=("parallel",), vmem_limit_bytes=VMEM_LIMIT),
    )(oa, ob, proj, proj, b_gate, x2, pa, pb, wo, side_cast)


def _ffn_kernel(h_ref, g_ref, wu_ref, wd_ref, o_ref, hn_ref):
    def partial_out(hn):
        u = jnp.maximum(jnp.dot(hn, wu_ref[...], preferred_element_type=F32), 0.0)
        return jnp.dot((u * u).astype(BF16), wd_ref[...], preferred_element_type=F32)

    @pl.when(pl.program_id(1) == 0)
    def _():
        h = h_ref[...]
        hn = _rms(h, g_ref[...]).astype(BF16)
        hn_ref[...] = hn
        o_ref[...] = h + partial_out(hn)

    @pl.when(pl.program_id(1) > 0)
    def _():
        o_ref[...] += partial_out(hn_ref[...])


def _ffn(h, gain, wu, wd, tm=512, tf=2048):
    n_tok = h.shape[0]
    return pl.pallas_call(
        _ffn_kernel,
        name="ffn",
        grid=(n_tok // tm, D_FF // tf),
        in_specs=[pl.BlockSpec((tm, D_MODEL), lambda i, f: (i, 0)),
                  pl.BlockSpec((1, D_MODEL), lambda i, f: (0, 0)),
                  pl.BlockSpec((D_MODEL, tf), lambda i, f: (0, f)),
                  pl.BlockSpec((tf, D_MODEL), lambda i, f: (f, 0))],
        out_specs=pl.BlockSpec((tm, D_MODEL), lambda i, f: (i, 0)),
        out_shape=jax.ShapeDtypeStruct((n_tok, D_MODEL), F32),
        scratch_shapes=[pltpu.VMEM((tm, D_MODEL), BF16)],
        compiler_params=pltpu.CompilerParams(
            dimension_semantics=("parallel", "arbitrary"), vmem_limit_bytes=VMEM_LIMIT),
    )(h, gain, wu, wd)


def kernel(x, norm_mix, w_in, b_gate, q_norm_a, k_norm_a, q_norm_b, k_norm_b, rpb_b,
           w_proj_a, w_proj_b, w_out, norm_ffn, w_up, w_down):
    batch, seq, d_model = x.shape
    assert d_model == D_MODEL and seq % (DILATED_PAIRS[-1][1] * Q_BLK) == 0
    for win, dil in DILATED_PAIRS:
        assert win // (2 * dil) == HALF_WIN
    depth = norm_mix.shape[0]
    cos, sin = _rope_tables(seq)
    h = x.reshape(batch * seq, D_MODEL)
    for l in range(depth):
        gains_a = jnp.stack([q_norm_a[l] * SCALE, k_norm_a[l]]).astype(F32)[:, None, :]
        rope_c = cos[None] * gains_a
        rope_s = sin[None] * jnp.roll(gains_a, HEAD_DIM // 2, axis=-1)
        gains_b = jnp.stack([q_norm_b[l] * SCALE, k_norm_b[l]]).astype(F32)[:, None, :]
        proj, w_up_l = _in_proj(h, norm_mix[l].reshape(1, D_MODEL), w_in[l].astype(BF16),
                                rope_c, rope_s, gains_b, seq, side_casts=(w_up[l],))
        proj3 = proj.reshape(batch, seq, D_PROJ)
        oa, pa_l, pb_l, wo_l = _attn_a(proj3, _score_bound(gains_a[0], gains_a[1]),
                                       side_casts=(w_proj_a[l], w_proj_b[l], w_out[l]))
        ob = _attn_b(proj3, _neighbourhood_bias_tiles(rpb_b[l]), _score_bound(gains_b[0], gains_b[1]), rpb_b[l])
        h, w_down_l = _mix_out(oa.reshape(batch * seq, GROUP_W), ob.reshape(batch * seq, GROUP_W), proj,
                               b_gate[l].reshape(1, 2 * D_MODEL), h,
                               pa_l, pb_l, wo_l, side_cast=w_down[l])
        h = _ffn(h, norm_ffn[l].reshape(1, D_MODEL), w_up_l, w_down_l)
    return h.reshape(batch, seq, D_MODEL)
```

```python
import functools

import numpy as np
import jax
import jax.numpy as jnp
from jax import lax
from jax.experimental import pallas as pl
from jax.experimental.pallas import tpu as pltpu

D_MODEL = 2048
HEAD_DIM = 128
N_HEADS = 16
N_HEADS_A = 12
N_HEADS_B = 4
DILATED_PAIRS = ((128, 1), (512, 4), (2048, 16))
N_GROUPS_A = len(DILATED_PAIRS)
HEADS_PER_GROUP = 4
GROUP_W = HEADS_PER_GROUP * HEAD_DIM
GRID_W = 64
WIN_R = 8
WIN_C = 16
QKV_W = N_HEADS * HEAD_DIM
D_FF = 4 * D_MODEL
ROPE_THETA = 10000.0
EPS = 1e-6
NEG_INF = -1e30
SCALE = HEAD_DIM ** -0.5

D_PROJ = 3 * QKV_W + 2 * D_MODEL
B_BLOCK0 = N_HEADS_A * HEAD_DIM // GROUP_W
GATE_BLOCK0 = 3 * QKV_W // D_MODEL

VMEM_LIMIT = 56 * 1024 * 1024

F32 = jnp.float32
BF16 = jnp.bfloat16


def _rope_tables(seq):
    pos = np.arange(seq, dtype=np.float64)
    inv = ROPE_THETA ** (-np.arange(0, HEAD_DIM, 2, dtype=np.float64) / HEAD_DIM)
    ang = pos[:, None] * inv[None, :]
    cos = np.concatenate([np.cos(ang), np.cos(ang)], axis=-1)
    sin = np.concatenate([-np.sin(ang), np.sin(ang)], axis=-1)
    return jnp.asarray(cos, F32), jnp.asarray(sin, F32)


def _rms(x, g):
    ms = jnp.mean(x * x, axis=-1, keepdims=True)
    return (x * lax.rsqrt(ms + EPS)) * g


QK_ROW_CHUNKS = 4


def _in_proj_kernel(*refs, n_casts):
    x_ref, g_ref, w_ref, rc_ref, rs_ref, gb_ref = refs[:6]
    cast_srcs = refs[6:6 + n_casts]
    o_ref = refs[6 + n_casts]
    cast_dsts = refs[7 + n_casts:7 + 2 * n_casts]
    xn_ref = refs[7 + 2 * n_casts]
    j = pl.program_id(1)

    @pl.when(pl.program_id(0) * pl.num_programs(1) + j < IN_PROJ_CAST_CHUNKS)
    def _():
        for src, dst in zip(cast_srcs, cast_dsts):
            dst[...] = src[...].astype(dst.dtype)

    def project(xn):
        return jnp.dot(xn, w_ref[...], preferred_element_type=F32)

    def qk_tile(xn):
        rows_per_chunk = xn.shape[0] // QK_ROW_CHUNKS
        for c in range(QK_ROW_CHUNKS):
            rows = slice(c * rows_per_chunk, (c + 1) * rows_per_chunk)
            acc = project(xn[rows, :])
            for h in range(N_HEADS):
                sl = slice(h * HEAD_DIM, (h + 1) * HEAD_DIM)
                y = acc[:, sl]
                inv_rms = lax.rsqrt(jnp.mean(y * y, axis=-1, keepdims=True) + EPS)
                if h < N_HEADS_A:
                    y = y * rc_ref[rows, :] + pltpu.roll(y, HEAD_DIM // 2, 1) * rs_ref[rows, :]
                else:
                    y = y * gb_ref[...]
                o_ref[rows, sl] = (y * inv_rms).astype(o_ref.dtype)

    @pl.when(j == 0)
    def _():
        xn = _rms(x_ref[...], g_ref[...]).astype(BF16)
        xn_ref[...] = xn
        qk_tile(xn)

    @pl.when(j == 1)
    def _():
        qk_tile(xn_ref[...])

    @pl.when(j >= 2)
    def _():
        o_ref[...] = project(xn_ref[...]).astype(o_ref.dtype)


def _row_chunk_spec(a, n_chunks, step):
    assert a.shape[0] % (n_chunks * 16) == 0
    return pl.BlockSpec((a.shape[0] // n_chunks, a.shape[1]),
                        lambda *ids: (jnp.minimum(step(*ids), n_chunks - 1), 0))


IN_PROJ_CAST_CHUNKS = 64


def _in_proj(x2, gain, w, rope_c, rope_s, gain_b, seq, side_casts, tm=1024):
    n_tok = x2.shape[0]
    tn = QKV_W
    assert seq % tm == 0 and D_PROJ % tn == 0
    tiles_per_seq = seq // tm
    n_i, n_j = n_tok // tm, D_PROJ // tn
    assert n_i * n_j >= IN_PROJ_CAST_CHUNKS

    def qk(j):
        return jnp.minimum(j, 1)

    cast_specs = [_row_chunk_spec(a, IN_PROJ_CAST_CHUNKS, lambda i, j: i * n_j + j) for a in side_casts]
    return pl.pallas_call(
        functools.partial(_in_proj_kernel, n_casts=len(side_casts)),
        name="in_proj",
        grid=(n_i, n_j),
        in_specs=[
            pl.BlockSpec((tm, D_MODEL), lambda i, j: (i, 0)),
            pl.BlockSpec((1, D_MODEL), lambda i, j: (0, 0)),
            pl.BlockSpec((D_MODEL, tn), lambda i, j: (0, j)),
            pl.BlockSpec((None, tm, HEAD_DIM), lambda i, j: (qk(j), i % tiles_per_seq, 0)),
            pl.BlockSpec((None, tm, HEAD_DIM), lambda i, j: (qk(j), i % tiles_per_seq, 0)),
            pl.BlockSpec((None, 1, HEAD_DIM), lambda i, j: (qk(j), 0, 0)),
        ] + cast_specs,
        out_specs=[pl.BlockSpec((tm, tn), lambda i, j: (i, j))] + cast_specs,
        out_shape=[jax.ShapeDtypeStruct((n_tok, D_PROJ), BF16)]
        + [jax.ShapeDtypeStruct(a.shape, BF16) for a in side_casts],
        scratch_shapes=[pltpu.VMEM((tm, D_MODEL), BF16)],
        compiler_params=pltpu.CompilerParams(
            dimension_semantics=("arbitrary", "arbitrary"), vmem_limit_bytes=VMEM_LIMIT),
    )(x2, gain, w, rope_c, rope_s, gain_b, *side_casts)


Q_BLK = 128
HALF_WIN = 64
A_WIN = 2 * Q_BLK
EXP_SPAN_LIMIT = 80.0


def _band_masks():
    rel = np.arange(A_WIN)[None, :] - np.arange(Q_BLK)[:, None]
    tiles = [np.where(np.abs(rel - off) <= HALF_WIN, 0.0, NEG_INF) for off in (0, HALF_WIN, 2 * HALF_WIN)]
    return jnp.asarray(np.stack(tiles), F32)


def _attn_a_kernel(*refs, seq, dils, n_casts):
    n_g = len(dils)
    qkv_refs = refs[:3 * n_g]
    bound_ref, mask_ref = refs[3 * n_g:3 * n_g + 2]
    rest = refs[3 * n_g + 2:]
    cast_srcs, o_ref, cast_dsts = rest[:n_casts], rest[n_casts], rest[n_casts + 1:2 * n_casts + 1]
    qn_ref, kn_ref, vn_ref, f_ref, og_ref, lg_ref, band_ref = rest[2 * n_casts + 1:]
    n_blk = seq // Q_BLK
    for src, dst in zip(cast_srcs, cast_dsts):
        dst[...] = src[...].astype(dst.dtype)
    bound = bound_ref[0]
    band_ref[...] = mask_ref[...] - bound
    vn_ref[:, :, HEAD_DIM:] = jnp.ones((n_g, seq, HEAD_DIM), BF16)

    def first_stride(dil):
        return 4 if dil % 8 == 0 else dil

    slabs = iter(range(f_ref.shape[0]))
    stage = {g: tuple(f_ref.at[next(slabs)] for _ in range(3)) for g, dil in enumerate(dils) if dil > 1}
    mid = {g: tuple(f_ref.at[next(slabs)] for _ in range(3))
           for g, dil in enumerate(dils) if first_stride(dil) != dil}

    def block_rows(n):
        return pl.ds(pl.multiple_of(n * Q_BLK, Q_BLK), Q_BLK)

    def token_rows(n, dil):
        if dil == 1:
            return block_rows(n)
        sub_len = seq // dil
        base = n * Q_BLK
        r = base // sub_len
        return pl.ds(r + (base - r * sub_len) * dil, Q_BLK, stride=dil)

    def stage_in(n, c):
        rows = block_rows(n)
        for g, dil in enumerate(dils):
            q_ref, k_ref, v_ref = qkv_refs[3 * g:3 * g + 3]
            if dil == 1:
                vn_ref[g, rows, :HEAD_DIM] = v_ref[rows, :]
            else:
                fq, fk, fv = stage[g]
                fq[rows, :] = q_ref[rows, :].astype(F32)
                fk[rows, :] = k_ref[rows, :].astype(F32)
                fv[rows, :] = v_ref[rows, :].astype(F32)
        return c

    lax.fori_loop(0, n_blk, stage_in, 0, unroll=2)

    def put_operands(g, dst, q, k, v):
        qn_ref[g, dst, :] = q.astype(BF16)
        kn_ref[g, dst, :] = k.astype(BF16)
        vn_ref[g, dst, :HEAD_DIM] = v.astype(BF16)

    def gather(n, c):
        dst = block_rows(n)
        for g, dil in enumerate(dils):
            if dil > 1:
                src = token_rows(n, first_stride(dil))
                q, k, v = (slab[src, :] for slab in stage[g])
                if g in mid:
                    for slab, x in zip(mid[g], (q, k, v)):
                        slab[dst, :] = x
                else:
                    put_operands(g, dst, q, k, v)
        return c

    lax.fori_loop(0, n_blk, gather, 0, unroll=2)

    def second_step_rows(n, g):
        dil = dils[g]
        s1 = first_stride(dil)
        sub_len = seq // dil
        base = n * Q_BLK
        r = base // sub_len
        start = (r % s1) * (seq // s1) + r // s1 + (base - r * sub_len) * (dil // s1)
        return pl.ds(start, Q_BLK, stride=dil // s1)

    def gather_second(n, c):
        for g in mid:
            put_operands(g, block_rows(n), *(slab[second_step_rows(n, g), :] for slab in mid[g]))
        return c

    if mid:
        lax.fori_loop(0, n_blk, gather_second, 0, unroll=2)

    def block(n, c, row_max):
        q0 = pl.multiple_of(n * Q_BLK, Q_BLK)
        for g, dil in enumerate(dils):
            sub_len = seq // dil
            win = min(A_WIN, sub_len)
            sub0 = (n // (sub_len // Q_BLK)) * sub_len
            q_loc = q0 - sub0
            k_loc = jnp.clip(q_loc - HALF_WIN, 0, sub_len - win)
            k0 = pl.multiple_of(sub0 + k_loc, HALF_WIN)
            band = band_ref[(q_loc - k_loc) // HALF_WIN, :, :win]
            q_src, k_src = (qkv_refs[3 * g], qkv_refs[3 * g + 1]) if dil == 1 else (qn_ref.at[g], kn_ref.at[g])
            q = q_src[pl.ds(q0, Q_BLK), :]
            k = k_src[pl.ds(k0, win), :]
            s = lax.dot_general(q, k, (((1,), (1,)), ((), ())), preferred_element_type=F32) + band
            if row_max:
                m = jnp.max(s, axis=-1, keepdims=True)
                s = s - m
            p = jnp.exp(s).astype(BF16)
            acc = jnp.dot(p, vn_ref[g, pl.ds(k0, win), :], preferred_element_type=F32)
            num, denom = acc[:, :HEAD_DIM], acc[:, HEAD_DIM:]
            if row_max:
                first, second = num * (1.0 / denom), jnp.log(denom) + m
            else:
                first, second = num, denom
            if g in mid:
                mid[g][0][pl.ds(q0, Q_BLK), :] = first
                mid[g][1][pl.ds(q0, Q_BLK), :] = second
            else:
                dst = token_rows(n, dil)
                og_ref.at[g][dst, :] = first
                lg_ref.at[g][dst, :] = second
        return c

    def scatter_first(n, c):
        for g in mid:
            dst = second_step_rows(n, g)
            stage[g][0][dst, :] = mid[g][0][block_rows(n), :]
            stage[g][1][dst, :] = mid[g][1][block_rows(n), :]
        return c

    def scatter_second(n, c):
        for g in mid:
            dst = token_rows(n, first_stride(dils[g]))
            og_ref.at[g][dst, :] = stage[g][0][block_rows(n), :]
            lg_ref.at[g][dst, :] = stage[g][1][block_rows(n), :]
        return c

    def combine(n, c, row_max):
        rows = block_rows(n)
        if row_max:
            lses = [lg_ref[g, rows, :] for g in range(n_g)]
            mx = functools.reduce(jnp.maximum, lses)
            es = [jnp.exp(x - mx) for x in lses]
            inv = 1.0 / functools.reduce(lambda a, b: a + b, es)
            oa = (es[0] * inv) * og_ref[0, rows, :]
            for g in range(1, n_g):
                oa = oa + (es[g] * inv) * og_ref[g, rows, :]
        else:
            num = functools.reduce(lambda a, b: a + b, [og_ref[g, rows, :] for g in range(n_g)])
            den = functools.reduce(lambda a, b: a + b, [lg_ref[g, rows, :] for g in range(n_g)])
            oa = num * (1.0 / den)
        o_ref[rows, :] = oa.astype(o_ref.dtype)
        return c

    def attend(row_max, unroll):
        lax.fori_loop(0, n_blk, functools.partial(block, row_max=row_max), 0, unroll=unroll)
        if mid:
            lax.fori_loop(0, n_blk, scatter_first, 0, unroll=2)
            lax.fori_loop(0, n_blk, scatter_second, 0, unroll=2)
        lax.fori_loop(0, n_blk, functools.partial(combine, row_max=row_max), 0, unroll=2)

    @pl.when(2.0 * bound < EXP_SPAN_LIMIT)
    def _():
        attend(row_max=False, unroll=16)

    @pl.when(2.0 * bound >= EXP_SPAN_LIMIT)
    def _():
        attend(row_max=True, unroll=2)


def _score_bound(gq, gk):
    return 1.01 * HEAD_DIM * jnp.max(jnp.abs(gq)) * jnp.max(jnp.abs(gk))


def _attn_a(proj3, bound, side_casts):
    batch, seq, _ = proj3.shape
    dils = tuple(d for _, d in DILATED_PAIRS)
    n_g = len(dils)
    for d in dils:
        assert d % 8 != 0 or (d % 4 == 0 and (d // 4) % 8 != 0)
    n_slabs = 3 * sum((d > 1) + (d % 8 == 0) for d in dils)
    masks = _band_masks()
    qkv_specs = [pl.BlockSpec((None, seq, HEAD_DIM),
                              lambda b, s, t=t, g=g: (b, 0, t * N_HEADS + g * HEADS_PER_GROUP + s))
                 for g in range(n_g) for t in range(3)]
    cast_specs = [_row_chunk_spec(a, batch * HEADS_PER_GROUP, lambda b, s: b * HEADS_PER_GROUP + s)
                  for a in side_casts]
    return pl.pallas_call(
        functools.partial(_attn_a_kernel, seq=seq, dils=dils, n_casts=len(side_casts)),
        name="attn_a",
        grid=(batch, HEADS_PER_GROUP),
        in_specs=qkv_specs + [pl.BlockSpec(memory_space=pltpu.SMEM),
                              pl.BlockSpec(masks.shape, lambda b, s: (0, 0, 0), pipeline_mode=pl.Buffered(1))]
        + cast_specs,
        out_specs=[pl.BlockSpec((None, seq, HEAD_DIM), lambda b, s: (b, 0, s))] + cast_specs,
        out_shape=[jax.ShapeDtypeStruct((batch, seq, GROUP_W), BF16)]
        + [jax.ShapeDtypeStruct(a.shape, BF16) for a in side_casts],
        scratch_shapes=[pltpu.VMEM((n_g, seq, HEAD_DIM), BF16)] * 2
        + [pltpu.VMEM((n_g, seq, 2 * HEAD_DIM), BF16)]
        + [pltpu.VMEM((n_slabs, seq, HEAD_DIM), F32)]
        + [pltpu.VMEM((n_g, seq, HEAD_DIM), F32)] * 2
        + [pltpu.VMEM(masks.shape, F32)],
        compiler_params=pltpu.CompilerParams(
            dimension_semantics=("parallel", "arbitrary"), vmem_limit_bytes=VMEM_LIMIT),
    )(*([proj3] * (3 * n_g)), bound.reshape(1).astype(F32), masks, *side_casts)


B_QROWS = 4
B_KROWS = B_QROWS + WIN_R
B_NQ = B_QROWS * GRID_W
B_NK = B_KROWS * GRID_W


def _window_row_offsets(rows):
    n_grp = rows // B_QROWS
    assert rows % B_QROWS == 0 and n_grp >= 3
    masked = 2 * WIN_R - 1
    table = []
    for i in (0, 1, n_grp - 1):
        r0 = i * B_QROWS
        ws = int(np.clip(r0 - WIN_R // 2, 0, rows - B_KROWS))
        per_q = []
        for rq in range(B_QROWS):
            r = r0 + rq
            rs = int(np.clip(r - WIN_R // 2, 0, rows - WIN_R))
            per_q.append([ws + jr - r + WIN_R - 1 if rs <= ws + jr < rs + WIN_R else masked
                          for jr in range(B_KROWS)])
        table.append(per_q)
    return table


def _attn_b_kernel(shift_ref, q_ref, k_ref, v_ref, tile_ref, o_ref, bias_ref, *, rows):
    n_grp = rows // B_QROWS
    shift = shift_ref[0]
    span = shift_ref[1]

    @pl.when(pl.program_id(0) == 0)
    def _():
        right = lax.broadcasted_iota(jnp.int32, (GRID_W, 2 * GRID_W), 1) >= GRID_W
        for case, per_q in enumerate(_window_row_offsets(rows)):
            for rq, offs in enumerate(per_q):
                for h in range(N_HEADS_B):
                    for j in range(0, B_KROWS, 2):
                        pair = jnp.where(right, tile_ref[h, offs[j + 1]], tile_ref[h, offs[j]])
                        bias_ref[h, case, rq * GRID_W:(rq + 1) * GRID_W,
                                 j * GRID_W:(j + 2) * GRID_W] = pair - shift

    def group(i, c, row_max):
        r0 = i * B_QROWS
        ws = jnp.clip(r0 - WIN_R // 2, 0, rows - B_KROWS)
        case = jnp.where(i == 0, 0, jnp.where(i == n_grp - 1, 2, 1))
        q0 = pl.multiple_of(r0 * GRID_W, B_NQ)
        k0 = pl.multiple_of(ws * GRID_W, GRID_W)
        for h in range(N_HEADS_B):
            sl = slice(h * HEAD_DIM, (h + 1) * HEAD_DIM)
            q = q_ref[pl.ds(q0, B_NQ), sl]
            k = k_ref[pl.ds(k0, B_NK), sl]
            v = v_ref[pl.ds(k0, B_NK), sl]
            s = lax.dot_general(q, k, (((1,), (1,)), ((), ())), preferred_element_type=F32)
            s = s + bias_ref[h, case]
            if row_max:
                s = s - jnp.max(s, axis=-1, keepdims=True)
            p = jnp.exp(s)
            l = jnp.sum(p, axis=-1, keepdims=True)
            acc = jnp.dot(p.astype(BF16), v, preferred_element_type=F32)
            o_ref[pl.ds(q0, B_NQ), sl] = (acc * (1.0 / l)).astype(o_ref.dtype)
        return c

    @pl.when(span < EXP_SPAN_LIMIT)
    def _():
        lax.fori_loop(0, n_grp, functools.partial(group, row_max=False), 0, unroll=4)

    @pl.when(span >= EXP_SPAN_LIMIT)
    def _():
        lax.fori_loop(0, n_grp, functools.partial(group, row_max=True), 0)


def _neighbourhood_bias_tiles(rpb):
    n_h = rpb.shape[0]
    c = np.arange(GRID_W)
    dc = np.clip(c[None, :] - c[:, None], -(WIN_C - 1), WIN_C - 1) + (WIN_C - 1)
    col_start = np.clip(c - WIN_C // 2, 0, GRID_W - WIN_C)
    col_ok = (c[None, :] >= col_start[:, None]) & (c[None, :] < col_start[:, None] + WIN_C)
    col_pick = np.zeros((2 * WIN_C - 1, GRID_W * GRID_W), np.float32)
    col_pick[dc.ravel(), np.arange(GRID_W * GRID_W)] = 1.0
    tiles = jnp.einsum("hab,bc->hac", rpb.astype(F32), col_pick, precision=lax.Precision.HIGHEST)
    tiles = jnp.where(col_ok.reshape(-1)[None, None], tiles, NEG_INF).reshape(n_h, -1, GRID_W, GRID_W)
    tiles = jnp.concatenate([tiles, jnp.full((n_h, 1, GRID_W, GRID_W), NEG_INF, F32)], axis=1)
    return jnp.concatenate([tiles, tiles], axis=-1)


def _attn_b(proj3, tiles, qk_bound, rpb):
    batch, seq, _ = proj3.shape
    rows = seq // GRID_W
    hi, lo = jnp.max(rpb).astype(F32), jnp.min(rpb).astype(F32)
    shift = jnp.stack([qk_bound + hi, 2.0 * qk_bound + (hi - lo)]).astype(F32)

    def qkv_spec(t):
        return pl.BlockSpec((None, seq, GROUP_W), lambda b: (b, 0, t * (QKV_W // GROUP_W) + B_BLOCK0))

    tile_spec = pl.BlockSpec(tiles.shape, lambda b: (0, 0, 0, 0), pipeline_mode=pl.Buffered(1))
    return pl.pallas_call(
        functools.partial(_attn_b_kernel, rows=rows),
        name="attn_b",
        grid=(batch,),
        in_specs=[pl.BlockSpec(memory_space=pltpu.SMEM), qkv_spec(0), qkv_spec(1), qkv_spec(2), tile_spec],
        out_specs=pl.BlockSpec((None, seq, GROUP_W), lambda b: (b, 0, 0)),
        out_shape=jax.ShapeDtypeStruct((batch, seq, GROUP_W), BF16),
        scratch_shapes=[pltpu.VMEM((N_HEADS_B, 3, B_NQ, B_NK), F32)],
        compiler_params=pltpu.CompilerParams(
            dimension_semantics=("arbitrary",), vmem_limit_bytes=VMEM_LIMIT),
    )(shift, proj3, proj3, proj3, tiles)


def _mix_out_kernel(oa_ref, ob_ref, ga_ref, gb_ref, bg_ref, x_ref, pa_ref, pb_ref, wo_ref, cast_src,
                    h_ref, cast_dst):
    cast_dst[...] = cast_src[...].astype(cast_dst.dtype)
    ya = jnp.dot(oa_ref[...], pa_ref[...], preferred_element_type=F32)
    yb = jnp.dot(ob_ref[...], pb_ref[...], preferred_element_type=F32)
    ga = jax.nn.sigmoid(ga_ref[...].astype(F32) + bg_ref[:, :D_MODEL])
    gb = jax.nn.sigmoid(gb_ref[...].astype(F32) + bg_ref[:, D_MODEL:])
    mixed = (ga * ya + gb * yb).astype(BF16)
    h_ref[...] = x_ref[...] + jnp.dot(mixed, wo_ref[...], preferred_element_type=F32)


def _mix_out(oa, ob, proj, b_gate, x2, pa, pb, wo, side_cast, tm=512):
    n_tok = x2.shape[0]
    row512 = pl.BlockSpec((tm, GROUP_W), lambda i: (i, 0))
    const = pl.Buffered(1)
    cast_spec = _row_chunk_spec(side_cast, n_tok // tm, lambda i: i)
    return pl.pallas_call(
        _mix_out_kernel,
        name="mix_out",
        grid=(n_tok // tm,),
        in_specs=[row512, row512,
                  pl.BlockSpec((tm, D_MODEL), lambda i: (i, GATE_BLOCK0)),
                  pl.BlockSpec((tm, D_MODEL), lambda i: (i, GATE_BLOCK0 + 1)),
                  pl.BlockSpec((1, 2 * D_MODEL), lambda i: (0, 0)),
                  pl.BlockSpec((tm, D_MODEL), lambda i: (i, 0)),
                  pl.BlockSpec((GROUP_W, D_MODEL), lambda i: (0, 0), pipeline_mode=const),
                  pl.BlockSpec((GROUP_W, D_MODEL), lambda i: (0, 0), pipeline_mode=const),
                  pl.BlockSpec((D_MODEL, D_MODEL), lambda i: (0, 0), pipeline_mode=const),
                  cast_spec],
        out_specs=[pl.BlockSpec((tm, D_MODEL), lambda i: (i, 0)), cast_spec],
        out_shape=[jax.ShapeDtypeStruct((n_tok, D_MODEL), F32), jax.ShapeDtypeStruct(side_cast.shape, BF16)],
        compiler_params=pltpu.CompilerParams(
            dimension_semantics=("parallel",), vmem_limit_bytes=VMEM_LIMIT),
    )(oa, ob, proj, proj, b_gate, x2, pa, pb, wo, side_cast)


def _ffn_kernel(h_ref, g_ref, wu_ref, wd_ref, o_ref, hn_ref):
    def partial_out(hn):
        u = jnp.maximum(jnp.dot(hn, wu_ref[...], preferred_element_type=F32), 0.0)
        return jnp.dot((u * u).astype(BF16), wd_ref[...], preferred_element_type=F32)

    @pl.when(pl.program_id(1) == 0)
    def _():
        h = h_ref[...]
        hn = _rms(h, g_ref[...]).astype(BF16)
        hn_ref[...] = hn
        o_ref[...] = h + partial_out(hn)

    @pl.when(pl.program_id(1) > 0)
    def _():
        o_ref[...] += partial_out(hn_ref[...])


def _ffn(h, gain, wu, wd, tm=512, tf=2048):
    n_tok = h.shape[0]
    return pl.pallas_call(
        _ffn_kernel,
        name="ffn",
        grid=(n_tok // tm, D_FF // tf),
        in_specs=[pl.BlockSpec((tm, D_MODEL), lambda i, f: (i, 0)),
                  pl.BlockSpec((1, D_MODEL), lambda i, f: (0, 0)),
                  pl.BlockSpec((D_MODEL, tf), lambda i, f: (0, f)),
                  pl.BlockSpec((tf, D_MODEL), lambda i, f: (f, 0))],
        out_specs=pl.BlockSpec((tm, D_MODEL), lambda i, f: (i, 0)),
        out_shape=jax.ShapeDtypeStruct((n_tok, D_MODEL), F32),
        scratch_shapes=[pltpu.VMEM((tm, D_MODEL), BF16)],
        compiler_params=pltpu.CompilerParams(
            dimension_semantics=("parallel", "arbitrary"), vmem_limit_bytes=VMEM_LIMIT),
    )(h, gain, wu, wd)


def kernel(x, norm_mix, w_in, b_gate, q_norm_a, k_norm_a, q_norm_b, k_norm_b, rpb_b,
           w_proj_a, w_proj_b, w_out, norm_ffn, w_up, w_down):
    batch, seq, d_model = x.shape
    assert d_model == D_MODEL and seq % (DILATED_PAIRS[-1][1] * Q_BLK) == 0
    for win, dil in DILATED_PAIRS:
        assert win // (2 * dil) == HALF_WIN
    depth = norm_mix.shape[0]
    cos, sin = _rope_tables(seq)
    h = x.reshape(batch * seq, D_MODEL)
    for l in range(depth):
        gains_a = jnp.stack([q_norm_a[l] * SCALE, k_norm_a[l]]).astype(F32)[:, None, :]
        rope_c = cos[None] * gains_a
        rope_s = sin[None] * jnp.roll(gains_a, HEAD_DIM // 2, axis=-1)
        gains_b = jnp.stack([q_norm_b[l] * SCALE, k_norm_b[l]]).astype(F32)[:, None, :]
        proj, w_up_l = _in_proj(h, norm_mix[l].reshape(1, D_MODEL), w_in[l].astype(BF16),
                                rope_c, rope_s, gains_b, seq, side_casts=(w_up[l],))
        proj3 = proj.reshape(batch, seq, D_PROJ)
        oa, pa_l, pb_l, wo_l = _attn_a(proj3, _score_bound(gains_a[0], gains_a[1]),
                                       side_casts=(w_proj_a[l], w_proj_b[l], w_out[l]))
        ob = _attn_b(proj3, _neighbourhood_bias_tiles(rpb_b[l]), _score_bound(gains_b[0], gains_b[1]), rpb_b[l])
        h, w_down_l = _mix_out(oa.reshape(batch * seq, GROUP_W), ob.reshape(batch * seq, GROUP_W), proj,
                               b_gate[l].reshape(1, 2 * D_MODEL), h,
                               pa_l, pb_l, wo_l, side_cast=w_down[l])
        h = _ffn(h, norm_ffn[l].reshape(1, D_MODEL), w_up_l, w_down_l)
    return h.reshape(batch, seq, D_MODEL)
```

```python
import functools

import numpy as np
import jax
import jax.numpy as jnp
from jax import lax
from jax.experimental import pallas as pl
from jax.experimental.pallas import tpu as pltpu

D_MODEL = 2048
HEAD_DIM = 128
N_HEADS = 16
N_HEADS_A = 12
N_HEADS_B = 4
DILATED_PAIRS = ((128, 1), (512, 4), (2048, 16))
N_GROUPS_A = len(DILATED_PAIRS)
HEADS_PER_GROUP = 4
GROUP_W = HEADS_PER_GROUP * HEAD_DIM
GRID_W = 64
WIN_R = 8
WIN_C = 16
QKV_W = N_HEADS * HEAD_DIM
D_FF = 4 * D_MODEL
ROPE_THETA = 10000.0
EPS = 1e-6
NEG_INF = -1e30
SCALE = HEAD_DIM ** -0.5

D_PROJ = 3 * QKV_W + 2 * D_MODEL
B_BLOCK0 = N_HEADS_A * HEAD_DIM // GROUP_W
GATE_BLOCK0 = 3 * QKV_W // D_MODEL

VMEM_LIMIT = 56 * 1024 * 1024

F32 = jnp.float32
BF16 = jnp.bfloat16


def _rope_tables(seq):
    pos = np.arange(seq, dtype=np.float64)
    inv = ROPE_THETA ** (-np.arange(0, HEAD_DIM, 2, dtype=np.float64) / HEAD_DIM)
    ang = pos[:, None] * inv[None, :]
    cos = np.concatenate([np.cos(ang), np.cos(ang)], axis=-1)
    sin = np.concatenate([-np.sin(ang), np.sin(ang)], axis=-1)
    return jnp.asarray(cos, F32), jnp.asarray(sin, F32)


def _rms(x, g):
    ms = jnp.mean(x * x, axis=-1, keepdims=True)
    return (x * lax.rsqrt(ms + EPS)) * g


QK_ROW_CHUNKS = 4


def _qk_norm_store(acc, rows, first_head, rc_ref, rs_ref, gb_ref, o_ref):
    for c in range(acc.shape[1] // HEAD_DIM):
        sl = slice(c * HEAD_DIM, (c + 1) * HEAD_DIM)
        y = acc[:, sl]
        inv_rms = lax.rsqrt(jnp.mean(y * y, axis=-1, keepdims=True) + EPS)
        if first_head + c < N_HEADS_A:
            y = y * rc_ref[rows, :] + pltpu.roll(y, HEAD_DIM // 2, 1) * rs_ref[rows, :]
        else:
            y = y * gb_ref[...]
        o_ref[rows, sl] = (y * inv_rms).astype(o_ref.dtype)


def _in_proj_first_kernel(x_ref, g_ref, w_ref, rc_ref, rs_ref, gb_ref, o_ref, wb_ref, xn_ref):
    j = pl.program_id(0)
    heads_per_tile = w_ref.shape[1] // HEAD_DIM

    def cast_weight():
        wb = w_ref[...].astype(BF16)
        wb_ref[...] = wb
        return wb

    def qk_tile(xn, first_head):
        wb = cast_weight()
        rows_per_chunk = xn.shape[0] // QK_ROW_CHUNKS
        for c in range(QK_ROW_CHUNKS):
            rows = slice(c * rows_per_chunk, (c + 1) * rows_per_chunk)
            acc = jnp.dot(xn[rows, :], wb, preferred_element_type=F32)
            _qk_norm_store(acc, rows, first_head, rc_ref, rs_ref, gb_ref, o_ref)

    @pl.when(j == 0)
    def _():
        xn = _rms(x_ref[...], g_ref[...]).astype(BF16)
        xn_ref[...] = xn
        qk_tile(xn, 0)

    @pl.when(j == 2)
    def _():
        qk_tile(xn_ref[...], 0)

    @pl.when((j == 1) | (j == 3))
    def _():
        qk_tile(xn_ref[...], heads_per_tile)

    @pl.when(j >= 4)
    def _():
        o_ref[...] = jnp.dot(xn_ref[...], cast_weight(), preferred_element_type=F32).astype(o_ref.dtype)


def _in_proj_kernel(*refs, n_casts):
    x_ref, g_ref, w_ref, rc_ref, rs_ref, gb_ref = refs[:6]
    cast_srcs = refs[7:7 + n_casts]
    o_ref = refs[7 + n_casts]
    cast_dsts = refs[8 + n_casts:8 + 2 * n_casts]
    xn_ref = refs[8 + 2 * n_casts]
    j = pl.program_id(1)

    @pl.when(pl.program_id(0) * pl.num_programs(1) + j < IN_PROJ_CAST_CHUNKS)
    def _():
        for src, dst in zip(cast_srcs, cast_dsts):
            dst[...] = src[...].astype(dst.dtype)

    def project(xn):
        return jnp.dot(xn, w_ref[...], preferred_element_type=F32)

    def qk_tile(xn):
        rows_per_chunk = xn.shape[0] // QK_ROW_CHUNKS
        for c in range(QK_ROW_CHUNKS):
            rows = slice(c * rows_per_chunk, (c + 1) * rows_per_chunk)
            _qk_norm_store(project(xn[rows, :]), rows, 0, rc_ref, rs_ref, gb_ref, o_ref)

    @pl.when(j == 0)
    def _():
        xn = _rms(x_ref[...], g_ref[...]).astype(BF16)
        xn_ref[...] = xn
        qk_tile(xn)

    @pl.when(j == 1)
    def _():
        qk_tile(xn_ref[...])

    @pl.when(j >= 2)
    def _():
        o_ref[...] = project(xn_ref[...]).astype(o_ref.dtype)


def _row_chunk_spec(a, n_chunks, step):
    assert a.shape[0] % (n_chunks * 16) == 0
    return pl.BlockSpec((a.shape[0] // n_chunks, a.shape[1]),
                        lambda *ids: (jnp.minimum(step(*ids), n_chunks - 1), 0))


IN_PROJ_CAST_CHUNKS = 64


IN_PROJ_TM = 1024


def _in_proj_first(x2, gain, w_f32, rope_c, rope_s, gain_b):
    n_tok = x2.shape[0]
    tm, tn = IN_PROJ_TM, QKV_W // 2

    def qk(j):
        return jnp.minimum(j // 2, 1)

    return pl.pallas_call(
        _in_proj_first_kernel,
        name="in_proj_first",
        grid=(D_PROJ // tn,),
        in_specs=[
            pl.BlockSpec((tm, D_MODEL), lambda j: (0, 0), pipeline_mode=pl.Buffered(1)),
            pl.BlockSpec((1, D_MODEL), lambda j: (0, 0)),
            pl.BlockSpec((D_MODEL, tn), lambda j: (0, j)),
            pl.BlockSpec((None, tm, HEAD_DIM), lambda j: (qk(j), 0, 0)),
            pl.BlockSpec((None, tm, HEAD_DIM), lambda j: (qk(j), 0, 0)),
            pl.BlockSpec((None, 1, HEAD_DIM), lambda j: (qk(j), 0, 0)),
        ],
        out_specs=[pl.BlockSpec((tm, tn), lambda j: (0, j)), pl.BlockSpec((D_MODEL, tn), lambda j: (0, j))],
        out_shape=[jax.ShapeDtypeStruct((n_tok, D_PROJ), BF16), jax.ShapeDtypeStruct(w_f32.shape, BF16)],
        scratch_shapes=[pltpu.VMEM((tm, D_MODEL), BF16)],
        compiler_params=pltpu.CompilerParams(
            dimension_semantics=("arbitrary",), vmem_limit_bytes=VMEM_LIMIT),
    )(x2, gain, w_f32, rope_c, rope_s, gain_b)


def _in_proj(x2, gain, w, rope_c, rope_s, gain_b, seq, proj_first, side_casts):
    n_tok = x2.shape[0]
    tm = IN_PROJ_TM
    tn = QKV_W
    assert seq % tm == 0 and D_PROJ % tn == 0
    tiles_per_seq = seq // tm
    n_i, n_j = n_tok // tm - 1, D_PROJ // tn
    assert n_i * n_j >= IN_PROJ_CAST_CHUNKS

    def qk(j):
        return jnp.minimum(j, 1)

    cast_specs = [_row_chunk_spec(a, IN_PROJ_CAST_CHUNKS, lambda i, j: i * n_j + j) for a in side_casts]
    return pl.pallas_call(
        functools.partial(_in_proj_kernel, n_casts=len(side_casts)),
        name="in_proj",
        grid=(n_i, n_j),
        in_specs=[
            pl.BlockSpec((tm, D_MODEL), lambda i, j: (i + 1, 0)),
            pl.BlockSpec((1, D_MODEL), lambda i, j: (0, 0)),
            pl.BlockSpec((D_MODEL, tn), lambda i, j: (0, j)),
            pl.BlockSpec((None, tm, HEAD_DIM), lambda i, j: (qk(j), (i + 1) % tiles_per_seq, 0)),
            pl.BlockSpec((None, tm, HEAD_DIM), lambda i, j: (qk(j), (i + 1) % tiles_per_seq, 0)),
            pl.BlockSpec((None, 1, HEAD_DIM), lambda i, j: (qk(j), 0, 0)),
            pl.BlockSpec(memory_space=pl.ANY),
        ] + cast_specs,
        out_specs=[pl.BlockSpec((tm, tn), lambda i, j: (i + 1, j))] + cast_specs,
        out_shape=[jax.ShapeDtypeStruct((n_tok, D_PROJ), BF16)]
        + [jax.ShapeDtypeStruct(a.shape, BF16) for a in side_casts],
        scratch_shapes=[pltpu.VMEM((tm, D_MODEL), BF16)],
        input_output_aliases={6: 0},
        compiler_params=pltpu.CompilerParams(
            dimension_semantics=("arbitrary", "arbitrary"), vmem_limit_bytes=VMEM_LIMIT),
    )(x2, gain, w, rope_c, rope_s, gain_b, proj_first, *side_casts)


Q_BLK = 128
HALF_WIN = 64
A_WIN = 2 * Q_BLK
EXP_SPAN_LIMIT = 80.0


def _band_masks():
    rel = np.arange(A_WIN)[None, :] - np.arange(Q_BLK)[:, None]
    tiles = [np.where(np.abs(rel - off) <= HALF_WIN, 0.0, NEG_INF) for off in (0, HALF_WIN, 2 * HALF_WIN)]
    return jnp.asarray(np.stack(tiles), F32)


def _attn_a_kernel(*refs, seq, dils, n_casts):
    n_g = len(dils)
    qkv_refs = refs[:3 * n_g]
    bound_ref, mask_ref = refs[3 * n_g:3 * n_g + 2]
    rest = refs[3 * n_g + 2:]
    cast_srcs, o_ref, cast_dsts = rest[:n_casts], rest[n_casts], rest[n_casts + 1:2 * n_casts + 1]
    qn_ref, kn_ref, vn_ref, f_ref, og_ref, lg_ref, band_ref = rest[2 * n_casts + 1:]
    n_blk = seq // Q_BLK
    for src, dst in zip(cast_srcs, cast_dsts):
        dst[...] = src[...].astype(dst.dtype)
    bound = bound_ref[0]
    band_ref[...] = mask_ref[...] - bound
    vn_ref[:, :, HEAD_DIM:] = jnp.ones((n_g, seq, HEAD_DIM), BF16)

    def first_stride(dil):
        return 4 if dil % 8 == 0 else dil

    slabs = iter(range(f_ref.shape[0]))
    stage = {g: tuple(f_ref.at[next(slabs)] for _ in range(3)) for g, dil in enumerate(dils) if dil > 1}
    mid = {g: tuple(f_ref.at[next(slabs)] for _ in range(3))
           for g, dil in enumerate(dils) if first_stride(dil) != dil}

    def block_rows(n):
        return pl.ds(pl.multiple_of(n * Q_BLK, Q_BLK), Q_BLK)

    def token_rows(n, dil):
        if dil == 1:
            return block_rows(n)
        sub_len = seq // dil
        base = n * Q_BLK
        r = base // sub_len
        return pl.ds(r + (base - r * sub_len) * dil, Q_BLK, stride=dil)

    def stage_in(n, c):
        rows = block_rows(n)
        for g, dil in enumerate(dils):
            q_ref, k_ref, v_ref = qkv_refs[3 * g:3 * g + 3]
            if dil == 1:
                vn_ref[g, rows, :HEAD_DIM] = v_ref[rows, :]
            else:
                fq, fk, fv = stage[g]
                fq[rows, :] = q_ref[rows, :].astype(F32)
                fk[rows, :] = k_ref[rows, :].astype(F32)
                fv[rows, :] = v_ref[rows, :].astype(F32)
        return c

    lax.fori_loop(0, n_blk, stage_in, 0, unroll=2)

    def put_operands(g, dst, q, k, v):
        qn_ref[g, dst, :] = q.astype(BF16)
        kn_ref[g, dst, :] = k.astype(BF16)
        vn_ref[g, dst, :HEAD_DIM] = v.astype(BF16)

    def gather(n, c):
        dst = block_rows(n)
        for g, dil in enumerate(dils):
            if dil > 1:
                src = token_rows(n, first_stride(dil))
                q, k, v = (slab[src, :] for slab in stage[g])
                if g in mid:
                    for slab, x in zip(mid[g], (q, k, v)):
                        slab[dst, :] = x
                else:
                    put_operands(g, dst, q, k, v)
        return c

    lax.fori_loop(0, n_blk, gather, 0, unroll=2)

    def second_step_rows(n, g):
        dil = dils[g]
        s1 = first_stride(dil)
        sub_len = seq // dil
        base = n * Q_BLK
        r = base // sub_len
        start = (r % s1) * (seq // s1) + r // s1 + (base - r * sub_len) * (dil // s1)
        return pl.ds(start, Q_BLK, stride=dil // s1)

    def gather_second(n, c):
        for g in mid:
            put_operands(g, block_rows(n), *(slab[second_step_rows(n, g), :] for slab in mid[g]))
        return c

    if mid:
        lax.fori_loop(0, n_blk, gather_second, 0, unroll=2)

    def block(n, c, row_max):
        q0 = pl.multiple_of(n * Q_BLK, Q_BLK)
        for g, dil in enumerate(dils):
            sub_len = seq // dil
            win = min(A_WIN, sub_len)
            sub0 = (n // (sub_len // Q_BLK)) * sub_len
            q_loc = q0 - sub0
            k_loc = jnp.clip(q_loc - HALF_WIN, 0, sub_len - win)
            k0 = pl.multiple_of(sub0 + k_loc, HALF_WIN)
            band = band_ref[(q_loc - k_loc) // HALF_WIN, :, :win]
            q_src, k_src = (qkv_refs[3 * g], qkv_refs[3 * g + 1]) if dil == 1 else (qn_ref.at[g], kn_ref.at[g])
            q = q_src[pl.ds(q0, Q_BLK), :]
            k = k_src[pl.ds(k0, win), :]
            s = lax.dot_general(q, k, (((1,), (1,)), ((), ())), preferred_element_type=F32) + band
            if row_max:
                m = jnp.max(s, axis=-1, keepdims=True)
                s = s - m
            p = jnp.exp(s).astype(BF16)
            acc = jnp.dot(p, vn_ref[g, pl.ds(k0, win), :], preferred_element_type=F32)
            num, denom = acc[:, :HEAD_DIM], acc[:, HEAD_DIM:]
            if row_max:
                first, second = num * (1.0 / denom), jnp.log(denom) + m
            else:
                first, second = num, denom
            if g in mid:
                mid[g][0][pl.ds(q0, Q_BLK), :] = first
                mid[g][1][pl.ds(q0, Q_BLK), :] = second
            else:
                dst = token_rows(n, dil)
                og_ref.at[g][dst, :] = first
                lg_ref.at[g][dst, :] = second
        return c

    def scatter_first(n, c):
        for g in mid:
            dst = second_step_rows(n, g)
            stage[g][0][dst, :] = mid[g][0][block_rows(n), :]
            stage[g][1][dst, :] = mid[g][1][block_rows(n), :]
        return c

    def scatter_second(n, c):
        for g in mid:
            dst = token_rows(n, first_stride(dils[g]))
            og_ref.at[g][dst, :] = stage[g][0][block_rows(n), :]
            lg_ref.at[g][dst, :] = stage[g][1][block_rows(n), :]
        return c

    def combine(n, c, row_max):
        rows = block_rows(n)
        if row_max:
            lses = [lg_ref[g, rows, :] for g in range(n_g)]
            mx = functools.reduce(jnp.maximum, lses)
            es = [jnp.exp(x - mx) for x in lses]
            inv = 1.0 / functools.reduce(lambda a, b: a + b, es)
            oa = (es[0] * inv) * og_ref[0, rows, :]
            for g in range(1, n_g):
                oa = oa + (es[g] * inv) * og_ref[g, rows, :]
        else:
            num = functools.reduce(lambda a, b: a + b, [og_ref[g, rows, :] for g in range(n_g)])
            den = functools.reduce(lambda a, b: a + b, [lg_ref[g, rows, :] for g in range(n_g)])
            oa = num * (1.0 / den)
        o_ref[rows, :] = oa.astype(o_ref.dtype)
        return c

    def attend(row_max, unroll):
        lax.fori_loop(0, n_blk, functools.partial(block, row_max=row_max), 0, unroll=unroll)
        if mid:
            lax.fori_loop(0, n_blk, scatter_first, 0, unroll=2)
            lax.fori_loop(0, n_blk, scatter_second, 0, unroll=2)
        lax.fori_loop(0, n_blk, functools.partial(combine, row_max=row_max), 0, unroll=2)

    @pl.when(2.0 * bound < EXP_SPAN_LIMIT)
    def _():
        attend(row_max=False, unroll=16)

    @pl.when(2.0 * bound >= EXP_SPAN_LIMIT)
    def _():
        attend(row_max=True, unroll=2)


def _score_bound(gq, gk):
    return 1.01 * HEAD_DIM * jnp.max(jnp.abs(gq)) * jnp.max(jnp.abs(gk))


def _attn_a(proj3, bound, side_casts):
    batch, seq, _ = proj3.shape
    dils = tuple(d for _, d in DILATED_PAIRS)
    n_g = len(dils)
    for d in dils:
        assert d % 8 != 0 or (d % 4 == 0 and (d // 4) % 8 != 0)
    n_slabs = 3 * sum((d > 1) + (d % 8 == 0) for d in dils)
    masks = _band_masks()
    qkv_specs = [pl.BlockSpec((None, seq, HEAD_DIM),
                              lambda b, s, t=t, g=g: (b, 0, t * N_HEADS + g * HEADS_PER_GROUP + s))
                 for g in range(n_g) for t in range(3)]
    cast_specs = [_row_chunk_spec(a, batch * HEADS_PER_GROUP, lambda b, s: b * HEADS_PER_GROUP + s)
                  for a in side_casts]
    return pl.pallas_call(
        functools.partial(_attn_a_kernel, seq=seq, dils=dils, n_casts=len(side_casts)),
        name="attn_a",
        grid=(batch, HEADS_PER_GROUP),
        in_specs=qkv_specs + [pl.BlockSpec(memory_space=pltpu.SMEM),
                              pl.BlockSpec(masks.shape, lambda b, s: (0, 0, 0), pipeline_mode=pl.Buffered(1))]
        + cast_specs,
        out_specs=[pl.BlockSpec((None, seq, HEAD_DIM), lambda b, s: (b, 0, s))] + cast_specs,
        out_shape=[jax.ShapeDtypeStruct((batch, seq, GROUP_W), BF16)]
        + [jax.ShapeDtypeStruct(a.shape, BF16) for a in side_casts],
        scratch_shapes=[pltpu.VMEM((n_g, seq, HEAD_DIM), BF16)] * 2
        + [pltpu.VMEM((n_g, seq, 2 * HEAD_DIM), BF16)]
        + [pltpu.VMEM((n_slabs, seq, HEAD_DIM), F32)]
        + [pltpu.VMEM((n_g, seq, HEAD_DIM), F32)] * 2
        + [pltpu.VMEM(masks.shape, F32)],
        compiler_params=pltpu.CompilerParams(
            dimension_semantics=("parallel", "arbitrary"), vmem_limit_bytes=VMEM_LIMIT),
    )(*([proj3] * (3 * n_g)), bound.reshape(1).astype(F32), masks, *side_casts)


B_QROWS = 4
B_KROWS = B_QROWS + WIN_R
B_NQ = B_QROWS * GRID_W
B_NK = B_KROWS * GRID_W


def _window_row_offsets(rows):
    n_grp = rows // B_QROWS
    assert rows % B_QROWS == 0 and n_grp >= 3
    masked = 2 * WIN_R - 1
    table = []
    for i in (0, 1, n_grp - 1):
        r0 = i * B_QROWS
        ws = int(np.clip(r0 - WIN_R // 2, 0, rows - B_KROWS))
        per_q = []
        for rq in range(B_QROWS):
            r = r0 + rq
            rs = int(np.clip(r - WIN_R // 2, 0, rows - WIN_R))
            per_q.append([ws + jr - r + WIN_R - 1 if rs <= ws + jr < rs + WIN_R else masked
                          for jr in range(B_KROWS)])
        table.append(per_q)
    return table


def _attn_b_kernel(shift_ref, q_ref, k_ref, v_ref, tile_ref, o_ref, bias_ref, *, rows):
    n_grp = rows // B_QROWS
    shift = shift_ref[0]
    span = shift_ref[1]

    @pl.when(pl.program_id(0) == 0)
    def _():
        right = lax.broadcasted_iota(jnp.int32, (GRID_W, 2 * GRID_W), 1) >= GRID_W
        for case, per_q in enumerate(_window_row_offsets(rows)):
            for rq, offs in enumerate(per_q):
                for h in range(N_HEADS_B):
                    for j in range(0, B_KROWS, 2):
                        pair = jnp.where(right, tile_ref[h, offs[j + 1]], tile_ref[h, offs[j]])
                        bias_ref[h, case, rq * GRID_W:(rq + 1) * GRID_W,
                                 j * GRID_W:(j + 2) * GRID_W] = pair - shift

    def group(i, c, row_max):
        r0 = i * B_QROWS
        ws = jnp.clip(r0 - WIN_R // 2, 0, rows - B_KROWS)
        case = jnp.where(i == 0, 0, jnp.where(i == n_grp - 1, 2, 1))
        q0 = pl.multiple_of(r0 * GRID_W, B_NQ)
        k0 = pl.multiple_of(ws * GRID_W, GRID_W)
        for h in range(N_HEADS_B):
            sl = slice(h * HEAD_DIM, (h + 1) * HEAD_DIM)
            q = q_ref[pl.ds(q0, B_NQ), sl]
            k = k_ref[pl.ds(k0, B_NK), sl]
            v = v_ref[pl.ds(k0, B_NK), sl]
            s = lax.dot_general(q, k, (((1,), (1,)), ((), ())), preferred_element_type=F32)
            s = s + bias_ref[h, case]
            if row_max:
                s = s - jnp.max(s, axis=-1, keepdims=True)
            p = jnp.exp(s)
            l = jnp.sum(p, axis=-1, keepdims=True)
            acc = jnp.dot(p.astype(BF16), v, preferred_element_type=F32)
            o_ref[pl.ds(q0, B_NQ), sl] = (acc * (1.0 / l)).astype(o_ref.dtype)
        return c

    @pl.when(span < EXP_SPAN_LIMIT)
    def _():
        lax.fori_loop(0, n_grp, functools.partial(group, row_max=False), 0, unroll=4)

    @pl.when(span >= EXP_SPAN_LIMIT)
    def _():
        lax.fori_loop(0, n_grp, functools.partial(group, row_max=True), 0)


def _neighbourhood_bias_tiles(rpb):
    n_h = rpb.shape[0]
    c = np.arange(GRID_W)
    dc = np.clip(c[None, :] - c[:, None], -(WIN_C - 1), WIN_C - 1) + (WIN_C - 1)
    col_start = np.clip(c - WIN_C // 2, 0, GRID_W - WIN_C)
    col_ok = (c[None, :] >= col_start[:, None]) & (c[None, :] < col_start[:, None] + WIN_C)
    col_pick = np.zeros((2 * WIN_C - 1, GRID_W * GRID_W), np.float32)
    col_pick[dc.ravel(), np.arange(GRID_W * GRID_W)] = 1.0
    tiles = jnp.einsum("hab,bc->hac", rpb.astype(F32), col_pick, precision=lax.Precision.HIGHEST)
    tiles = jnp.where(col_ok.reshape(-1)[None, None], tiles, NEG_INF).reshape(n_h, -1, GRID_W, GRID_W)
    tiles = jnp.concatenate([tiles, jnp.full((n_h, 1, GRID_W, GRID_W), NEG_INF, F32)], axis=1)
    return jnp.concatenate([tiles, tiles], axis=-1)


def _attn_b(proj3, tiles, qk_bound, rpb):
    batch, seq, _ = proj3.shape
    rows = seq // GRID_W
    hi, lo = jnp.max(rpb).astype(F32), jnp.min(rpb).astype(F32)
    shift = jnp.stack([qk_bound + hi, 2.0 * qk_bound + (hi - lo)]).astype(F32)

    def qkv_spec(t):
        return pl.BlockSpec((None, seq, GROUP_W), lambda b: (b, 0, t * (QKV_W // GROUP_W) + B_BLOCK0))

    tile_spec = pl.BlockSpec(tiles.shape, lambda b: (0, 0, 0, 0), pipeline_mode=pl.Buffered(1))
    return pl.pallas_call(
        functools.partial(_attn_b_kernel, rows=rows),
        name="attn_b",
        grid=(batch,),
        in_specs=[pl.BlockSpec(memory_space=pltpu.SMEM), qkv_spec(0), qkv_spec(1), qkv_spec(2), tile_spec],
        out_specs=pl.BlockSpec((None, seq, GROUP_W), lambda b: (b, 0, 0)),
        out_shape=jax.ShapeDtypeStruct((batch, seq, GROUP_W), BF16),
        scratch_shapes=[pltpu.VMEM((N_HEADS_B, 3, B_NQ, B_NK), F32)],
        compiler_params=pltpu.CompilerParams(
            dimension_semantics=("arbitrary",), vmem_limit_bytes=VMEM_LIMIT),
    )(shift, proj3, proj3, proj3, tiles)


def _mix_out_kernel(oa_ref, ob_ref, ga_ref, gb_ref, bg_ref, x_ref, pa_ref, pb_ref, wo_ref, cast_src,
                    h_ref, cast_dst):
    cast_dst[...] = cast_src[...].astype(cast_dst.dtype)
    ya = jnp.dot(oa_ref[...], pa_ref[...], preferred_element_type=F32)
    yb = jnp.dot(ob_ref[...], pb_ref[...], preferred_element_type=F32)
    ga = jax.nn.sigmoid(ga_ref[...].astype(F32) + bg_ref[:, :D_MODEL])
    gb = jax.nn.sigmoid(gb_ref[...].astype(F32) + bg_ref[:, D_MODEL:])
    mixed = (ga * ya + gb * yb).astype(BF16)
    h_ref[...] = x_ref[...] + jnp.dot(mixed, wo_ref[...], preferred_element_type=F32)


def _mix_out(oa, ob, proj, b_gate, x2, pa, pb, wo, side_cast, tm=512):
    n_tok = x2.shape[0]
    row512 = pl.BlockSpec((tm, GROUP_W), lambda i: (i, 0))
    const = pl.Buffered(1)
    cast_spec = _row_chunk_spec(side_cast, n_tok // tm, lambda i: i)
    return pl.pallas_call(
        _mix_out_kernel,
        name="mix_out",
        grid=(n_tok // tm,),
        in_specs=[row512, row512,
                  pl.BlockSpec((tm, D_MODEL), lambda i: (i, GATE_BLOCK0)),
                  pl.BlockSpec((tm, D_MODEL), lambda i: (i, GATE_BLOCK0 + 1)),
                  pl.BlockSpec((1, 2 * D_MODEL), lambda i: (0, 0)),
                  pl.BlockSpec((tm, D_MODEL), lambda i: (i, 0)),
                  pl.BlockSpec((GROUP_W, D_MODEL), lambda i: (0, 0), pipeline_mode=const),
                  pl.BlockSpec((GROUP_W, D_MODEL), lambda i: (0, 0), pipeline_mode=const),
                  pl.BlockSpec((D_MODEL, D_MODEL), lambda i: (0, 0), pipeline_mode=const),
                  cast_spec],
        out_specs=[pl.BlockSpec((tm, D_MODEL), lambda i: (i, 0)), cast_spec],
        out_shape=[jax.ShapeDtypeStruct((n_tok, D_MODEL), F32), jax.ShapeDtypeStruct(side_cast.shape, BF16)],
        compiler_params=pltpu.CompilerParams(
            dimension_semantics=("parallel",), vmem_limit_bytes=VMEM_LIMIT),
    )(oa, ob, proj, proj, b_gate, x2, pa, pb, wo, side_cast)


def _ffn_kernel(h_ref, g_ref, wu_ref, wd_ref, o_ref, hn_ref):
    def partial_out(hn):
        u = jnp.maximum(jnp.dot(hn, wu_ref[...], preferred_element_type=F32), 0.0)
        return jnp.dot((u * u).astype(BF16), wd_ref[...], preferred_element_type=F32)

    @pl.when(pl.program_id(1) == 0)
    def _():
        h = h_ref[...]
        hn = _rms(h, g_ref[...]).astype(BF16)
        hn_ref[...] = hn
        o_ref[...] = h + partial_out(hn)

    @pl.when(pl.program_id(1) > 0)
    def _():
        o_ref[...] += partial_out(hn_ref[...])


def _ffn(h, gain, wu, wd, tm=512, tf=2048):
    n_tok = h.shape[0]
    return pl.pallas_call(
        _ffn_kernel,
        name="ffn",
        grid=(n_tok // tm, D_FF // tf),
        in_specs=[pl.BlockSpec((tm, D_MODEL), lambda i, f: (i, 0)),
                  pl.BlockSpec((1, D_MODEL), lambda i, f: (0, 0)),
                  pl.BlockSpec((D_MODEL, tf), lambda i, f: (0, f)),
                  pl.BlockSpec((tf, D_MODEL), lambda i, f: (f, 0))],
        out_specs=pl.BlockSpec((tm, D_MODEL), lambda i, f: (i, 0)),
        out_shape=jax.ShapeDtypeStruct((n_tok, D_MODEL), F32),
        scratch_shapes=[pltpu.VMEM((tm, D_MODEL), BF16)],
        compiler_params=pltpu.CompilerParams(
            dimension_semantics=("parallel", "arbitrary"), vmem_limit_bytes=VMEM_LIMIT),
    )(h, gain, wu, wd)


def kernel(x, norm_mix, w_in, b_gate, q_norm_a, k_norm_a, q_norm_b, k_norm_b, rpb_b,
           w_proj_a, w_proj_b, w_out, norm_ffn, w_up, w_down):
    batch, seq, d_model = x.shape
    assert d_model == D_MODEL and seq % (DILATED_PAIRS[-1][1] * Q_BLK) == 0
    for win, dil in DILATED_PAIRS:
        assert win // (2 * dil) == HALF_WIN
    depth = norm_mix.shape[0]
    cos, sin = _rope_tables(seq)
    h = x.reshape(batch * seq, D_MODEL)
    for l in range(depth):
        gains_a = jnp.stack([q_norm_a[l] * SCALE, k_norm_a[l]]).astype(F32)[:, None, :]
        rope_c = cos[None] * gains_a
        rope_s = sin[None] * jnp.roll(gains_a, HEAD_DIM // 2, axis=-1)
        gains_b = jnp.stack([q_norm_b[l] * SCALE, k_norm_b[l]]).astype(F32)[:, None, :]
        gain_mix = norm_mix[l].reshape(1, D_MODEL)
        proj_first, w_in_l = _in_proj_first(h, gain_mix, w_in[l], rope_c, rope_s, gains_b)
        proj, w_up_l = _in_proj(h, gain_mix, w_in_l, rope_c, rope_s, gains_b, seq, proj_first,
                                side_casts=(w_up[l],))
        proj3 = proj.reshape(batch, seq, D_PROJ)
        oa, pa_l, pb_l, wo_l = _attn_a(proj3, _score_bound(gains_a[0], gains_a[1]),
                                       side_casts=(w_proj_a[l], w_proj_b[l], w_out[l]))
        ob = _attn_b(proj3, _neighbourhood_bias_tiles(rpb_b[l]), _score_bound(gains_b[0], gains_b[1]), rpb_b[l])
        h, w_down_l = _mix_out(oa.reshape(batch * seq, GROUP_W), ob.reshape(batch * seq, GROUP_W), proj,
                               b_gate[l].reshape(1, 2 * D_MODEL), h,
                               pa_l, pb_l, wo_l, side_cast=w_down[l])
        h = _ffn(h, norm_ffn[l].reshape(1, D_MODEL), w_up_l, w_down_l)
    return h.reshape(batch, seq, D_MODEL)
```

```python
import functools

import numpy as np
import jax
import jax.numpy as jnp
from jax import lax
from jax.experimental import pallas as pl
from jax.experimental.pallas import tpu as pltpu

D_MODEL = 2048
HEAD_DIM = 128
N_HEADS = 16
N_HEADS_A = 12
N_HEADS_B = 4
DILATED_PAIRS = ((128, 1), (512, 4), (2048, 16))
N_GROUPS_A = len(DILATED_PAIRS)
HEADS_PER_GROUP = 4
GROUP_W = HEADS_PER_GROUP * HEAD_DIM
GRID_W = 64
WIN_R = 8
WIN_C = 16
QKV_W = N_HEADS * HEAD_DIM
D_FF = 4 * D_MODEL
ROPE_THETA = 10000.0
EPS = 1e-6
NEG_INF = -1e30
SCALE = HEAD_DIM ** -0.5

D_PROJ = 3 * QKV_W + 2 * D_MODEL
B_BLOCK0 = N_HEADS_A * HEAD_DIM // GROUP_W
GATE_BLOCK0 = 3 * QKV_W // D_MODEL

VMEM_LIMIT = 56 * 1024 * 1024

F32 = jnp.float32
BF16 = jnp.bfloat16


def _rope_tables(seq):
    pos = np.arange(seq, dtype=np.float64)
    inv = ROPE_THETA ** (-np.arange(0, HEAD_DIM, 2, dtype=np.float64) / HEAD_DIM)
    ang = pos[:, None] * inv[None, :]
    cos = np.concatenate([np.cos(ang), np.cos(ang)], axis=-1)
    sin = np.concatenate([-np.sin(ang), np.sin(ang)], axis=-1)
    return jnp.asarray(cos, F32), jnp.asarray(sin, F32)


def _rms(x, g):
    ms = jnp.mean(x * x, axis=-1, keepdims=True)
    return (x * lax.rsqrt(ms + EPS)) * g


QK_ROW_CHUNKS = 4


def _qk_norm_store(acc, rows, first_head, rc_ref, rs_ref, gb_ref, o_ref):
    for c in range(acc.shape[1] // HEAD_DIM):
        sl = slice(c * HEAD_DIM, (c + 1) * HEAD_DIM)
        y = acc[:, sl]
        inv_rms = lax.rsqrt(jnp.mean(y * y, axis=-1, keepdims=True) + EPS)
        if first_head + c < N_HEADS_A:
            y = y * rc_ref[rows, :] + pltpu.roll(y, HEAD_DIM // 2, 1) * rs_ref[rows, :]
        else:
            y = y * gb_ref[...]
        o_ref[rows, sl] = (y * inv_rms).astype(o_ref.dtype)


def _in_proj_first_kernel(x_ref, g_ref, w_ref, rc_ref, rs_ref, gb_ref, o_ref, wb_ref, xn_ref):
    j = pl.program_id(0)
    i = pl.program_id(1)
    heads_per_tile = w_ref.shape[1] // HEAD_DIM

    @pl.when(i == 0)
    def _():
        wb_ref[...] = w_ref[...].astype(BF16)

    def qk_tile(xn, first_head):
        rows_per_chunk = xn.shape[0] // QK_ROW_CHUNKS
        for c in range(QK_ROW_CHUNKS):
            rows = slice(c * rows_per_chunk, (c + 1) * rows_per_chunk)
            acc = jnp.dot(xn[rows, :], wb_ref[...], preferred_element_type=F32)
            _qk_norm_store(acc, rows, first_head, rc_ref, rs_ref, gb_ref, o_ref)

    @pl.when(j == 0)
    def _():
        xn = _rms(x_ref[...], g_ref[...]).astype(BF16)
        xn_ref[i] = xn
        qk_tile(xn, 0)

    @pl.when(j == 2)
    def _():
        qk_tile(xn_ref[i], 0)

    @pl.when((j == 1) | (j == 3))
    def _():
        qk_tile(xn_ref[i], heads_per_tile)

    @pl.when(j >= 4)
    def _():
        o_ref[...] = jnp.dot(xn_ref[i], wb_ref[...], preferred_element_type=F32).astype(o_ref.dtype)


def _in_proj_kernel(*refs, n_casts):
    x_ref, g_ref, w_ref, rc_ref, rs_ref, gb_ref = refs[:6]
    cast_srcs = refs[7:7 + n_casts]
    o_ref = refs[7 + n_casts]
    cast_dsts = refs[8 + n_casts:8 + 2 * n_casts]
    xn_ref = refs[8 + 2 * n_casts]
    j = pl.program_id(1)

    @pl.when(pl.program_id(0) * pl.num_programs(1) + j < IN_PROJ_CAST_CHUNKS)
    def _():
        for src, dst in zip(cast_srcs, cast_dsts):
            dst[...] = src[...].astype(dst.dtype)

    def project(xn):
        return jnp.dot(xn, w_ref[...], preferred_element_type=F32)

    def qk_tile(xn):
        rows_per_chunk = xn.shape[0] // QK_ROW_CHUNKS
        for c in range(QK_ROW_CHUNKS):
            rows = slice(c * rows_per_chunk, (c + 1) * rows_per_chunk)
            _qk_norm_store(project(xn[rows, :]), rows, 0, rc_ref, rs_ref, gb_ref, o_ref)

    @pl.when(j == 0)
    def _():
        xn = _rms(x_ref[...], g_ref[...]).astype(BF16)
        xn_ref[...] = xn
        qk_tile(xn)

    @pl.when(j == 1)
    def _():
        qk_tile(xn_ref[...])

    @pl.when(j >= 2)
    def _():
        o_ref[...] = project(xn_ref[...]).astype(o_ref.dtype)


def _row_chunk_spec(a, n_chunks, step):
    assert a.shape[0] % (n_chunks * 16) == 0
    return pl.BlockSpec((a.shape[0] // n_chunks, a.shape[1]),
                        lambda *ids: (jnp.minimum(step(*ids), n_chunks - 1), 0))


IN_PROJ_CAST_CHUNKS = 64


IN_PROJ_TM = 1024
IN_PROJ_FIRST_TILES = 2


def _in_proj_first(x2, gain, w_f32, rope_c, rope_s, gain_b, seq):
    n_tok = x2.shape[0]
    tm, tn = IN_PROJ_TM, QKV_W // 2
    n_first = IN_PROJ_FIRST_TILES
    assert n_first <= seq // tm

    def qk(j):
        return jnp.minimum(j // 2, 1)

    return pl.pallas_call(
        _in_proj_first_kernel,
        name="in_proj_first",
        grid=(D_PROJ // tn, n_first),
        in_specs=[
            pl.BlockSpec((tm, D_MODEL), lambda j, i: (jnp.where(j == 0, i, n_first - 1), 0),
                         pipeline_mode=pl.Buffered(1)),
            pl.BlockSpec((1, D_MODEL), lambda j, i: (0, 0)),
            pl.BlockSpec((D_MODEL, tn), lambda j, i: (0, j)),
            pl.BlockSpec((None, tm, HEAD_DIM), lambda j, i: (qk(j), i, 0)),
            pl.BlockSpec((None, tm, HEAD_DIM), lambda j, i: (qk(j), i, 0)),
            pl.BlockSpec((None, 1, HEAD_DIM), lambda j, i: (qk(j), 0, 0)),
        ],
        out_specs=[pl.BlockSpec((tm, tn), lambda j, i: (i, j)), pl.BlockSpec((D_MODEL, tn), lambda j, i: (0, j))],
        out_shape=[jax.ShapeDtypeStruct((n_tok, D_PROJ), BF16), jax.ShapeDtypeStruct(w_f32.shape, BF16)],
        scratch_shapes=[pltpu.VMEM((n_first, tm, D_MODEL), BF16)],
        compiler_params=pltpu.CompilerParams(
            dimension_semantics=("arbitrary", "arbitrary"), vmem_limit_bytes=VMEM_LIMIT),
    )(x2, gain, w_f32, rope_c, rope_s, gain_b)


def _in_proj(x2, gain, w, rope_c, rope_s, gain_b, seq, proj_first, side_casts):
    n_tok = x2.shape[0]
    tm = IN_PROJ_TM
    tn = QKV_W
    assert seq % tm == 0 and D_PROJ % tn == 0
    tiles_per_seq = seq // tm
    first = IN_PROJ_FIRST_TILES
    n_i, n_j = n_tok // tm - first, D_PROJ // tn
    assert n_i * n_j >= IN_PROJ_CAST_CHUNKS

    def qk(j):
        return jnp.minimum(j, 1)

    cast_specs = [_row_chunk_spec(a, IN_PROJ_CAST_CHUNKS, lambda i, j: i * n_j + j) for a in side_casts]
    return pl.pallas_call(
        functools.partial(_in_proj_kernel, n_casts=len(side_casts)),
        name="in_proj",
        grid=(n_i, n_j),
        in_specs=[
            pl.BlockSpec((tm, D_MODEL), lambda i, j: (i + first, 0)),
            pl.BlockSpec((1, D_MODEL), lambda i, j: (0, 0)),
            pl.BlockSpec((D_MODEL, tn), lambda i, j: (0, j)),
            pl.BlockSpec((None, tm, HEAD_DIM), lambda i, j: (qk(j), (i + first) % tiles_per_seq, 0)),
            pl.BlockSpec((None, tm, HEAD_DIM), lambda i, j: (qk(j), (i + first) % tiles_per_seq, 0)),
            pl.BlockSpec((None, 1, HEAD_DIM), lambda i, j: (qk(j), 0, 0)),
            pl.BlockSpec(memory_space=pl.ANY),
        ] + cast_specs,
        out_specs=[pl.BlockSpec((tm, tn), lambda i, j: (i + first, j))] + cast_specs,
        out_shape=[jax.ShapeDtypeStruct((n_tok, D_PROJ), BF16)]
        + [jax.ShapeDtypeStruct(a.shape, BF16) for a in side_casts],
        scratch_shapes=[pltpu.VMEM((tm, D_MODEL), BF16)],
        input_output_aliases={6: 0},
        compiler_params=pltpu.CompilerParams(
            dimension_semantics=("arbitrary", "arbitrary"), vmem_limit_bytes=VMEM_LIMIT),
    )(x2, gain, w, rope_c, rope_s, gain_b, proj_first, *side_casts)


Q_BLK = 128
HALF_WIN = 64
A_WIN = 2 * Q_BLK
EXP_SPAN_LIMIT = 80.0


def _band_masks():
    rel = np.arange(A_WIN)[None, :] - np.arange(Q_BLK)[:, None]
    tiles = [np.where(np.abs(rel - off) <= HALF_WIN, 0.0, NEG_INF) for off in (0, HALF_WIN, 2 * HALF_WIN)]
    return jnp.asarray(np.stack(tiles), F32)


def _attn_a_kernel(*refs, seq, dils, n_casts):
    n_g = len(dils)
    qkv_refs = refs[:3 * n_g]
    bound_ref, mask_ref = refs[3 * n_g:3 * n_g + 2]
    rest = refs[3 * n_g + 2:]
    cast_srcs, o_ref, cast_dsts = rest[:n_casts], rest[n_casts], rest[n_casts + 1:2 * n_casts + 1]
    qn_ref, kn_ref, vn_ref, f_ref, og_ref, lg_ref, band_ref = rest[2 * n_casts + 1:]
    n_blk = seq // Q_BLK
    for src, dst in zip(cast_srcs, cast_dsts):
        dst[...] = src[...].astype(dst.dtype)
    bound = bound_ref[0]
    band_ref[...] = mask_ref[...] - bound
    vn_ref[:, :, HEAD_DIM:] = jnp.ones((n_g, seq, HEAD_DIM), BF16)

    def first_stride(dil):
        return 4 if dil % 8 == 0 else dil

    slabs = iter(range(f_ref.shape[0]))
    stage = {g: tuple(f_ref.at[next(slabs)] for _ in range(3)) for g, dil in enumerate(dils) if dil > 1}
    mid = {g: tuple(f_ref.at[next(slabs)] for _ in range(3))
           for g, dil in enumerate(dils) if first_stride(dil) != dil}

    def block_rows(n):
        return pl.ds(pl.multiple_of(n * Q_BLK, Q_BLK), Q_BLK)

    def token_rows(n, dil):
        if dil == 1:
            return block_rows(n)
        sub_len = seq // dil
        base = n * Q_BLK
        r = base // sub_len
        return pl.ds(r + (base - r * sub_len) * dil, Q_BLK, stride=dil)

    def stage_in(n, c):
        rows = block_rows(n)
        for g, dil in enumerate(dils):
            q_ref, k_ref, v_ref = qkv_refs[3 * g:3 * g + 3]
            if dil == 1:
                vn_ref[g, rows, :HEAD_DIM] = v_ref[rows, :]
            else:
                fq, fk, fv = stage[g]
                fq[rows, :] = q_ref[rows, :].astype(F32)
                fk[rows, :] = k_ref[rows, :].astype(F32)
                fv[rows, :] = v_ref[rows, :].astype(F32)
        return c

    lax.fori_loop(0, n_blk, stage_in, 0, unroll=2)

    def put_operands(g, dst, q, k, v):
        qn_ref[g, dst, :] = q.astype(BF16)
        kn_ref[g, dst, :] = k.astype(BF16)
        vn_ref[g, dst, :HEAD_DIM] = v.astype(BF16)

    def gather(n, c):
        dst = block_rows(n)
        for g, dil in enumerate(dils):
            if dil > 1:
                src = token_rows(n, first_stride(dil))
                q, k, v = (slab[src, :] for slab in stage[g])
                if g in mid:
                    for slab, x in zip(mid[g], (q, k, v)):
                        slab[dst, :] = x
                else:
                    put_operands(g, dst, q, k, v)
        return c

    lax.fori_loop(0, n_blk, gather, 0, unroll=2)

    def second_step_rows(n, g):
        dil = dils[g]
        s1 = first_stride(dil)
        sub_len = seq // dil
        base = n * Q_BLK
        r = base // sub_len
        start = (r % s1) * (seq // s1) + r // s1 + (base - r * sub_len) * (dil // s1)
        return pl.ds(start, Q_BLK, stride=dil // s1)

    def gather_second(n, c):
        for g in mid:
            put_operands(g, block_rows(n), *(slab[second_step_rows(n, g), :] for slab in mid[g]))
        return c

    if mid:
        lax.fori_loop(0, n_blk, gather_second, 0, unroll=2)

    def block(n, c, row_max):
        q0 = pl.multiple_of(n * Q_BLK, Q_BLK)
        for g, dil in enumerate(dils):
            sub_len = seq // dil
            win = min(A_WIN, sub_len)
            sub0 = (n // (sub_len // Q_BLK)) * sub_len
            q_loc = q0 - sub0
            k_loc = jnp.clip(q_loc - HALF_WIN, 0, sub_len - win)
            k0 = pl.multiple_of(sub0 + k_loc, HALF_WIN)
            band = band_ref[(q_loc - k_loc) // HALF_WIN, :, :win]
            q_src, k_src = (qkv_refs[3 * g], qkv_refs[3 * g + 1]) if dil == 1 else (qn_ref.at[g], kn_ref.at[g])
            q = q_src[pl.ds(q0, Q_BLK), :]
            k = k_src[pl.ds(k0, win), :]
            s = lax.dot_general(q, k, (((1,), (1,)), ((), ())), preferred_element_type=F32) + band
            if row_max:
                m = jnp.max(s, axis=-1, keepdims=True)
                s = s - m
            p = jnp.exp(s).astype(BF16)
            acc = jnp.dot(p, vn_ref[g, pl.ds(k0, win), :], preferred_element_type=F32)
            num, denom = acc[:, :HEAD_DIM], acc[:, HEAD_DIM:]
            if row_max:
                first, second = num * (1.0 / denom), jnp.log(denom) + m
            else:
                first, second = num, denom
            if g in mid:
                mid[g][0][pl.ds(q0, Q_BLK), :] = first
                mid[g][1][pl.ds(q0, Q_BLK), :] = second
            else:
                dst = token_rows(n, dil)
                og_ref.at[g][dst, :] = first
                lg_ref.at[g][dst, :] = second
        return c

    def scatter_first(n, c):
        for g in mid:
            dst = second_step_rows(n, g)
            stage[g][0][dst, :] = mid[g][0][block_rows(n), :]
            stage[g][1][dst, :] = mid[g][1][block_rows(n), :]
        return c

    def scatter_second(n, c):
        for g in mid:
            dst = token_rows(n, first_stride(dils[g]))
            og_ref.at[g][dst, :] = stage[g][0][block_rows(n), :]
            lg_ref.at[g][dst, :] = stage[g][1][block_rows(n), :]
        return c

    def combine(n, c, row_max):
        rows = block_rows(n)
        if row_max:
            lses = [lg_ref[g, rows, :] for g in range(n_g)]
            mx = functools.reduce(jnp.maximum, lses)
            es = [jnp.exp(x - mx) for x in lses]
            inv = 1.0 / functools.reduce(lambda a, b: a + b, es)
            oa = (es[0] * inv) * og_ref[0, rows, :]
            for g in range(1, n_g):
                oa = oa + (es[g] * inv) * og_ref[g, rows, :]
        else:
            num = functools.reduce(lambda a, b: a + b, [og_ref[g, rows, :] for g in range(n_g)])
            den = functools.reduce(lambda a, b: a + b, [lg_ref[g, rows, :] for g in range(n_g)])
            oa = num * (1.0 / den)
        o_ref[rows, :] = oa.astype(o_ref.dtype)
        return c

    def attend(row_max, unroll):
        lax.fori_loop(0, n_blk, functools.partial(block, row_max=row_max), 0, unroll=unroll)
        if mid:
            lax.fori_loop(0, n_blk, scatter_first, 0, unroll=2)
            lax.fori_loop(0, n_blk, scatter_second, 0, unroll=2)
        lax.fori_loop(0, n_blk, functools.partial(combine, row_max=row_max), 0, unroll=2)

    @pl.when(2.0 * bound < EXP_SPAN_LIMIT)
    def _():
        attend(row_max=False, unroll=16)

    @pl.when(2.0 * bound >= EXP_SPAN_LIMIT)
    def _():
        attend(row_max=True, unroll=2)


def _score_bound(gq, gk):
    return 1.01 * HEAD_DIM * jnp.max(jnp.abs(gq)) * jnp.max(jnp.abs(gk))


def _attn_a(proj3, bound, side_casts):
    batch, seq, _ = proj3.shape
    dils = tuple(d for _, d in DILATED_PAIRS)
    n_g = len(dils)
    for d in dils:
        assert d % 8 != 0 or (d % 4 == 0 and (d // 4) % 8 != 0)
    n_slabs = 3 * sum((d > 1) + (d % 8 == 0) for d in dils)
    masks = _band_masks()
    qkv_specs = [pl.BlockSpec((None, seq, HEAD_DIM),
                              lambda b, s, t=t, g=g: (b, 0, t * N_HEADS + g * HEADS_PER_GROUP + s))
                 for g in range(n_g) for t in range(3)]
    cast_specs = [_row_chunk_spec(a, batch * HEADS_PER_GROUP, lambda b, s: b * HEADS_PER_GROUP + s)
                  for a in side_casts]
    return pl.pallas_call(
        functools.partial(_attn_a_kernel, seq=seq, dils=dils, n_casts=len(side_casts)),
        name="attn_a",
        grid=(batch, HEADS_PER_GROUP),
        in_specs=qkv_specs + [pl.BlockSpec(memory_space=pltpu.SMEM),
                              pl.BlockSpec(masks.shape, lambda b, s: (0, 0, 0), pipeline_mode=pl.Buffered(1))]
        + cast_specs,
        out_specs=[pl.BlockSpec((None, seq, HEAD_DIM), lambda b, s: (b, 0, s))] + cast_specs,
        out_shape=[jax.ShapeDtypeStruct((batch, seq, GROUP_W), BF16)]
        + [jax.ShapeDtypeStruct(a.shape, BF16) for a in side_casts],
        scratch_shapes=[pltpu.VMEM((n_g, seq, HEAD_DIM), BF16)] * 2
        + [pltpu.VMEM((n_g, seq, 2 * HEAD_DIM), BF16)]
        + [pltpu.VMEM((n_slabs, seq, HEAD_DIM), F32)]
        + [pltpu.VMEM((n_g, seq, HEAD_DIM), F32)] * 2
        + [pltpu.VMEM(masks.shape, F32)],
        compiler_params=pltpu.CompilerParams(
            dimension_semantics=("parallel", "arbitrary"), vmem_limit_bytes=VMEM_LIMIT),
    )(*([proj3] * (3 * n_g)), bound.reshape(1).astype(F32), masks, *side_casts)


B_QROWS = 4
B_KROWS = B_QROWS + WIN_R
B_NQ = B_QROWS * GRID_W
B_NK = B_KROWS * GRID_W


def _window_row_offsets(rows):
    n_grp = rows // B_QROWS
    assert rows % B_QROWS == 0 and n_grp >= 3
    masked = 2 * WIN_R - 1
    table = []
    for i in (0, 1, n_grp - 1):
        r0 = i * B_QROWS
        ws = int(np.clip(r0 - WIN_R // 2, 0, rows - B_KROWS))
        per_q = []
        for rq in range(B_QROWS):
            r = r0 + rq
            rs = int(np.clip(r - WIN_R // 2, 0, rows - WIN_R))
            per_q.append([ws + jr - r + WIN_R - 1 if rs <= ws + jr < rs + WIN_R else masked
                          for jr in range(B_KROWS)])
        table.append(per_q)
    return table


def _attn_b_kernel(shift_ref, q_ref, k_ref, v_ref, tile_ref, o_ref, bias_ref, *, rows):
    n_grp = rows // B_QROWS
    shift = shift_ref[0]
    span = shift_ref[1]

    @pl.when(pl.program_id(0) == 0)
    def _():
        right = lax.broadcasted_iota(jnp.int32, (GRID_W, 2 * GRID_W), 1) >= GRID_W
        for case, per_q in enumerate(_window_row_offsets(rows)):
            for rq, offs in enumerate(per_q):
                for h in range(N_HEADS_B):
                    for j in range(0, B_KROWS, 2):
                        pair = jnp.where(right, tile_ref[h, offs[j + 1]], tile_ref[h, offs[j]])
                        bias_ref[h, case, rq * GRID_W:(rq + 1) * GRID_W,
                                 j * GRID_W:(j + 2) * GRID_W] = pair - shift

    def group(i, c, row_max):
        r0 = i * B_QROWS
        ws = jnp.clip(r0 - WIN_R // 2, 0, rows - B_KROWS)
        case = jnp.where(i == 0, 0, jnp.where(i == n_grp - 1, 2, 1))
        q0 = pl.multiple_of(r0 * GRID_W, B_NQ)
        k0 = pl.multiple_of(ws * GRID_W, GRID_W)
        for h in range(N_HEADS_B):
            sl = slice(h * HEAD_DIM, (h + 1) * HEAD_DIM)
            q = q_ref[pl.ds(q0, B_NQ), sl]
            k = k_ref[pl.ds(k0, B_NK), sl]
            v = v_ref[pl.ds(k0, B_NK), sl]
            s = lax.dot_general(q, k, (((1,), (1,)), ((), ())), preferred_element_type=F32)
            s = s + bias_ref[h, case]
            if row_max:
                s = s - jnp.max(s, axis=-1, keepdims=True)
            p = jnp.exp(s)
            l = jnp.sum(p, axis=-1, keepdims=True)
            acc = jnp.dot(p.astype(BF16), v, preferred_element_type=F32)
            o_ref[pl.ds(q0, B_NQ), sl] = (acc * (1.0 / l)).astype(o_ref.dtype)
        return c

    @pl.when(span < EXP_SPAN_LIMIT)
    def _():
        lax.fori_loop(0, n_grp, functools.partial(group, row_max=False), 0, unroll=4)

    @pl.when(span >= EXP_SPAN_LIMIT)
    def _():
        lax.fori_loop(0, n_grp, functools.partial(group, row_max=True), 0)


def _neighbourhood_bias_tiles(rpb):
    n_h = rpb.shape[0]
    c = np.arange(GRID_W)
    dc = np.clip(c[None, :] - c[:, None], -(WIN_C - 1), WIN_C - 1) + (WIN_C - 1)
    col_start = np.clip(c - WIN_C // 2, 0, GRID_W - WIN_C)
    col_ok = (c[None, :] >= col_start[:, None]) & (c[None, :] < col_start[:, None] + WIN_C)
    col_pick = np.zeros((2 * WIN_C - 1, GRID_W * GRID_W), np.float32)
    col_pick[dc.ravel(), np.arange(GRID_W * GRID_W)] = 1.0
    tiles = jnp.einsum("hab,bc->hac", rpb.astype(F32), col_pick, precision=lax.Precision.HIGHEST)
    tiles = jnp.where(col_ok.reshape(-1)[None, None], tiles, NEG_INF).reshape(n_h, -1, GRID_W, GRID_W)
    tiles = jnp.concatenate([tiles, jnp.full((n_h, 1, GRID_W, GRID_W), NEG_INF, F32)], axis=1)
    return jnp.concatenate([tiles, tiles], axis=-1)


def _attn_b(proj3, tiles, qk_bound, rpb):
    batch, seq, _ = proj3.shape
    rows = seq // GRID_W
    hi, lo = jnp.max(rpb).astype(F32), jnp.min(rpb).astype(F32)
    shift = jnp.stack([qk_bound + hi, 2.0 * qk_bound + (hi - lo)]).astype(F32)

    def qkv_spec(t):
        return pl.BlockSpec((None, seq, GROUP_W), lambda b: (b, 0, t * (QKV_W // GROUP_W) + B_BLOCK0))

    tile_spec = pl.BlockSpec(tiles.shape, lambda b: (0, 0, 0, 0), pipeline_mode=pl.Buffered(1))
    return pl.pallas_call(
        functools.partial(_attn_b_kernel, rows=rows),
        name="attn_b",
        grid=(batch,),
        in_specs=[pl.BlockSpec(memory_space=pltpu.SMEM), qkv_spec(0), qkv_spec(1), qkv_spec(2), tile_spec],
        out_specs=pl.BlockSpec((None, seq, GROUP_W), lambda b: (b, 0, 0)),
        out_shape=jax.ShapeDtypeStruct((batch, seq, GROUP_W), BF16),
        scratch_shapes=[pltpu.VMEM((N_HEADS_B, 3, B_NQ, B_NK), F32)],
        compiler_params=pltpu.CompilerParams(
            dimension_semantics=("arbitrary",), vmem_limit_bytes=VMEM_LIMIT),
    )(shift, proj3, proj3, proj3, tiles)


def _mix_out_kernel(oa_ref, ob_ref, ga_ref, gb_ref, bg_ref, x_ref, pa_ref, pb_ref, wo_ref, cast_src,
                    h_ref, cast_dst):
    cast_dst[...] = cast_src[...].astype(cast_dst.dtype)
    ya = jnp.dot(oa_ref[...], pa_ref[...], preferred_element_type=F32)
    yb = jnp.dot(ob_ref[...], pb_ref[...], preferred_element_type=F32)
    ga = jax.nn.sigmoid(ga_ref[...].astype(F32) + bg_ref[:, :D_MODEL])
    gb = jax.nn.sigmoid(gb_ref[...].astype(F32) + bg_ref[:, D_MODEL:])
    mixed = (ga * ya + gb * yb).astype(BF16)
    h_ref[...] = x_ref[...] + jnp.dot(mixed, wo_ref[...], preferred_element_type=F32)


def _mix_out(oa, ob, proj, b_gate, x2, pa, pb, wo, side_cast, tm=512):
    n_tok = x2.shape[0]
    row512 = pl.BlockSpec((tm, GROUP_W), lambda i: (i, 0))
    const = pl.Buffered(1)
    cast_spec = _row_chunk_spec(side_cast, n_tok // tm, lambda i: i)
    return pl.pallas_call(
        _mix_out_kernel,
        name="mix_out",
        grid=(n_tok // tm,),
        in_specs=[row512, row512,
                  pl.BlockSpec((tm, D_MODEL), lambda i: (i, GATE_BLOCK0)),
                  pl.BlockSpec((tm, D_MODEL), lambda i: (i, GATE_BLOCK0 + 1)),
                  pl.BlockSpec((1, 2 * D_MODEL), lambda i: (0, 0)),
                  pl.BlockSpec((tm, D_MODEL), lambda i: (i, 0)),
                  pl.BlockSpec((GROUP_W, D_MODEL), lambda i: (0, 0), pipeline_mode=const),
                  pl.BlockSpec((GROUP_W, D_MODEL), lambda i: (0, 0), pipeline_mode=const),
                  pl.BlockSpec((D_MODEL, D_MODEL), lambda i: (0, 0), pipeline_mode=const),
                  cast_spec],
        out_specs=[pl.BlockSpec((tm, D_MODEL), lambda i: (i, 0)), cast_spec],
        out_shape=[jax.ShapeDtypeStruct((n_tok, D_MODEL), F32), jax.ShapeDtypeStruct(side_cast.shape, BF16)],
        compiler_params=pltpu.CompilerParams(
            dimension_semantics=("parallel",), vmem_limit_bytes=VMEM_LIMIT),
    )(oa, ob, proj, proj, b_gate, x2, pa, pb, wo, side_cast)


def _ffn_kernel(h_ref, g_ref, wu_ref, wd_ref, o_ref, hn_ref):
    def partial_out(hn):
        u = jnp.maximum(jnp.dot(hn, wu_ref[...], preferred_element_type=F32), 0.0)
        return jnp.dot((u * u).astype(BF16), wd_ref[...], preferred_element_type=F32)

    @pl.when(pl.program_id(1) == 0)
    def _():
        h = h_ref[...]
        hn = _rms(h, g_ref[...]).astype(BF16)
        hn_ref[...] = hn
        o_ref[...] = h + partial_out(hn)

    @pl.when(pl.program_id(1) > 0)
    def _():
        o_ref[...] += partial_out(hn_ref[...])


def _ffn(h, gain, wu, wd, tm=512, tf=2048):
    n_tok = h.shape[0]
    return pl.pallas_call(
        _ffn_kernel,
        name="ffn",
        grid=(n_tok // tm, D_FF // tf),
        in_specs=[pl.BlockSpec((tm, D_MODEL), lambda i, f: (i, 0)),
                  pl.BlockSpec((1, D_MODEL), lambda i, f: (0, 0)),
                  pl.BlockSpec((D_MODEL, tf), lambda i, f: (0, f)),
                  pl.BlockSpec((tf, D_MODEL), lambda i, f: (f, 0))],
        out_specs=pl.BlockSpec((tm, D_MODEL), lambda i, f: (i, 0)),
        out_shape=jax.ShapeDtypeStruct((n_tok, D_MODEL), F32),
        scratch_shapes=[pltpu.VMEM((tm, D_MODEL), BF16)],
        compiler_params=pltpu.CompilerParams(
            dimension_semantics=("parallel", "arbitrary"), vmem_limit_bytes=VMEM_LIMIT),
    )(h, gain, wu, wd)


def kernel(x, norm_mix, w_in, b_gate, q_norm_a, k_norm_a, q_norm_b, k_norm_b, rpb_b,
           w_proj_a, w_proj_b, w_out, norm_ffn, w_up, w_down):
    batch, seq, d_model = x.shape
    assert d_model == D_MODEL and seq % (DILATED_PAIRS[-1][1] * Q_BLK) == 0
    for win, dil in DILATED_PAIRS:
        assert win // (2 * dil) == HALF_WIN
    depth = norm_mix.shape[0]
    cos, sin = _rope_tables(seq)
    h = x.reshape(batch * seq, D_MODEL)
    for l in range(depth):
        gains_a = jnp.stack([q_norm_a[l] * SCALE, k_norm_a[l]]).astype(F32)[:, None, :]
        rope_c = cos[None] * gains_a
        rope_s = sin[None] * jnp.roll(gains_a, HEAD_DIM // 2, axis=-1)
        gains_b = jnp.stack([q_norm_b[l] * SCALE, k_norm_b[l]]).astype(F32)[:, None, :]
        gain_mix = norm_mix[l].reshape(1, D_MODEL)
        proj_first, w_in_l = _in_proj_first(h, gain_mix, w_in[l], rope_c, rope_s, gains_b, seq)
        proj, w_up_l = _in_proj(h, gain_mix, w_in_l, rope_c, rope_s, gains_b, seq, proj_first,
                                side_casts=(w_up[l],))
        proj3 = proj.reshape(batch, seq, D_PROJ)
        oa, pa_l, pb_l, wo_l = _attn_a(proj3, _score_bound(gains_a[0], gains_a[1]),
                                       side_casts=(w_proj_a[l], w_proj_b[l], w_out[l]))
        ob = _attn_b(proj3, _neighbourhood_bias_tiles(rpb_b[l]), _score_bound(gains_b[0], gains_b[1]), rpb_b[l])
        h, w_down_l = _mix_out(oa.reshape(batch * seq, GROUP_W), ob.reshape(batch * seq, GROUP_W), proj,
                               b_gate[l].reshape(1, 2 * D_MODEL), h,
                               pa_l, pb_l, wo_l, side_cast=w_down[l])
        h = _ffn(h, norm_ffn[l].reshape(1, D_MODEL), w_up_l, w_down_l)
    return h.reshape(batch, seq, D_MODEL)
```

```python
import functools

import numpy as np
import jax
import jax.numpy as jnp
from jax import lax
from jax.experimental import pallas as pl
from jax.experimental.pallas import tpu as pltpu

D_MODEL = 2048
HEAD_DIM = 128
N_HEADS = 16
N_HEADS_A = 12
N_HEADS_B = 4
DILATED_PAIRS = ((128, 1), (512, 4), (2048, 16))
N_GROUPS_A = len(DILATED_PAIRS)
HEADS_PER_GROUP = 4
GROUP_W = HEADS_PER_GROUP * HEAD_DIM
GRID_W = 64
WIN_R = 8
WIN_C = 16
QKV_W = N_HEADS * HEAD_DIM
D_FF = 4 * D_MODEL
ROPE_THETA = 10000.0
EPS = 1e-6
NEG_INF = -1e30
SCALE = HEAD_DIM ** -0.5

D_PROJ = 3 * QKV_W + 2 * D_MODEL
B_BLOCK0 = N_HEADS_A * HEAD_DIM // GROUP_W
GATE_BLOCK0 = 3 * QKV_W // D_MODEL

VMEM_LIMIT = 56 * 1024 * 1024

F32 = jnp.float32
BF16 = jnp.bfloat16


def _rope_tables(seq):
    pos = np.arange(seq, dtype=np.float64)
    inv = ROPE_THETA ** (-np.arange(0, HEAD_DIM, 2, dtype=np.float64) / HEAD_DIM)
    ang = pos[:, None] * inv[None, :]
    cos = np.concatenate([np.cos(ang), np.cos(ang)], axis=-1)
    sin = np.concatenate([-np.sin(ang), np.sin(ang)], axis=-1)
    return jnp.asarray(cos, F32), jnp.asarray(sin, F32)


def _rms(x, g):
    ms = jnp.mean(x * x, axis=-1, keepdims=True)
    return (x * lax.rsqrt(ms + EPS)) * g


QK_ROW_CHUNKS = 4


def _in_proj_kernel(*refs, n_casts):
    x_ref, g_ref, w_ref, rc_ref, rs_ref, gb_ref = refs[:6]
    cast_srcs = refs[6:6 + n_casts]
    o_ref = refs[6 + n_casts]
    cast_dsts = refs[7 + n_casts:7 + 2 * n_casts]
    xn_ref = refs[7 + 2 * n_casts]
    j = pl.program_id(1)

    @pl.when(pl.program_id(0) * pl.num_programs(1) + j < IN_PROJ_CAST_CHUNKS)
    def _():
        for src, dst in zip(cast_srcs, cast_dsts):
            dst[...] = src[...].astype(dst.dtype)

    def project(xn):
        return jnp.dot(xn, w_ref[...], preferred_element_type=F32)

    def qk_tile(xn):
        rows_per_chunk = xn.shape[0] // QK_ROW_CHUNKS
        for c in range(QK_ROW_CHUNKS):
            rows = slice(c * rows_per_chunk, (c + 1) * rows_per_chunk)
            acc = project(xn[rows, :])
            for h in range(N_HEADS):
                sl = slice(h * HEAD_DIM, (h + 1) * HEAD_DIM)
                y = acc[:, sl]
                inv_rms = lax.rsqrt(jnp.mean(y * y, axis=-1, keepdims=True) + EPS)
                if h < N_HEADS_A:
                    y = y * rc_ref[rows, :] + pltpu.roll(y, HEAD_DIM // 2, 1) * rs_ref[rows, :]
                else:
                    y = y * gb_ref[...]
                o_ref[rows, sl] = (y * inv_rms).astype(o_ref.dtype)

    @pl.when(j == 0)
    def _():
        xn = _rms(x_ref[...], g_ref[...]).astype(BF16)
        xn_ref[...] = xn
        qk_tile(xn)

    @pl.when(j == 1)
    def _():
        qk_tile(xn_ref[...])

    @pl.when(j >= 2)
    def _():
        o_ref[...] = project(xn_ref[...]).astype(o_ref.dtype)


def _row_chunk_spec(a, n_chunks, step):
    assert a.shape[0] % (n_chunks * 16) == 0
    return pl.BlockSpec((a.shape[0] // n_chunks, a.shape[1]),
                        lambda *ids: (jnp.minimum(step(*ids), n_chunks - 1), 0))


IN_PROJ_CAST_CHUNKS = 64


def _in_proj(x2, gain, w, rope_c, rope_s, gain_b, seq, side_casts, tm=1024):
    n_tok = x2.shape[0]
    tn = QKV_W
    assert seq % tm == 0 and D_PROJ % tn == 0
    tiles_per_seq = seq // tm
    n_i, n_j = n_tok // tm, D_PROJ // tn
    assert n_i * n_j >= IN_PROJ_CAST_CHUNKS

    def qk(j):
        return jnp.minimum(j, 1)

    cast_specs = [_row_chunk_spec(a, IN_PROJ_CAST_CHUNKS, lambda i, j: i * n_j + j) for a in side_casts]
    return pl.pallas_call(
        functools.partial(_in_proj_kernel, n_casts=len(side_casts)),
        name="in_proj",
        grid=(n_i, n_j),
        in_specs=[
            pl.BlockSpec((tm, D_MODEL), lambda i, j: (i, 0)),
            pl.BlockSpec((1, D_MODEL), lambda i, j: (0, 0)),
            pl.BlockSpec((D_MODEL, tn), lambda i, j: (0, j)),
            pl.BlockSpec((None, tm, HEAD_DIM), lambda i, j: (qk(j), i % tiles_per_seq, 0)),
            pl.BlockSpec((None, tm, HEAD_DIM), lambda i, j: (qk(j), i % tiles_per_seq, 0)),
            pl.BlockSpec((None, 1, HEAD_DIM), lambda i, j: (qk(j), 0, 0)),
        ] + cast_specs,
        out_specs=[pl.BlockSpec((tm, tn), lambda i, j: (i, j))] + cast_specs,
        out_shape=[jax.ShapeDtypeStruct((n_tok, D_PROJ), BF16)]
        + [jax.ShapeDtypeStruct(a.shape, BF16) for a in side_casts],
        scratch_shapes=[pltpu.VMEM((tm, D_MODEL), BF16)],
        compiler_params=pltpu.CompilerParams(
            dimension_semantics=("arbitrary", "arbitrary"), vmem_limit_bytes=VMEM_LIMIT),
    )(x2, gain, w, rope_c, rope_s, gain_b, *side_casts)


Q_BLK = 128
HALF_WIN = 64
A_WIN = 2 * Q_BLK
EXP_SPAN_LIMIT = 80.0


def _band_masks():
    rel = np.arange(A_WIN)[None, :] - np.arange(Q_BLK)[:, None]
    tiles = [np.where(np.abs(rel - off) <= HALF_WIN, 0.0, NEG_INF) for off in (0, HALF_WIN, 2 * HALF_WIN)]
    return jnp.asarray(np.stack(tiles), F32)


def _attn_a_kernel(*refs, seq, dils, n_casts):
    n_g = len(dils)
    qkv_refs = refs[:3 * n_g]
    bound_ref, mask_ref = refs[3 * n_g:3 * n_g + 2]
    rest = refs[3 * n_g + 2:]
    cast_srcs, o_ref, cast_dsts = rest[:n_casts], rest[n_casts], rest[n_casts + 1:2 * n_casts + 1]
    qn_ref, kn_ref, vn_ref, f_ref, og_ref, lg_ref, band_ref = rest[2 * n_casts + 1:]
    n_blk = seq // Q_BLK
    for src, dst in zip(cast_srcs, cast_dsts):
        dst[...] = src[...].astype(dst.dtype)
    bound = bound_ref[0]
    band_ref[...] = mask_ref[...] - bound
    vn_ref[:, :, HEAD_DIM:] = jnp.ones((n_g, seq, HEAD_DIM), BF16)

    def first_stride(dil):
        return 4 if dil % 8 == 0 else dil

    slabs = iter(range(f_ref.shape[0]))
    stage = {g: tuple(f_ref.at[next(slabs)] for _ in range(3)) for g, dil in enumerate(dils) if dil > 1}
    mid = {g: tuple(f_ref.at[next(slabs)] for _ in range(3))
           for g, dil in enumerate(dils) if first_stride(dil) != dil}

    def block_rows(n):
        return pl.ds(pl.multiple_of(n * Q_BLK, Q_BLK), Q_BLK)

    def token_rows(n, dil):
        if dil == 1:
            return block_rows(n)
        sub_len = seq // dil
        base = n * Q_BLK
        r = base // sub_len
        return pl.ds(r + (base - r * sub_len) * dil, Q_BLK, stride=dil)

    def stage_in(n, c):
        rows = block_rows(n)
        for g, dil in enumerate(dils):
            q_ref, k_ref, v_ref = qkv_refs[3 * g:3 * g + 3]
            if dil == 1:
                vn_ref[g, rows, :HEAD_DIM] = v_ref[rows, :]
            else:
                fq, fk, fv = stage[g]
                fq[rows, :] = q_ref[rows, :].astype(F32)
                fk[rows, :] = k_ref[rows, :].astype(F32)
                fv[rows, :] = v_ref[rows, :].astype(F32)
        return c

    lax.fori_loop(0, n_blk, stage_in, 0, unroll=2)

    def put_operands(g, dst, q, k, v):
        qn_ref[g, dst, :] = q.astype(BF16)
        kn_ref[g, dst, :] = k.astype(BF16)
        vn_ref[g, dst, :HEAD_DIM] = v.astype(BF16)

    def gather(n, c):
        dst = block_rows(n)
        for g, dil in enumerate(dils):
            if dil > 1:
                src = token_rows(n, first_stride(dil))
                q, k, v = (slab[src, :] for slab in stage[g])
                if g in mid:
                    for slab, x in zip(mid[g], (q, k, v)):
                        slab[dst, :] = x
                else:
                    put_operands(g, dst, q, k, v)
        return c

    lax.fori_loop(0, n_blk, gather, 0, unroll=2)

    def second_step_rows(n, g):
        dil = dils[g]
        s1 = first_stride(dil)
        sub_len = seq // dil
        base = n * Q_BLK
        r = base // sub_len
        start = (r % s1) * (seq // s1) + r // s1 + (base - r * sub_len) * (dil // s1)
        return pl.ds(start, Q_BLK, stride=dil // s1)

    def gather_second(n, c):
        for g in mid:
            put_operands(g, block_rows(n), *(slab[second_step_rows(n, g), :] for slab in mid[g]))
        return c

    if mid:
        lax.fori_loop(0, n_blk, gather_second, 0, unroll=2)

    def block(n, c, row_max):
        q0 = pl.multiple_of(n * Q_BLK, Q_BLK)
        for g, dil in enumerate(dils):
            sub_len = seq // dil
            win = min(A_WIN, sub_len)
            sub0 = (n // (sub_len // Q_BLK)) * sub_len
            q_loc = q0 - sub0
            k_loc = jnp.clip(q_loc - HALF_WIN, 0, sub_len - win)
            k0 = pl.multiple_of(sub0 + k_loc, HALF_WIN)
            band = band_ref[(q_loc - k_loc) // HALF_WIN, :, :win]
            q_src, k_src = (qkv_refs[3 * g], qkv_refs[3 * g + 1]) if dil == 1 else (qn_ref.at[g], kn_ref.at[g])
            q = q_src[pl.ds(q0, Q_BLK), :]
            k = k_src[pl.ds(k0, win), :]
            s = lax.dot_general(q, k, (((1,), (1,)), ((), ())), preferred_element_type=F32) + band
            if row_max:
                m = jnp.max(s, axis=-1, keepdims=True)
                s = s - m
            p = jnp.exp(s).astype(BF16)
            acc = jnp.dot(p, vn_ref[g, pl.ds(k0, win), :], preferred_element_type=F32)
            num, denom = acc[:, :HEAD_DIM], acc[:, HEAD_DIM:]
            if row_max:
                first, second = num * (1.0 / denom), jnp.log(denom) + m
            else:
                first, second = num, denom
            if g in mid:
                mid[g][0][pl.ds(q0, Q_BLK), :] = first
                mid[g][1][pl.ds(q0, Q_BLK), :] = second
            else:
                dst = token_rows(n, dil)
                og_ref.at[g][dst, :] = first
                lg_ref.at[g][dst, :] = second
        return c

    def scatter_first(n, c):
        for g in mid:
            dst = second_step_rows(n, g)
            stage[g][0][dst, :] = mid[g][0][block_rows(n), :]
            stage[g][1][dst, :] = mid[g][1][block_rows(n), :]
        return c

    def scatter_second(n, c):
        for g in mid:
            dst = token_rows(n, first_stride(dils[g]))
            og_ref.at[g][dst, :] = stage[g][0][block_rows(n), :]
            lg_ref.at[g][dst, :] = stage[g][1][block_rows(n), :]
        return c

    def combine(n, c, row_max):
        rows = block_rows(n)
        if row_max:
            lses = [lg_ref[g, rows, :] for g in range(n_g)]
            mx = functools.reduce(jnp.maximum, lses)
            es = [jnp.exp(x - mx) for x in lses]
            inv = 1.0 / functools.reduce(lambda a, b: a + b, es)
            oa = (es[0] * inv) * og_ref[0, rows, :]
            for g in range(1, n_g):
                oa = oa + (es[g] * inv) * og_ref[g, rows, :]
        else:
            num = functools.reduce(lambda a, b: a + b, [og_ref[g, rows, :] for g in range(n_g)])
            den = functools.reduce(lambda a, b: a + b, [lg_ref[g, rows, :] for g in range(n_g)])
            oa = num * (1.0 / den)
        o_ref[rows, :] = oa.astype(o_ref.dtype)
        return c

    def attend(row_max, unroll):
        lax.fori_loop(0, n_blk, functools.partial(block, row_max=row_max), 0, unroll=unroll)
        if mid:
            lax.fori_loop(0, n_blk, scatter_first, 0, unroll=2)
            lax.fori_loop(0, n_blk, scatter_second, 0, unroll=2)
        lax.fori_loop(0, n_blk, functools.partial(combine, row_max=row_max), 0, unroll=2)

    @pl.when(2.0 * bound < EXP_SPAN_LIMIT)
    def _():
        attend(row_max=False, unroll=16)

    @pl.when(2.0 * bound >= EXP_SPAN_LIMIT)
    def _():
        attend(row_max=True, unroll=2)


def _score_bound(gq, gk):
    return 1.01 * HEAD_DIM * jnp.max(jnp.abs(gq)) * jnp.max(jnp.abs(gk))


def _attn_a(proj3, bound, side_casts):
    batch, seq, _ = proj3.shape
    dils = tuple(d for _, d in DILATED_PAIRS)
    n_g = len(dils)
    for d in dils:
        assert d % 8 != 0 or (d % 4 == 0 and (d // 4) % 8 != 0)
    n_slabs = 3 * sum((d > 1) + (d % 8 == 0) for d in dils)
    masks = _band_masks()
    qkv_specs = [pl.BlockSpec((None, seq, HEAD_DIM),
                              lambda b, s, t=t, g=g: (b, 0, t * N_HEADS + g * HEADS_PER_GROUP + s))
                 for g in range(n_g) for t in range(3)]
    cast_specs = [_row_chunk_spec(a, batch * HEADS_PER_GROUP, lambda b, s: b * HEADS_PER_GROUP + s)
                  for a in side_casts]
    return pl.pallas_call(
        functools.partial(_attn_a_kernel, seq=seq, dils=dils, n_casts=len(side_casts)),
        name="attn_a",
        grid=(batch, HEADS_PER_GROUP),
        in_specs=qkv_specs + [pl.BlockSpec(memory_space=pltpu.SMEM),
                              pl.BlockSpec(masks.shape, lambda b, s: (0, 0, 0), pipeline_mode=pl.Buffered(1))]
        + cast_specs,
        out_specs=[pl.BlockSpec((None, seq, HEAD_DIM), lambda b, s: (b, 0, s))] + cast_specs,
        out_shape=[jax.ShapeDtypeStruct((batch, seq, GROUP_W), BF16)]
        + [jax.ShapeDtypeStruct(a.shape, BF16) for a in side_casts],
        scratch_shapes=[pltpu.VMEM((n_g, seq, HEAD_DIM), BF16)] * 2
        + [pltpu.VMEM((n_g, seq, 2 * HEAD_DIM), BF16)]
        + [pltpu.VMEM((n_slabs, seq, HEAD_DIM), F32)]
        + [pltpu.VMEM((n_g, seq, HEAD_DIM), F32)] * 2
        + [pltpu.VMEM(masks.shape, F32)],
        compiler_params=pltpu.CompilerParams(
            dimension_semantics=("parallel", "arbitrary"), vmem_limit_bytes=VMEM_LIMIT),
    )(*([proj3] * (3 * n_g)), bound.reshape(1).astype(F32), masks, *side_casts)


B_QROWS = 4
B_KROWS = B_QROWS + WIN_R
B_NQ = B_QROWS * GRID_W
B_NK = B_KROWS * GRID_W


def _window_row_offsets(rows):
    n_grp = rows // B_QROWS
    assert rows % B_QROWS == 0 and n_grp >= 3
    masked = 2 * WIN_R - 1
    table = []
    for i in (0, 1, n_grp - 1):
        r0 = i * B_QROWS
        ws = int(np.clip(r0 - WIN_R // 2, 0, rows - B_KROWS))
        per_q = []
        for rq in range(B_QROWS):
            r = r0 + rq
            rs = int(np.clip(r - WIN_R // 2, 0, rows - WIN_R))
            per_q.append([ws + jr - r + WIN_R - 1 if rs <= ws + jr < rs + WIN_R else masked
                          for jr in range(B_KROWS)])
        table.append(per_q)
    return table


def _attn_b_kernel(shift_ref, q_ref, k_ref, v_ref, tile_ref, o_ref, bias_ref, *, rows):
    n_grp = rows // B_QROWS
    shift = shift_ref[0]
    span = shift_ref[1]

    @pl.when(pl.program_id(0) == 0)
    def _():
        right = lax.broadcasted_iota(jnp.int32, (GRID_W, 2 * GRID_W), 1) >= GRID_W
        for case, per_q in enumerate(_window_row_offsets(rows)):
            for rq, offs in enumerate(per_q):
                for h in range(N_HEADS_B):
                    for j in range(0, B_KROWS, 2):
                        pair = jnp.where(right, tile_ref[h, offs[j + 1]], tile_ref[h, offs[j]])
                        bias_ref[h, case, rq * GRID_W:(rq + 1) * GRID_W,
                                 j * GRID_W:(j + 2) * GRID_W] = pair - shift

    def group(i, c, row_max):
        r0 = i * B_QROWS
        ws = jnp.clip(r0 - WIN_R // 2, 0, rows - B_KROWS)
        case = jnp.where(i == 0, 0, jnp.where(i == n_grp - 1, 2, 1))
        q0 = pl.multiple_of(r0 * GRID_W, B_NQ)
        k0 = pl.multiple_of(ws * GRID_W, GRID_W)
        for h in range(N_HEADS_B):
            sl = slice(h * HEAD_DIM, (h + 1) * HEAD_DIM)
            q = q_ref[pl.ds(q0, B_NQ), sl]
            k = k_ref[pl.ds(k0, B_NK), sl]
            v = v_ref[pl.ds(k0, B_NK), sl]
            s = lax.dot_general(q, k, (((1,), (1,)), ((), ())), preferred_element_type=F32)
            s = s + bias_ref[h, case]
            if row_max:
                s = s - jnp.max(s, axis=-1, keepdims=True)
            p = jnp.exp(s)
            l = jnp.sum(p, axis=-1, keepdims=True)
            acc = jnp.dot(p.astype(BF16), v, preferred_element_type=F32)
            o_ref[pl.ds(q0, B_NQ), sl] = (acc * (1.0 / l)).astype(o_ref.dtype)
        return c

    @pl.when(span < EXP_SPAN_LIMIT)
    def _():
        lax.fori_loop(0, n_grp, functools.partial(group, row_max=False), 0, unroll=4)

    @pl.when(span >= EXP_SPAN_LIMIT)
    def _():
        lax.fori_loop(0, n_grp, functools.partial(group, row_max=True), 0)


def _neighbourhood_bias_tiles(rpb):
    n_h = rpb.shape[0]
    c = np.arange(GRID_W)
    dc = np.clip(c[None, :] - c[:, None], -(WIN_C - 1), WIN_C - 1) + (WIN_C - 1)
    col_start = np.clip(c - WIN_C // 2, 0, GRID_W - WIN_C)
    col_ok = (c[None, :] >= col_start[:, None]) & (c[None, :] < col_start[:, None] + WIN_C)
    col_pick = np.zeros((2 * WIN_C - 1, GRID_W * GRID_W), np.float32)
    col_pick[dc.ravel(), np.arange(GRID_W * GRID_W)] = 1.0
    tiles = jnp.einsum("hab,bc->hac", rpb.astype(F32), col_pick, precision=lax.Precision.HIGHEST)
    tiles = jnp.where(col_ok.reshape(-1)[None, None], tiles, NEG_INF).reshape(n_h, -1, GRID_W, GRID_W)
    tiles = jnp.concatenate([tiles, jnp.full((n_h, 1, GRID_W, GRID_W), NEG_INF, F32)], axis=1)
    return jnp.concatenate([tiles, tiles], axis=-1)


def _attn_b(proj3, tiles, qk_bound, rpb):
    batch, seq, _ = proj3.shape
    rows = seq // GRID_W
    hi, lo = jnp.max(rpb).astype(F32), jnp.min(rpb).astype(F32)
    shift = jnp.stack([qk_bound + hi, 2.0 * qk_bound + (hi - lo)]).astype(F32)

    def qkv_spec(t):
        return pl.BlockSpec((None, seq, GROUP_W), lambda b: (b, 0, t * (QKV_W // GROUP_W) + B_BLOCK0))

    tile_spec = pl.BlockSpec(tiles.shape, lambda b: (0, 0, 0, 0), pipeline_mode=pl.Buffered(1))
    return pl.pallas_call(
        functools.partial(_attn_b_kernel, rows=rows),
        name="attn_b",
        grid=(batch,),
        in_specs=[pl.BlockSpec(memory_space=pltpu.SMEM), qkv_spec(0), qkv_spec(1), qkv_spec(2), tile_spec],
        out_specs=pl.BlockSpec((None, seq, GROUP_W), lambda b: (b, 0, 0)),
        out_shape=jax.ShapeDtypeStruct((batch, seq, GROUP_W), BF16),
        scratch_shapes=[pltpu.VMEM((N_HEADS_B, 3, B_NQ, B_NK), F32)],
        compiler_params=pltpu.CompilerParams(
            dimension_semantics=("arbitrary",), vmem_limit_bytes=VMEM_LIMIT),
    )(shift, proj3, proj3, proj3, tiles)


X_RING_SLOTS = 3


def _mix_out_kernel(oa_ref, ob_ref, ga_ref, gb_ref, bg_ref, x_hbm, pa_ref, pb_ref, wo_ref, cast_src,
                    h_ref, cast_dst, x_buf, x_sem):
    i = pl.program_id(0)
    n_steps = pl.num_programs(0)
    tm = h_ref.shape[0]
    ahead = X_RING_SLOTS - 1

    def x_copy(step):
        slot = step % X_RING_SLOTS
        rows = pl.ds(pl.multiple_of(step * tm, tm), tm)
        return pltpu.make_async_copy(x_hbm.at[rows, :], x_buf.at[slot], x_sem.at[slot])

    @pl.when(i == 0)
    def _():
        for s in range(ahead):
            x_copy(s).start()

    @pl.when(i + ahead < n_steps)
    def _():
        x_copy(i + ahead).start()

    x_copy(i).wait()
    x_ref = x_buf.at[i % X_RING_SLOTS]
    cast_dst[...] = cast_src[...].astype(cast_dst.dtype)
    ya = jnp.dot(oa_ref[...], pa_ref[...], preferred_element_type=F32)
    yb = jnp.dot(ob_ref[...], pb_ref[...], preferred_element_type=F32)
    ga = jax.nn.sigmoid(ga_ref[...].astype(F32) + bg_ref[:, :D_MODEL])
    gb = jax.nn.sigmoid(gb_ref[...].astype(F32) + bg_ref[:, D_MODEL:])
    mixed = (ga * ya + gb * yb).astype(BF16)
    h_ref[...] = x_ref[...] + jnp.dot(mixed, wo_ref[...], preferred_element_type=F32)


def _mix_out(oa, ob, proj, b_gate, x2, pa, pb, wo, side_cast, tm=512):
    n_tok = x2.shape[0]
    row512 = pl.BlockSpec((tm, GROUP_W), lambda i: (i, 0))
    const = pl.Buffered(1)
    cast_spec = _row_chunk_spec(side_cast, n_tok // tm, lambda i: i)
    return pl.pallas_call(
        _mix_out_kernel,
        name="mix_out",
        grid=(n_tok // tm,),
        in_specs=[row512, row512,
                  pl.BlockSpec((tm, D_MODEL), lambda i: (i, GATE_BLOCK0)),
                  pl.BlockSpec((tm, D_MODEL), lambda i: (i, GATE_BLOCK0 + 1)),
                  pl.BlockSpec((1, 2 * D_MODEL), lambda i: (0, 0)),
                  pl.BlockSpec(memory_space=pl.ANY),
                  pl.BlockSpec((GROUP_W, D_MODEL), lambda i: (0, 0), pipeline_mode=const),
                  pl.BlockSpec((GROUP_W, D_MODEL), lambda i: (0, 0), pipeline_mode=const),
                  pl.BlockSpec((D_MODEL, D_MODEL), lambda i: (0, 0), pipeline_mode=const),
                  cast_spec],
        out_specs=[pl.BlockSpec((tm, D_MODEL), lambda i: (i, 0)), cast_spec],
        out_shape=[jax.ShapeDtypeStruct((n_tok, D_MODEL), F32), jax.ShapeDtypeStruct(side_cast.shape, BF16)],
        scratch_shapes=[pltpu.VMEM((X_RING_SLOTS, tm, D_MODEL), F32), pltpu.SemaphoreType.DMA((X_RING_SLOTS,))],
        compiler_params=pltpu.CompilerParams(
            dimension_semantics=("arbitrary",), vmem_limit_bytes=VMEM_LIMIT),
    )(oa, ob, proj, proj, b_gate, x2, pa, pb, wo, side_cast)


def _ffn_kernel(h_ref, g_ref, wu_ref, wd_ref, o_ref, hn_ref):
    def partial_out(hn):
        u = jnp.maximum(jnp.dot(hn, wu_ref[...], preferred_element_type=F32), 0.0)
        return jnp.dot((u * u).astype(BF16), wd_ref[...], preferred_element_type=F32)

    @pl.when(pl.program_id(1) == 0)
    def _():
        h = h_ref[...]
        hn = _rms(h, g_ref[...]).astype(BF16)
        hn_ref[...] = hn
        o_ref[...] = h + partial_out(hn)

    @pl.when(pl.program_id(1) > 0)
    def _():
        o_ref[...] += partial_out(hn_ref[...])


def _ffn(h, gain, wu, wd, tm=512, tf=2048):
    n_tok = h.shape[0]
    return pl.pallas_call(
        _ffn_kernel,
        name="ffn",
        grid=(n_tok // tm, D_FF // tf),
        in_specs=[pl.BlockSpec((tm, D_MODEL), lambda i, f: (i, 0)),
                  pl.BlockSpec((1, D_MODEL), lambda i, f: (0, 0)),
                  pl.BlockSpec((D_MODEL, tf), lambda i, f: (0, f)),
                  pl.BlockSpec((tf, D_MODEL), lambda i, f: (f, 0))],
        out_specs=pl.BlockSpec((tm, D_MODEL), lambda i, f: (i, 0)),
        out_shape=jax.ShapeDtypeStruct((n_tok, D_MODEL), F32),
        scratch_shapes=[pltpu.VMEM((tm, D_MODEL), BF16)],
        compiler_params=pltpu.CompilerParams(
            dimension_semantics=("parallel", "arbitrary"), vmem_limit_bytes=VMEM_LIMIT),
    )(h, gain, wu, wd)


def kernel(x, norm_mix, w_in, b_gate, q_norm_a, k_norm_a, q_norm_b, k_norm_b, rpb_b,
           w_proj_a, w_proj_b, w_out, norm_ffn, w_up, w_down):
    batch, seq, d_model = x.shape
    assert d_model == D_MODEL and seq % (DILATED_PAIRS[-1][1] * Q_BLK) == 0
    for win, dil in DILATED_PAIRS:
        assert win // (2 * dil) == HALF_WIN
    depth = norm_mix.shape[0]
    cos, sin = _rope_tables(seq)
    h = x.reshape(batch * seq, D_MODEL)
    for l in range(depth):
        gains_a = jnp.stack([q_norm_a[l] * SCALE, k_norm_a[l]]).astype(F32)[:, None, :]
        rope_c = cos[None] * gains_a
        rope_s = sin[None] * jnp.roll(gains_a, HEAD_DIM // 2, axis=-1)
        gains_b = jnp.stack([q_norm_b[l] * SCALE, k_norm_b[l]]).astype(F32)[:, None, :]
        proj, w_up_l = _in_proj(h, norm_mix[l].reshape(1, D_MODEL), w_in[l].astype(BF16),
                                rope_c, rope_s, gains_b, seq, side_casts=(w_up[l],))
        proj3 = proj.reshape(batch, seq, D_PROJ)
        oa, pa_l, pb_l, wo_l = _attn_a(proj3, _score_bound(gains_a[0], gains_a[1]),
                                       side_casts=(w_proj_a[l], w_proj_b[l], w_out[l]))
        ob = _attn_b(proj3, _neighbourhood_bias_tiles(rpb_b[l]), _score_bound(gains_b[0], gains_b[1]), rpb_b[l])
        h, w_down_l = _mix_out(oa.reshape(batch * seq, GROUP_W), ob.reshape(batch * seq, GROUP_W), proj,
                               b_gate[l].reshape(1, 2 * D_MODEL), h,
                               pa_l, pb_l, wo_l, side_cast=w_down[l])
        h = _ffn(h, norm_ffn[l].reshape(1, D_MODEL), w_up_l, w_down_l)
    return h.reshape(batch, seq, D_MODEL)
```

```python
import functools

import numpy as np
import jax
import jax.numpy as jnp
from jax import lax
from jax.experimental import pallas as pl
from jax.experimental.pallas import tpu as pltpu

D_MODEL = 2048
HEAD_DIM = 128
N_HEADS = 16
N_HEADS_A = 12
N_HEADS_B = 4
DILATED_PAIRS = ((128, 1), (512, 4), (2048, 16))
N_GROUPS_A = len(DILATED_PAIRS)
HEADS_PER_GROUP = 4
GROUP_W = HEADS_PER_GROUP * HEAD_DIM
GRID_W = 64
WIN_R = 8
WIN_C = 16
QKV_W = N_HEADS * HEAD_DIM
D_FF = 4 * D_MODEL
ROPE_THETA = 10000.0
EPS = 1e-6
NEG_INF = -1e30
SCALE = HEAD_DIM ** -0.5

D_PROJ = 3 * QKV_W + 2 * D_MODEL
B_BLOCK0 = N_HEADS_A * HEAD_DIM // GROUP_W
GATE_BLOCK0 = 3 * QKV_W // D_MODEL

VMEM_LIMIT = 56 * 1024 * 1024

F32 = jnp.float32
BF16 = jnp.bfloat16


def _rope_tables(seq):
    pos = np.arange(seq, dtype=np.float64)
    inv = ROPE_THETA ** (-np.arange(0, HEAD_DIM, 2, dtype=np.float64) / HEAD_DIM)
    ang = pos[:, None] * inv[None, :]
    cos = np.concatenate([np.cos(ang), np.cos(ang)], axis=-1)
    sin = np.concatenate([-np.sin(ang), np.sin(ang)], axis=-1)
    return jnp.asarray(cos, F32), jnp.asarray(sin, F32)


def _rms(x, g):
    ms = jnp.mean(x * x, axis=-1, keepdims=True)
    return (x * lax.rsqrt(ms + EPS)) * g


QK_ROW_CHUNKS = 4


def _in_proj_kernel(*refs, n_casts):
    x_ref, g_ref, w_ref, rc_ref, rs_ref, gb_ref = refs[:6]
    cast_srcs = refs[6:6 + n_casts]
    o_ref = refs[6 + n_casts]
    cast_dsts = refs[7 + n_casts:7 + 2 * n_casts]
    xn_ref = refs[7 + 2 * n_casts]
    j = pl.program_id(1)

    @pl.when(pl.program_id(0) * pl.num_programs(1) + j < IN_PROJ_CAST_CHUNKS)
    def _():
        for src, dst in zip(cast_srcs, cast_dsts):
            dst[...] = src[...].astype(dst.dtype)

    def project(xn):
        return jnp.dot(xn, w_ref[...], preferred_element_type=F32)

    def qk_tile(xn):
        rows_per_chunk = xn.shape[0] // QK_ROW_CHUNKS
        for c in range(QK_ROW_CHUNKS):
            rows = slice(c * rows_per_chunk, (c + 1) * rows_per_chunk)
            acc = project(xn[rows, :])
            for h in range(N_HEADS):
                sl = slice(h * HEAD_DIM, (h + 1) * HEAD_DIM)
                y = acc[:, sl]
                inv_rms = lax.rsqrt(jnp.mean(y * y, axis=-1, keepdims=True) + EPS)
                if h < N_HEADS_A:
                    y = y * rc_ref[rows, :] + pltpu.roll(y, HEAD_DIM // 2, 1) * rs_ref[rows, :]
                else:
                    y = y * gb_ref[...]
                o_ref[rows, sl] = (y * inv_rms).astype(o_ref.dtype)

    @pl.when(j == 0)
    def _():
        xn = _rms(x_ref[...], g_ref[...]).astype(BF16)
        xn_ref[...] = xn
        qk_tile(xn)

    @pl.when(j == 1)
    def _():
        qk_tile(xn_ref[...])

    @pl.when(j >= 2)
    def _():
        o_ref[...] = project(xn_ref[...]).astype(o_ref.dtype)


def _row_chunk_spec(a, n_chunks, step):
    assert a.shape[0] % (n_chunks * 16) == 0
    return pl.BlockSpec((a.shape[0] // n_chunks, a.shape[1]),
                        lambda *ids: (jnp.minimum(step(*ids), n_chunks - 1), 0))


IN_PROJ_CAST_CHUNKS = 64


def _in_proj(x2, gain, w, rope_c, rope_s, gain_b, seq, side_casts, tm=1024):
    n_tok = x2.shape[0]
    tn = QKV_W
    assert seq % tm == 0 and D_PROJ % tn == 0
    tiles_per_seq = seq // tm
    n_i, n_j = n_tok // tm, D_PROJ // tn
    assert n_i * n_j >= IN_PROJ_CAST_CHUNKS

    def qk(j):
        return jnp.minimum(j, 1)

    cast_specs = [_row_chunk_spec(a, IN_PROJ_CAST_CHUNKS, lambda i, j: i * n_j + j) for a in side_casts]
    return pl.pallas_call(
        functools.partial(_in_proj_kernel, n_casts=len(side_casts)),
        name="in_proj",
        grid=(n_i, n_j),
        in_specs=[
            pl.BlockSpec((tm, D_MODEL), lambda i, j: (i, 0)),
            pl.BlockSpec((1, D_MODEL), lambda i, j: (0, 0)),
            pl.BlockSpec((D_MODEL, tn), lambda i, j: (0, j)),
            pl.BlockSpec((None, tm, HEAD_DIM), lambda i, j: (qk(j), i % tiles_per_seq, 0)),
            pl.BlockSpec((None, tm, HEAD_DIM), lambda i, j: (qk(j), i % tiles_per_seq, 0)),
            pl.BlockSpec((None, 1, HEAD_DIM), lambda i, j: (qk(j), 0, 0)),
        ] + cast_specs,
        out_specs=[pl.BlockSpec((tm, tn), lambda i, j: (i, j))] + cast_specs,
        out_shape=[jax.ShapeDtypeStruct((n_tok, D_PROJ), BF16)]
        + [jax.ShapeDtypeStruct(a.shape, BF16) for a in side_casts],
        scratch_shapes=[pltpu.VMEM((tm, D_MODEL), BF16)],
        compiler_params=pltpu.CompilerParams(
            dimension_semantics=("arbitrary", "arbitrary"), vmem_limit_bytes=VMEM_LIMIT),
    )(x2, gain, w, rope_c, rope_s, gain_b, *side_casts)


Q_BLK = 128
HALF_WIN = 64
A_WIN = 2 * Q_BLK
EXP_SPAN_LIMIT = 80.0


def _band_masks():
    rel = np.arange(A_WIN)[None, :] - np.arange(Q_BLK)[:, None]
    tiles = [np.where(np.abs(rel - off) <= HALF_WIN, 0.0, NEG_INF) for off in (0, HALF_WIN, 2 * HALF_WIN)]
    return jnp.asarray(np.stack(tiles), F32)


def _attn_a_kernel(*refs, seq, dils, n_casts):
    n_g = len(dils)
    qkv_refs = refs[:3 * n_g]
    bound_ref, mask_ref = refs[3 * n_g:3 * n_g + 2]
    rest = refs[3 * n_g + 2:]
    cast_srcs, o_ref, cast_dsts = rest[:n_casts], rest[n_casts], rest[n_casts + 1:2 * n_casts + 1]
    qn_ref, kn_ref, vn_ref, f_ref, og_ref, lg_ref, band_ref = rest[2 * n_casts + 1:]
    n_blk = seq // Q_BLK
    for src, dst in zip(cast_srcs, cast_dsts):
        dst[...] = src[...].astype(dst.dtype)
    bound = bound_ref[0]

    @pl.when((pl.program_id(0) == 0) & (pl.program_id(1) == 0))
    def _():
        band_ref[...] = mask_ref[...] - bound
        vn_ref[:, :, HEAD_DIM:] = jnp.ones((n_g, seq, HEAD_DIM), BF16)

    def first_stride(dil):
        return 4 if dil % 8 == 0 else dil

    slabs = iter(range(f_ref.shape[0]))
    stage = {g: tuple(f_ref.at[next(slabs)] for _ in range(3)) for g, dil in enumerate(dils) if dil > 1}
    mid = {g: tuple(f_ref.at[next(slabs)] for _ in range(3))
           for g, dil in enumerate(dils) if first_stride(dil) != dil}

    def block_rows(n):
        return pl.ds(pl.multiple_of(n * Q_BLK, Q_BLK), Q_BLK)

    def token_rows(n, dil):
        if dil == 1:
            return block_rows(n)
        sub_len = seq // dil
        base = n * Q_BLK
        r = base // sub_len
        return pl.ds(r + (base - r * sub_len) * dil, Q_BLK, stride=dil)

    def stage_in(n, c):
        rows = block_rows(n)
        for g, dil in enumerate(dils):
            q_ref, k_ref, v_ref = qkv_refs[3 * g:3 * g + 3]
            if dil == 1:
                vn_ref[g, rows, :HEAD_DIM] = v_ref[rows, :]
            else:
                fq, fk, fv = stage[g]
                fq[rows, :] = q_ref[rows, :].astype(F32)
                fk[rows, :] = k_ref[rows, :].astype(F32)
                fv[rows, :] = v_ref[rows, :].astype(F32)
        return c

    lax.fori_loop(0, n_blk, stage_in, 0, unroll=2)

    def put_operands(g, dst, q, k, v):
        qn_ref[g, dst, :] = q.astype(BF16)
        kn_ref[g, dst, :] = k.astype(BF16)
        vn_ref[g, dst, :HEAD_DIM] = v.astype(BF16)

    def gather(n, c):
        dst = block_rows(n)
        for g, dil in enumerate(dils):
            if dil > 1:
                src = token_rows(n, first_stride(dil))
                q, k, v = (slab[src, :] for slab in stage[g])
                if g in mid:
                    for slab, x in zip(mid[g], (q, k, v)):
                        slab[dst, :] = x
                else:
                    put_operands(g, dst, q, k, v)
        return c

    lax.fori_loop(0, n_blk, gather, 0, unroll=2)

    def second_step_rows(n, g):
        dil = dils[g]
        s1 = first_stride(dil)
        sub_len = seq // dil
        base = n * Q_BLK
        r = base // sub_len
        start = (r % s1) * (seq // s1) + r // s1 + (base - r * sub_len) * (dil // s1)
        return pl.ds(start, Q_BLK, stride=dil // s1)

    def gather_second(n, c):
        for g in mid:
            put_operands(g, block_rows(n), *(slab[second_step_rows(n, g), :] for slab in mid[g]))
        return c

    if mid:
        lax.fori_loop(0, n_blk, gather_second, 0, unroll=2)

    def block(n, c, row_max):
        q0 = pl.multiple_of(n * Q_BLK, Q_BLK)
        for g, dil in enumerate(dils):
            sub_len = seq // dil
            win = min(A_WIN, sub_len)
            sub0 = (n // (sub_len // Q_BLK)) * sub_len
            q_loc = q0 - sub0
            k_loc = jnp.clip(q_loc - HALF_WIN, 0, sub_len - win)
            k0 = pl.multiple_of(sub0 + k_loc, HALF_WIN)
            band = band_ref[(q_loc - k_loc) // HALF_WIN, :, :win]
            q_src, k_src = (qkv_refs[3 * g], qkv_refs[3 * g + 1]) if dil == 1 else (qn_ref.at[g], kn_ref.at[g])
            q = q_src[pl.ds(q0, Q_BLK), :]
            k = k_src[pl.ds(k0, win), :]
            s = lax.dot_general(q, k, (((1,), (1,)), ((), ())), preferred_element_type=F32) + band
            if row_max:
                m = jnp.max(s, axis=-1, keepdims=True)
                s = s - m
            p = jnp.exp(s).astype(BF16)
            acc = jnp.dot(p, vn_ref[g, pl.ds(k0, win), :], preferred_element_type=F32)
            num, denom = acc[:, :HEAD_DIM], acc[:, HEAD_DIM:]
            if row_max:
                first, second = num * (1.0 / denom), jnp.log(denom) + m
            else:
                first, second = num, denom
            if g in mid:
                mid[g][0][pl.ds(q0, Q_BLK), :] = first
                mid[g][1][pl.ds(q0, Q_BLK), :] = second
            else:
                dst = token_rows(n, dil)
                og_ref.at[g][dst, :] = first
                lg_ref.at[g][dst, :] = second
        return c

    def scatter_first(n, c):
        for g in mid:
            dst = second_step_rows(n, g)
            stage[g][0][dst, :] = mid[g][0][block_rows(n), :]
            stage[g][1][dst, :] = mid[g][1][block_rows(n), :]
        return c

    def scatter_second(n, c):
        for g in mid:
            dst = token_rows(n, first_stride(dils[g]))
            og_ref.at[g][dst, :] = stage[g][0][block_rows(n), :]
            lg_ref.at[g][dst, :] = stage[g][1][block_rows(n), :]
        return c

    def combine(n, c, row_max):
        rows = block_rows(n)
        if row_max:
            lses = [lg_ref[g, rows, :] for g in range(n_g)]
            mx = functools.reduce(jnp.maximum, lses)
            es = [jnp.exp(x - mx) for x in lses]
            inv = 1.0 / functools.reduce(lambda a, b: a + b, es)
            oa = (es[0] * inv) * og_ref[0, rows, :]
            for g in range(1, n_g):
                oa = oa + (es[g] * inv) * og_ref[g, rows, :]
        else:
            num = functools.reduce(lambda a, b: a + b, [og_ref[g, rows, :] for g in range(n_g)])
            den = functools.reduce(lambda a, b: a + b, [lg_ref[g, rows, :] for g in range(n_g)])
            oa = num * (1.0 / den)
        o_ref[rows, :] = oa.astype(o_ref.dtype)
        return c

    def attend(row_max, unroll):
        lax.fori_loop(0, n_blk, functools.partial(block, row_max=row_max), 0, unroll=unroll)
        if mid:
            lax.fori_loop(0, n_blk, scatter_first, 0, unroll=2)
            lax.fori_loop(0, n_blk, scatter_second, 0, unroll=2)
        lax.fori_loop(0, n_blk, functools.partial(combine, row_max=row_max), 0, unroll=2)

    @pl.when(2.0 * bound < EXP_SPAN_LIMIT)
    def _():
        attend(row_max=False, unroll=16)

    @pl.when(2.0 * bound >= EXP_SPAN_LIMIT)
    def _():
        attend(row_max=True, unroll=2)


def _score_bound(gq, gk):
    return 1.01 * HEAD_DIM * jnp.max(jnp.abs(gq)) * jnp.max(jnp.abs(gk))


def _attn_a(proj3, bound, side_casts):
    batch, seq, _ = proj3.shape
    dils = tuple(d for _, d in DILATED_PAIRS)
    n_g = len(dils)
    for d in dils:
        assert d % 8 != 0 or (d % 4 == 0 and (d // 4) % 8 != 0)
    n_slabs = 3 * sum((d > 1) + (d % 8 == 0) for d in dils)
    masks = _band_masks()
    qkv_specs = [pl.BlockSpec((None, seq, HEAD_DIM),
                              lambda b, s, t=t, g=g: (b, 0, t * N_HEADS + g * HEADS_PER_GROUP + s))
                 for g in range(n_g) for t in range(3)]
    cast_specs = [_row_chunk_spec(a, batch * HEADS_PER_GROUP, lambda b, s: b * HEADS_PER_GROUP + s)
                  for a in side_casts]
    return pl.pallas_call(
        functools.partial(_attn_a_kernel, seq=seq, dils=dils, n_casts=len(side_casts)),
        name="attn_a",
        grid=(batch, HEADS_PER_GROUP),
        in_specs=qkv_specs + [pl.BlockSpec(memory_space=pltpu.SMEM),
                              pl.BlockSpec(masks.shape, lambda b, s: (0, 0, 0), pipeline_mode=pl.Buffered(1))]
        + cast_specs,
        out_specs=[pl.BlockSpec((None, seq, HEAD_DIM), lambda b, s: (b, 0, s))] + cast_specs,
        out_shape=[jax.ShapeDtypeStruct((batch, seq, GROUP_W), BF16)]
        + [jax.ShapeDtypeStruct(a.shape, BF16) for a in side_casts],
        scratch_shapes=[pltpu.VMEM((n_g, seq, HEAD_DIM), BF16)] * 2
        + [pltpu.VMEM((n_g, seq, 2 * HEAD_DIM), BF16)]
        + [pltpu.VMEM((n_slabs, seq, HEAD_DIM), F32)]
        + [pltpu.VMEM((n_g, seq, HEAD_DIM), F32)] * 2
        + [pltpu.VMEM(masks.shape, F32)],
        compiler_params=pltpu.CompilerParams(
            dimension_semantics=("arbitrary", "arbitrary"), vmem_limit_bytes=VMEM_LIMIT),
    )(*([proj3] * (3 * n_g)), bound.reshape(1).astype(F32), masks, *side_casts)


B_QROWS = 4
B_KROWS = B_QROWS + WIN_R
B_NQ = B_QROWS * GRID_W
B_NK = B_KROWS * GRID_W


def _window_row_offsets(rows):
    n_grp = rows // B_QROWS
    assert rows % B_QROWS == 0 and n_grp >= 3
    masked = 2 * WIN_R - 1
    table = []
    for i in (0, 1, n_grp - 1):
        r0 = i * B_QROWS
        ws = int(np.clip(r0 - WIN_R // 2, 0, rows - B_KROWS))
        per_q = []
        for rq in range(B_QROWS):
            r = r0 + rq
            rs = int(np.clip(r - WIN_R // 2, 0, rows - WIN_R))
            per_q.append([ws + jr - r + WIN_R - 1 if rs <= ws + jr < rs + WIN_R else masked
                          for jr in range(B_KROWS)])
        table.append(per_q)
    return table


def _attn_b_kernel(shift_ref, q_ref, k_ref, v_ref, tile_ref, o_ref, bias_ref, *, rows):
    n_grp = rows // B_QROWS
    shift = shift_ref[0]
    span = shift_ref[1]

    @pl.when(pl.program_id(0) == 0)
    def _():
        right = lax.broadcasted_iota(jnp.int32, (GRID_W, 2 * GRID_W), 1) >= GRID_W
        for case, per_q in enumerate(_window_row_offsets(rows)):
            for rq, offs in enumerate(per_q):
                for h in range(N_HEADS_B):
                    for j in range(0, B_KROWS, 2):
                        pair = jnp.where(right, tile_ref[h, offs[j + 1]], tile_ref[h, offs[j]])
                        bias_ref[h, case, rq * GRID_W:(rq + 1) * GRID_W,
                                 j * GRID_W:(j + 2) * GRID_W] = pair - shift

    def group(i, c, row_max):
        r0 = i * B_QROWS
        ws = jnp.clip(r0 - WIN_R // 2, 0, rows - B_KROWS)
        case = jnp.where(i == 0, 0, jnp.where(i == n_grp - 1, 2, 1))
        q0 = pl.multiple_of(r0 * GRID_W, B_NQ)
        k0 = pl.multiple_of(ws * GRID_W, GRID_W)
        for h in range(N_HEADS_B):
            sl = slice(h * HEAD_DIM, (h + 1) * HEAD_DIM)
            q = q_ref[pl.ds(q0, B_NQ), sl]
            k = k_ref[pl.ds(k0, B_NK), sl]
            v = v_ref[pl.ds(k0, B_NK), sl]
            s = lax.dot_general(q, k, (((1,), (1,)), ((), ())), preferred_element_type=F32)
            s = s + bias_ref[h, case]
            if row_max:
                s = s - jnp.max(s, axis=-1, keepdims=True)
            p = jnp.exp(s)
            l = jnp.sum(p, axis=-1, keepdims=True)
            acc = jnp.dot(p.astype(BF16), v, preferred_element_type=F32)
            o_ref[pl.ds(q0, B_NQ), sl] = (acc * (1.0 / l)).astype(o_ref.dtype)
        return c

    @pl.when(span < EXP_SPAN_LIMIT)
    def _():
        lax.fori_loop(0, n_grp, functools.partial(group, row_max=False), 0, unroll=4)

    @pl.when(span >= EXP_SPAN_LIMIT)
    def _():
        lax.fori_loop(0, n_grp, functools.partial(group, row_max=True), 0)


def _neighbourhood_bias_tiles(rpb):
    n_h = rpb.shape[0]
    c = np.arange(GRID_W)
    dc = np.clip(c[None, :] - c[:, None], -(WIN_C - 1), WIN_C - 1) + (WIN_C - 1)
    col_start = np.clip(c - WIN_C // 2, 0, GRID_W - WIN_C)
    col_ok = (c[None, :] >= col_start[:, None]) & (c[None, :] < col_start[:, None] + WIN_C)
    col_pick = np.zeros((2 * WIN_C - 1, GRID_W * GRID_W), np.float32)
    col_pick[dc.ravel(), np.arange(GRID_W * GRID_W)] = 1.0
    tiles = jnp.einsum("hab,bc->hac", rpb.astype(F32), col_pick, precision=lax.Precision.HIGHEST)
    tiles = jnp.where(col_ok.reshape(-1)[None, None], tiles, NEG_INF).reshape(n_h, -1, GRID_W, GRID_W)
    tiles = jnp.concatenate([tiles, jnp.full((n_h, 1, GRID_W, GRID_W), NEG_INF, F32)], axis=1)
    return jnp.concatenate([tiles, tiles], axis=-1)


def _attn_b(proj3, tiles, qk_bound, rpb):
    batch, seq, _ = proj3.shape
    rows = seq // GRID_W
    hi, lo = jnp.max(rpb).astype(F32), jnp.min(rpb).astype(F32)
    shift = jnp.stack([qk_bound + hi, 2.0 * qk_bound + (hi - lo)]).astype(F32)

    def qkv_spec(t):
        return pl.BlockSpec((None, seq, GROUP_W), lambda b: (b, 0, t * (QKV_W // GROUP_W) + B_BLOCK0))

    tile_spec = pl.BlockSpec(tiles.shape, lambda b: (0, 0, 0, 0), pipeline_mode=pl.Buffered(1))
    return pl.pallas_call(
        functools.partial(_attn_b_kernel, rows=rows),
        name="attn_b",
        grid=(batch,),
        in_specs=[pl.BlockSpec(memory_space=pltpu.SMEM), qkv_spec(0), qkv_spec(1), qkv_spec(2), tile_spec],
        out_specs=pl.BlockSpec((None, seq, GROUP_W), lambda b: (b, 0, 0)),
        out_shape=jax.ShapeDtypeStruct((batch, seq, GROUP_W), BF16),
        scratch_shapes=[pltpu.VMEM((N_HEADS_B, 3, B_NQ, B_NK), F32)],
        compiler_params=pltpu.CompilerParams(
            dimension_semantics=("arbitrary",), vmem_limit_bytes=VMEM_LIMIT),
    )(shift, proj3, proj3, proj3, tiles)


def _mix_out_kernel(oa_ref, ob_ref, ga_ref, gb_ref, bg_ref, x_ref, pa_ref, pb_ref, wo_ref, cast_src,
                    h_ref, cast_dst):
    cast_dst[...] = cast_src[...].astype(cast_dst.dtype)
    ya = jnp.dot(oa_ref[...], pa_ref[...], preferred_element_type=F32)
    yb = jnp.dot(ob_ref[...], pb_ref[...], preferred_element_type=F32)
    ga = jax.nn.sigmoid(ga_ref[...].astype(F32) + bg_ref[:, :D_MODEL])
    gb = jax.nn.sigmoid(gb_ref[...].astype(F32) + bg_ref[:, D_MODEL:])
    mixed = (ga * ya + gb * yb).astype(BF16)
    h_ref[...] = x_ref[...] + jnp.dot(mixed, wo_ref[...], preferred_element_type=F32)


def _mix_out(oa, ob, proj, b_gate, x2, pa, pb, wo, side_cast, tm=512):
    n_tok = x2.shape[0]
    row512 = pl.BlockSpec((tm, GROUP_W), lambda i: (i, 0))
    const = pl.Buffered(1)
    cast_spec = _row_chunk_spec(side_cast, n_tok // tm, lambda i: i)
    return pl.pallas_call(
        _mix_out_kernel,
        name="mix_out",
        grid=(n_tok // tm,),
        in_specs=[row512, row512,
                  pl.BlockSpec((tm, D_MODEL), lambda i: (i, GATE_BLOCK0)),
                  pl.BlockSpec((tm, D_MODEL), lambda i: (i, GATE_BLOCK0 + 1)),
                  pl.BlockSpec((1, 2 * D_MODEL), lambda i: (0, 0)),
                  pl.BlockSpec((tm, D_MODEL), lambda i: (i, 0)),
                  pl.BlockSpec((GROUP_W, D_MODEL), lambda i: (0, 0), pipeline_mode=const),
                  pl.BlockSpec((GROUP_W, D_MODEL), lambda i: (0, 0), pipeline_mode=const),
                  pl.BlockSpec((D_MODEL, D_MODEL), lambda i: (0, 0), pipeline_mode=const),
                  cast_spec],
        out_specs=[pl.BlockSpec((tm, D_MODEL), lambda i: (i, 0)), cast_spec],
        out_shape=[jax.ShapeDtypeStruct((n_tok, D_MODEL), F32), jax.ShapeDtypeStruct(side_cast.shape, BF16)],
        compiler_params=pltpu.CompilerParams(
            dimension_semantics=("parallel",), vmem_limit_bytes=VMEM_LIMIT),
    )(oa, ob, proj, proj, b_gate, x2, pa, pb, wo, side_cast)


def _ffn_kernel(h_ref, g_ref, wu_ref, wd_ref, o_ref, hn_ref):
    def partial_out(hn):
        u = jnp.maximum(jnp.dot(hn, wu_ref[...], preferred_element_type=F32), 0.0)
        return jnp.dot((u * u).astype(BF16), wd_ref[...], preferred_element_type=F32)

    @pl.when(pl.program_id(1) == 0)
    def _():
        h = h_ref[...]
        hn = _rms(h, g_ref[...]).astype(BF16)
        hn_ref[...] = hn
        o_ref[...] = h + partial_out(hn)

    @pl.when(pl.program_id(1) > 0)
    def _():
        o_ref[...] += partial_out(hn_ref[...])


def _ffn(h, gain, wu, wd, tm=512, tf=2048):
    n_tok = h.shape[0]
    return pl.pallas_call(
        _ffn_kernel,
        name="ffn",
        grid=(n_tok // tm, D_FF // tf),
        in_specs=[pl.BlockSpec((tm, D_MODEL), lambda i, f: (i, 0)),
                  pl.BlockSpec((1, D_MODEL), lambda i, f: (0, 0)),
                  pl.BlockSpec((D_MODEL, tf), lambda i, f: (0, f)),
                  pl.BlockSpec((tf, D_MODEL), lambda i, f: (f, 0))],
        out_specs=pl.BlockSpec((tm, D_MODEL), lambda i, f: (i, 0)),
        out_shape=jax.ShapeDtypeStruct((n_tok, D_MODEL), F32),
        scratch_shapes=[pltpu.VMEM((tm, D_MODEL), BF16)],
        compiler_params=pltpu.CompilerParams(
            dimension_semantics=("parallel", "arbitrary"), vmem_limit_bytes=VMEM_LIMIT),
    )(h, gain, wu, wd)


def kernel(x, norm_mix, w_in, b_gate, q_norm_a, k_norm_a, q_norm_b, k_norm_b, rpb_b,
           w_proj_a, w_proj_b, w_out, norm_ffn, w_up, w_down):
    batch, seq, d_model = x.shape
    assert d_model == D_MODEL and seq % (DILATED_PAIRS[-1][1] * Q_BLK) == 0
    for win, dil in DILATED_PAIRS:
        assert win // (2 * dil) == HALF_WIN
    depth = norm_mix.shape[0]
    cos, sin = _rope_tables(seq)
    h = x.reshape(batch * seq, D_MODEL)
    for l in range(depth):
        gains_a = jnp.stack([q_norm_a[l] * SCALE, k_norm_a[l]]).astype(F32)[:, None, :]
        rope_c = cos[None] * gains_a
        rope_s = sin[None] * jnp.roll(gains_a, HEAD_DIM // 2, axis=-1)
        gains_b = jnp.stack([q_norm_b[l] * SCALE, k_norm_b[l]]).astype(F32)[:, None, :]
        proj, w_up_l = _in_proj(h, norm_mix[l].reshape(1, D_MODEL), w_in[l].astype(BF16),
                                rope_c, rope_s, gains_b, seq, side_casts=(w_up[l],))
        proj3 = proj.reshape(batch, seq, D_PROJ)
        oa, pa_l, pb_l, wo_l = _attn_a(proj3, _score_bound(gains_a[0], gains_a[1]),
                                       side_casts=(w_proj_a[l], w_proj_b[l], w_out[l]))
        ob = _attn_b(proj3, _neighbourhood_bias_tiles(rpb_b[l]), _score_bound(gains_b[0], gains_b[1]), rpb_b[l])
        h, w_down_l = _mix_out(oa.reshape(batch * seq, GROUP_W), ob.reshape(batch * seq, GROUP_W), proj,
                               b_gate[l].reshape(1, 2 * D_MODEL), h,
                               pa_l, pb_l, wo_l, side_cast=w_down[l])
        h = _ffn(h, norm_ffn[l].reshape(1, D_MODEL), w_up_l, w_down_l)
    return h.reshape(batch, seq, D_MODEL)
```
